```python
import math
import jax, jax.numpy as jnp
from jax import lax
import numpy as np

D_MODEL = 1024
BATCH = 8
SEQ = 2048
DEPTH = 1
DEC_BATCH = 128
DEC_SEQ = 4
PAST_LEN = 16384
PAGE_SIZE = 128

N_HEADS = 4
DK = D_MODEL // N_HEADS
DV = D_MODEL // N_HEADS
CONV_CH = D_MODEL
CONV_K = 31
D_FF = ((8 * D_MODEL // 3 + 127) // 128) * 128
FFN_K = 3
CHUNK = 128
EPS = 1e-6
IN_SIZES = (N_HEADS * DK, N_HEADS * DK, N_HEADS * DV, N_HEADS * DV,
            N_HEADS, N_HEADS,
            CONV_CH, CONV_CH,
            D_MODEL, D_MODEL)
IN_SPLITS = tuple(int(s) for s in np.cumsum(IN_SIZES)[:-1])
N_IN = int(sum(IN_SIZES))

kernel_name = "mlstm_conformer_gated_hybrid_step"


def rmsnorm(x, g):
    xf = x.astype(jnp.float32)
    y = xf * lax.rsqrt(jnp.mean(xf * xf, axis=-1, keepdims=True) + EPS)
    return (y * g.astype(jnp.float32)).astype(x.dtype)


def layernorm(x, g, b):
    xf = x.astype(jnp.float32)
    mu = jnp.mean(xf, axis=-1, keepdims=True)
    xc = xf - mu
    y = xc * lax.rsqrt(jnp.mean(xc * xc, axis=-1, keepdims=True) + EPS)
    return (y * g.astype(jnp.float32) + b.astype(jnp.float32)).astype(x.dtype)


def causal_dwconv(u, buf, w, b):
    full = jnp.concatenate([buf.astype(u.dtype), u], axis=1)
    y = lax.conv_general_dilated(full, w[:, None, :].astype(u.dtype), (1,), 'VALID',
                                 dimension_numbers=('NWC', 'WIO', 'NWC'),
                                 feature_group_count=u.shape[-1])
    return y + b.astype(u.dtype), full[:, full.shape[1] - (w.shape[0] - 1):]


def mlstm_chunk(carry, inp):
    C, n, m = carry
    q, k, v, ig, lf = inp
    L = q.shape[2]
    b = jnp.cumsum(lf, axis=-1)
    inter = b + m[..., None]
    D = b[..., :, None] - b[..., None, :] + ig[..., None, :]
    causal = jnp.tril(jnp.ones((L, L), dtype=bool))
    D = jnp.where(causal, D, -jnp.inf)
    m_t = jnp.maximum(inter, jnp.max(D, axis=-1))
    w_inter = jnp.exp(inter - m_t)
    s = jnp.einsum('bhtk,bhsk->bhts', q, k) * jnp.exp(D - m_t[..., None])
    num = w_inter[..., None] * jnp.einsum('bhvk,bhtk->bhtv', C, q) + jnp.einsum('bhts,bhsv->bhtv', s, v)
    den = w_inter * jnp.einsum('bhk,bhtk->bht', n, q) + jnp.sum(s, axis=-1)
    h = num / jnp.maximum(jnp.abs(den), jnp.exp(-m_t))[..., None]
    m_new = m_t[..., -1]
    g_inter = jnp.exp(b[..., -1] + m - m_new)
    g_s = jnp.exp(b[..., -1:] - b + ig - m_new[..., None])
    C_new = g_inter[..., None, None] * C + jnp.einsum('bhs,bhsv,bhsk->bhvk', g_s, v, k)
    n_new = g_inter[..., None] * n + jnp.einsum('bhs,bhsk->bhk', g_s, k)
    return (C_new, n_new, m_new), h


def mlstm(q, k, v, ig, lf, C, n, m):
    B, H, T, _ = q.shape
    L = math.gcd(T, CHUNK)
    nc = T // L

    def split(a):
        return jnp.moveaxis(a.reshape(a.shape[:2] + (nc, L) + a.shape[3:]), 2, 0)

    (C, n, m), h = lax.scan(mlstm_chunk, (C, n, m), tuple(split(a) for a in (q, k, v, ig, lf)))
    h = jnp.moveaxis(h, 0, 2).reshape(B, H, T, -1)
    return h, C, n, m


def layer(x, C, n, m, conv_buf, ffn_buf, mix_g, w_in, b_if, head_g, w_mo, conv_w, conv_b,
          cln_g, cln_b, w_co, w_out, ffn_g, w_up, fconv_w, fconv_b, w_down):
    B, T, _ = x.shape
    f32 = jnp.float32
    hn = rmsnorm(x, mix_g)
    z = hn @ w_in
    q, k, v, o, ig, fg, glu_v, glu_g, g_a, g_b = jnp.split(z, IN_SPLITS, axis=-1)

    def heads(a, d):
        return a.reshape(B, T, N_HEADS, d).transpose(0, 2, 1, 3).astype(f32)

    ig_t = (ig.astype(f32) + b_if[:N_HEADS].astype(f32)).transpose(0, 2, 1)
    lf_t = jax.nn.log_sigmoid(fg.astype(f32) + b_if[N_HEADS:].astype(f32)).transpose(0, 2, 1)
    hA, C_new, n_new, m_new = mlstm(heads(q, DK), heads(k, DK) * (DK ** -0.5), heads(v, DV),
                                     ig_t, lf_t, C.astype(f32), n.astype(f32), m.astype(f32))
    hA = hA.transpose(0, 2, 1, 3)
    mu = jnp.mean(hA, axis=-1, keepdims=True)
    hc = hA - mu
    hA = (hc * lax.rsqrt(jnp.mean(hc * hc, axis=-1, keepdims=True) + EPS)).reshape(B, T, N_HEADS * DV)
    hA = (hA * head_g.astype(f32) * jax.nn.sigmoid(o.astype(f32))).astype(x.dtype)
    yA = hA @ w_mo

    u = glu_v * jax.nn.sigmoid(glu_g)
    uc, conv_buf_new = causal_dwconv(u, conv_buf, conv_w, conv_b)
    uc = jax.nn.silu(layernorm(uc, cln_g, cln_b))
    yB = uc @ w_co

    mix = jax.nn.sigmoid(g_a) * yA + jax.nn.sigmoid(g_b) * yB
    x = x + mix @ w_out

    hf = rmsnorm(x, ffn_g)
    up = hf @ w_up
    upc, ffn_buf_new = causal_dwconv(up, ffn_buf, fconv_w, fconv_b)
    a, gt = jnp.split(upc, 2, axis=-1)
    x = x + (jax.nn.silu(a) * gt) @ w_down
    dt = x.dtype
    return x, C_new.astype(dt), n_new.astype(dt), m_new.astype(dt), conv_buf_new, ffn_buf_new


def setup_inputs(seed: int = 0) -> dict:
    key = jax.random.key(seed)
    ks = iter(jax.random.split(key, 40))
    f32 = jnp.float32

    def nrm(shape, scale):
        return jax.random.normal(next(ks), shape, f32) * scale

    def uni(shape, lo, hi):
        return jax.random.uniform(next(ks), shape, f32, lo, hi)

    H = N_HEADS
    b_if = jnp.concatenate([nrm((DEPTH, H), 0.1), uni((DEPTH, H), 3.0, 6.0)], axis=-1)
    return {
        "x_prompt": nrm((BATCH, SEQ, D_MODEL), 1.0),
        "x_sample": nrm((DEC_BATCH, DEC_SEQ, D_MODEL), 1.0),
        "state_C": nrm((DEPTH, DEC_BATCH, H, DV, DK), 0.1),
        "state_n": nrm((DEPTH, DEC_BATCH, H, DK), 0.1),
        "state_m": uni((DEPTH, DEC_BATCH, H), 0.0, 4.0),
        "state_conv": nrm((DEPTH, DEC_BATCH, CONV_K - 1, CONV_CH), 0.5),
        "state_ffn": nrm((DEPTH, DEC_BATCH, FFN_K - 1, 2 * D_FF), 0.5),
        "mix_norm_g": 1.0 + nrm((DEPTH, D_MODEL), 0.05),
        "w_in": nrm((DEPTH, D_MODEL, N_IN), D_MODEL ** -0.5),
        "b_if": b_if,
        "head_norm_g": 1.0 + nrm((DEPTH, H * DV), 0.05),
        "w_mlstm_out": nrm((DEPTH, H * DV, D_MODEL), (H * DV) ** -0.5),
        "conv_w": nrm((DEPTH, CONV_K, CONV_CH), CONV_K ** -0.5),
        "conv_b": nrm((DEPTH, CONV_CH), 0.02),
        "conv_ln_g": 1.0 + nrm((DEPTH, CONV_CH), 0.05),
        "conv_ln_b": nrm((DEPTH, CONV_CH), 0.02),
        "w_conv_out": nrm((DEPTH, CONV_CH, D_MODEL), CONV_CH ** -0.5),
        "w_out": nrm((DEPTH, D_MODEL, D_MODEL), D_MODEL ** -0.5),
        "ffn_norm_g": 1.0 + nrm((DEPTH, D_MODEL), 0.05),
        "w_up": nrm((DEPTH, D_MODEL, 2 * D_FF), D_MODEL ** -0.5),
        "ffn_conv_w": nrm((DEPTH, FFN_K, 2 * D_FF), FFN_K ** -0.5),
        "ffn_conv_b": nrm((DEPTH, 2 * D_FF), 0.02),
        "w_down": nrm((DEPTH, D_FF, D_MODEL), D_FF ** -0.5),
        "final_norm_g": 1.0 + nrm((D_MODEL,), 0.05),
    }


def reference(x_prompt, x_sample, state_C, state_n, state_m, state_conv, state_ffn,
              mix_norm_g, w_in, b_if, head_norm_g, w_mlstm_out, conv_w, conv_b, conv_ln_g,
              conv_ln_b, w_conv_out, w_out, ffn_norm_g, w_up, ffn_conv_w, ffn_conv_b, w_down,
              final_norm_g):
    f32 = jnp.float32
    B = x_prompt.shape[0]
    xp, xs = x_prompt, x_sample
    Cp_l, np_l, mp_l, cp_l, fp_l = [], [], [], [], []
    Cs_l, ns_l, ms_l, cs_l, fs_l = [], [], [], [], []
    for l in range(DEPTH):
        w = (mix_norm_g[l], w_in[l], b_if[l], head_norm_g[l], w_mlstm_out[l], conv_w[l], conv_b[l],
             conv_ln_g[l], conv_ln_b[l], w_conv_out[l], w_out[l], ffn_norm_g[l], w_up[l],
             ffn_conv_w[l], ffn_conv_b[l], w_down[l])
        xp, Cp, n_p, mp, cp, fp = layer(
            xp,
            jnp.zeros((B, N_HEADS, DV, DK), f32), jnp.zeros((B, N_HEADS, DK), f32),
            jnp.zeros((B, N_HEADS), f32),
            jnp.zeros((B, CONV_K - 1, CONV_CH), xp.dtype),
            jnp.zeros((B, FFN_K - 1, 2 * D_FF), xp.dtype), *w)
        xs, Cs, n_s, ms, cs, fs = layer(xs, state_C[l], state_n[l], state_m[l], state_conv[l],
                                        state_ffn[l], *w)
        Cp_l.append(Cp); np_l.append(n_p); mp_l.append(mp); cp_l.append(cp); fp_l.append(fp)
        Cs_l.append(Cs); ns_l.append(n_s); ms_l.append(ms); cs_l.append(cs); fs_l.append(fs)
    y_prompt = rmsnorm(xp, final_norm_g)
    y_sample = rmsnorm(xs, final_norm_g)
    return (y_prompt, y_sample,
            jnp.stack(Cp_l), jnp.stack(np_l), jnp.stack(mp_l), jnp.stack(cp_l), jnp.stack(fp_l),
            jnp.stack(Cs_l), jnp.stack(ns_l), jnp.stack(ms_l), jnp.stack(cs_l), jnp.stack(fs_l))
```

```python
import functools

import jax
import jax.numpy as jnp
from jax import lax
from jax.experimental import pallas as pl
from jax.experimental.pallas import tpu as pltpu

F32 = jnp.float32
BF16 = jnp.bfloat16

D_MODEL = 1024
N_HEADS = 4
DH = D_MODEL // N_HEADS
CONV_K = 31
D_FF = 2816
FFN_K = 3
EPS = 1e-6

LANES = 128
Q0, K0, V0, O0, GV0, GG0, GA0, GB0 = (i * D_MODEL for i in range(8))
IG0 = 8 * D_MODEL
FG0 = IG0 + LANES
N_Z = FG0 + LANES

PROMPT_CHUNK = 256
SAMPLE_ROWS = 128
MLSTM_SEQS = 8
VMEM_LIMIT = 56 * 1024 * 1024


def _mm(a, b):
    return jnp.dot(a, b, preferred_element_type=F32)


def _mm_nt(a, b):
    return lax.dot_general(a, b, (((1,), (1,)), ((), ())), preferred_element_type=F32)


def _mm_tn(a, b):
    return lax.dot_general(a, b, (((0,), (0,)), ((), ())), preferred_element_type=F32)


def _rmsnorm(x, g):
    return x * lax.rsqrt(jnp.mean(x * x, axis=-1, keepdims=True) + EPS) * g


def _layernorm(x, g, b):
    mu = jnp.mean(x, axis=-1, keepdims=True)
    xc = x - mu
    return xc * lax.rsqrt(jnp.mean(xc * xc, axis=-1, keepdims=True) + EPS) * g + b


def _sigmoid(x):
    return 1.0 / (1.0 + jnp.exp(-x))


def _log_sigmoid(x):
    return jnp.minimum(x, 0.0) - jnp.log1p(jnp.exp(-jnp.abs(x)))


def _cumsum_rows(x, period=None):
    n = x.shape[0]
    row = lax.broadcasted_iota(jnp.int32, x.shape, 0)
    pos = row if period is None else row % period
    span = n if period is None else period
    s = 1
    while s < span:
        x = x + jnp.where(pos >= s, pltpu.roll(x, s, axis=0), 0.0)
        s *= 2
    return x


def _mlstm_head(q, k, v, b_col, a_col, a_row, m_prev, c_prev, n_prev, n_valid):
    L, LK = q.shape[0], k.shape[0]
    row = lax.broadcasted_iota(jnp.int32, (L, LK), 0)
    col = lax.broadcasted_iota(jnp.int32, (L, LK), 1)
    d = jnp.where(col <= row, b_col + a_row, -jnp.inf)
    inter = b_col + m_prev
    m_t = jnp.maximum(inter, jnp.max(d, axis=1, keepdims=True))
    w_inter = jnp.exp(inter - m_t)
    p = jnp.exp(d - m_t)
    qb, kb, vb = q.astype(BF16), k.astype(BF16), v.astype(BF16)
    s = _mm_nt(qb, kb) * p
    num = w_inter * _mm_nt(qb, c_prev.astype(BF16)) + _mm(s.astype(BF16), vb)
    den = w_inter * jnp.sum(q * n_prev, axis=1, keepdims=True) + jnp.sum(s, axis=1, keepdims=True)
    h = num / jnp.maximum(jnp.abs(den), jnp.exp(-m_t))
    m_new = m_t[n_valid - 1:n_valid, :]
    b_last = b_col[n_valid - 1:n_valid, :]
    g_inter = jnp.exp(b_last + m_prev - m_new)
    g_col = jnp.exp(b_last + a_col - m_new)
    if n_valid < LK:
        g_col = jnp.where(lax.broadcasted_iota(jnp.int32, (LK, 1), 0) < n_valid, g_col, 0.0)
    c_new = g_inter * c_prev + _mm_tn((g_col * v).astype(BF16), kb)
    n_new = g_inter * n_prev + jnp.sum(g_col * k, axis=0, keepdims=True)
    return h, c_new, n_new, m_new


def _head_out(h, head_g, o):
    mu = jnp.mean(h, axis=-1, keepdims=True)
    hc = h - mu
    hn = hc * lax.rsqrt(jnp.mean(hc * hc, axis=-1, keepdims=True) + EPS)
    return hn * head_g * _sigmoid(o)


def _prompt_mixer_kernel(x_ref, mixg_ref, w_ref, bi_ref, bf_ref, hg_ref, wmo_ref, cw_ref, cb_ref,
                         lng_ref, lnb_ref, wco_ref, wout_ref,
                         x1_ref, c_ref, n_ref, m_ref, conv_ref,
                         xn_s, ha_s, full_s, uc_s, m_s, *, n_chunks):
    T = x_ref.shape[1]
    c = pl.program_id(1)

    @pl.when(c == 0)
    def _():
        c_ref[...] = jnp.zeros_like(c_ref)
        n_ref[...] = jnp.zeros_like(n_ref)
        m_s[...] = jnp.zeros_like(m_s)
        full_s[0:32, :] = jnp.zeros((32, D_MODEL), F32)

    x = x_ref[0]
    xn_s[...] = _rmsnorm(x, mixg_ref[...]).astype(BF16)
    xn = xn_s[...]

    zi = _mm(xn, w_ref[:, IG0:IG0 + LANES]) + bi_ref[...]
    lf = _log_sigmoid(_mm(xn, w_ref[:, FG0:FG0 + LANES]) + bf_ref[...])
    b_all = _cumsum_rows(lf)
    a_all = zi - b_all
    a_t = a_all.T

    for h in range(N_HEADS):
        cs = slice(h * DH, (h + 1) * DH)
        q = _mm(xn, w_ref[:, Q0 + h * DH:Q0 + (h + 1) * DH])
        k = _mm(xn, w_ref[:, K0 + h * DH:K0 + (h + 1) * DH]) * (DH ** -0.5)
        v = _mm(xn, w_ref[:, V0 + h * DH:V0 + (h + 1) * DH])
        hh, c_new, n_new, m_new = _mlstm_head(
            q, k, v, b_all[:, h:h + 1], a_all[:, h:h + 1], a_t[h:h + 1, :],
            m_s[h:h + 1, 0:1], c_ref[0, h], n_ref[0, h:h + 1, :], T)
        c_ref[0, h] = c_new
        n_ref[0, h:h + 1, :] = n_new
        m_s[h:h + 1, :] = jnp.broadcast_to(m_new, (1, LANES))
        o = _mm(xn, w_ref[:, O0 + h * DH:O0 + (h + 1) * DH])
        ha_s[:, cs] = _head_out(hh, hg_ref[:, cs], o).astype(BF16)

    y_a = _mm(ha_s[...], wmo_ref[...])

    for cb in range(D_MODEL // DH):
        cs = slice(cb * DH, (cb + 1) * DH)
        gv = _mm(xn, w_ref[:, GV0 + cb * DH:GV0 + (cb + 1) * DH])
        gg = _mm(xn, w_ref[:, GG0 + cb * DH:GG0 + (cb + 1) * DH])
        full_s[32:32 + T, cs] = gv * _sigmoid(gg)
        acc = cb_ref[:, cs] + cw_ref[0:1, cs] * full_s[2:2 + T, cs]
        for j in range(1, CONV_K):
            acc = acc + cw_ref[j:j + 1, cs] * full_s[2 + j:2 + j + T, cs]
        uc_s[:, cs] = acc
    tail = full_s[T + 2:T + 32, :]
    conv_ref[0] = tail
    full_s[2:32, :] = tail

    ucn = _layernorm(uc_s[...], lng_ref[...], lnb_ref[...])
    y_b = _mm((ucn * _sigmoid(ucn)).astype(BF16), wco_ref[...])

    ga = _mm(xn, w_ref[:, GA0:GA0 + D_MODEL])
    gb = _mm(xn, w_ref[:, GB0:GB0 + D_MODEL])
    mix = _sigmoid(ga) * y_a + _sigmoid(gb) * y_b
    x1_ref[0] = x + _mm(mix.astype(BF16), wout_ref[...])

    @pl.when(c == n_chunks - 1)
    def _():
        for h in range(N_HEADS):
            m_ref[0, :, h:h + 1] = m_s[h:h + 1, 0:1]


def _const_spec(shape):
    nd = len(shape)
    return pl.BlockSpec(shape, lambda *_: (0,) * nd, pipeline_mode=pl.Buffered(1))


def _prompt_mixer(x, mixg, w_all, bi, bf, hg, wmo, cw, cb, lng, lnb, wco, wout):
    B, S, D = x.shape
    T = PROMPT_CHUNK
    nc = S // T
    kern = functools.partial(_prompt_mixer_kernel, n_chunks=nc)
    return pl.pallas_call(
        kern,
        grid=(B, nc),
        in_specs=[
            pl.BlockSpec((1, T, D), lambda b, c: (b, c, 0)),
            _const_spec((1, D)), _const_spec((D, N_Z)), _const_spec((1, LANES)), _const_spec((1, LANES)),
            _const_spec((1, D)), _const_spec((D, D)), _const_spec((CONV_K, D)), _const_spec((1, D)),
            _const_spec((1, D)), _const_spec((1, D)), _const_spec((D, D)), _const_spec((D, D)),
        ],
        out_specs=[
            pl.BlockSpec((1, T, D), lambda b, c: (b, c, 0)),
            pl.BlockSpec((1, N_HEADS, DH, DH), lambda b, c: (b, 0, 0, 0)),
            pl.BlockSpec((1, N_HEADS, DH), lambda b, c: (b, 0, 0)),
            pl.BlockSpec((1, 1, N_HEADS), lambda b, c: (b, 0, 0)),
            pl.BlockSpec((1, CONV_K - 1, D), lambda b, c: (b, 0, 0)),
        ],
        out_shape=[
            jax.ShapeDtypeStruct((B, S, D), F32),
            jax.ShapeDtypeStruct((B, N_HEADS, DH, DH), F32),
            jax.ShapeDtypeStruct((B, N_HEADS, DH), F32),
            jax.ShapeDtypeStruct((B, 1, N_HEADS), F32),
            jax.ShapeDtypeStruct((B, CONV_K - 1, D), F32),
        ],
        scratch_shapes=[
            pltpu.VMEM((T, D), BF16), pltpu.VMEM((T, D), BF16),
            pltpu.VMEM((T + 32, D), F32), pltpu.VMEM((T, D), F32),
            pltpu.VMEM((8, LANES), F32),
        ],
        compiler_params=pltpu.CompilerParams(
            dimension_semantics=("arbitrary", "arbitrary"), vmem_limit_bytes=VMEM_LIMIT),
        name="prompt_mixer",
    )(x, mixg, w_all, bi, bf, hg, wmo, cw, cb, lng, lnb, wco, wout)


def _ffn_tail(upc_a, upc_g, wdown_ref, x, fing_ref):
    act = (upc_a * _sigmoid(upc_a) * upc_g).astype(BF16)
    x2 = x + _mm(act, wdown_ref[...])
    return _rmsnorm(x2, fing_ref[...])


def _prompt_ffn_kernel(x_ref, fg_ref, wup_ref, fw_ref, fb_ref, wdown_ref, fing_ref,
                       y_ref, ffn_ref, up_s, act_s):
    T = x_ref.shape[1]
    c = pl.program_id(1)

    @pl.when(c == 0)
    def _():
        up_s[0:8, :] = jnp.zeros((8, 2 * D_FF), F32)

    x = x_ref[0]
    hf = _rmsnorm(x, fg_ref[...]).astype(BF16)
    nb = 2 * D_FF // DH
    for blk in range(nb):
        cs = slice(blk * DH, (blk + 1) * DH)
        up_s[8:8 + T, cs] = _mm(hf, wup_ref[:, cs])
    for blk in range(D_FF // DH):
        ca = slice(blk * DH, (blk + 1) * DH)
        cg = slice(D_FF + blk * DH, D_FF + (blk + 1) * DH)
        a = fb_ref[:, ca] + fw_ref[0:1, ca] * up_s[6:6 + T, ca] + fw_ref[1:2, ca] * up_s[7:7 + T, ca] \
            + fw_ref[2:3, ca] * up_s[8:8 + T, ca]
        g = fb_ref[:, cg] + fw_ref[0:1, cg] * up_s[6:6 + T, cg] + fw_ref[1:2, cg] * up_s[7:7 + T, cg] \
            + fw_ref[2:3, cg] * up_s[8:8 + T, cg]
        act_s[:, ca] = (a * _sigmoid(a) * g).astype(BF16)
    tail = up_s[T + 6:T + 8, :]
    ffn_ref[0] = tail
    up_s[6:8, :] = tail
    x2 = x + _mm(act_s[...], wdown_ref[...])
    y_ref[0] = _rmsnorm(x2, fing_ref[...])


def _prompt_ffn(x1, fg, wup, fw, fb, wdown, fing):
    B, S, D = x1.shape
    T = PROMPT_CHUNK
    nc = S // T
    return pl.pallas_call(
        _prompt_ffn_kernel,
        grid=(B, nc),
        in_specs=[
            pl.BlockSpec((1, T, D), lambda b, c: (b, c, 0)),
            _const_spec((1, D)), _const_spec((D, 2 * D_FF)), _const_spec((FFN_K, 2 * D_FF)),
            _const_spec((1, 2 * D_FF)), _const_spec((D_FF, D)), _const_spec((1, D)),
        ],
        out_specs=[
            pl.BlockSpec((1, T, D), lambda b, c: (b, c, 0)),
            pl.BlockSpec((1, FFN_K - 1, 2 * D_FF), lambda b, c: (b, 0, 0)),
        ],
        out_shape=[
            jax.ShapeDtypeStruct((B, S, D), F32),
            jax.ShapeDtypeStruct((B, FFN_K - 1, 2 * D_FF), F32),
        ],
        scratch_shapes=[pltpu.VMEM((T + 8, 2 * D_FF), F32), pltpu.VMEM((T, D_FF), BF16)],
        compiler_params=pltpu.CompilerParams(
            dimension_semantics=("arbitrary", "arbitrary"), vmem_limit_bytes=VMEM_LIMIT),
        name="prompt_ffn",
    )(x1, fg, wup, fw, fb, wdown, fing)


def _sample_proj_kernel(x_ref, mixg_ref, w_ref, z_ref):
    xn = _rmsnorm(x_ref[...], mixg_ref[...]).astype(BF16)
    for blk in range(N_Z // DH):
        cs = slice(blk * DH, (blk + 1) * DH)
        z_ref[:, cs] = _mm(xn, w_ref[:, cs])


def _sample_proj(x, mixg, w_all):
    R, D = x.shape
    return pl.pallas_call(
        _sample_proj_kernel,
        grid=(R // SAMPLE_ROWS,),
        in_specs=[pl.BlockSpec((SAMPLE_ROWS, D), lambda i: (i, 0)),
                  _const_spec((1, D)), _const_spec((D, N_Z))],
        out_specs=pl.BlockSpec((SAMPLE_ROWS, N_Z), lambda i: (i, 0)),
        out_shape=jax.ShapeDtypeStruct((R, N_Z), F32),
        compiler_params=pltpu.CompilerParams(
            dimension_semantics=("arbitrary",), vmem_limit_bytes=VMEM_LIMIT),
        name="sample_proj",
    )(x, mixg, w_all)


def _sample_mlstm_kernel(zq_ref, zg_ref, bi_ref, bf_ref, c_ref, n_ref, m_ref,
                         h_ref, co_ref, no_ref, mo_ref, *, t_seq):
    per_win = 8 // t_seq
    pad = LANES - 8
    for win in range(MLSTM_SEQS // per_win):
        zq_w = zq_ref[8 * win:8 * win + 8, :]
        zg_w = zg_ref[8 * win:8 * win + 8, :]
        for sub in range(per_win):
            j = win * per_win + sub
            if sub:
                zq = pltpu.roll(zq_w, 8 - sub * t_seq, axis=0)
                zg = pltpu.roll(zg_w, 8 - sub * t_seq, axis=0)
            else:
                zq, zg = zq_w, zg_w
            zi = zg[:, 0:LANES] + bi_ref[...]
            lf = _log_sigmoid(zg[:, LANES:2 * LANES] + bf_ref[...])
            b_all = _cumsum_rows(lf)
            a_all = jnp.concatenate([zi - b_all, jnp.zeros((pad, LANES), F32)], axis=0)
            a_t = a_all.T
            for h in range(N_HEADS):
                q = zq[:, Q0 + h * DH:Q0 + (h + 1) * DH]
                k = zq[:, K0 + h * DH:K0 + (h + 1) * DH] * (DH ** -0.5)
                v = zq[:, V0 + h * DH:V0 + (h + 1) * DH]
                zeros = jnp.zeros((pad, DH), F32)
                k = jnp.concatenate([k, zeros], axis=0)
                v = jnp.concatenate([v, zeros], axis=0)
                hh, c_new, n_new, m_new = _mlstm_head(
                    q, k, v, b_all[:, h:h + 1], a_all[:, h:h + 1], a_t[h:h + 1, :],
                    m_ref[j:j + 1, h:h + 1], c_ref[j, h], n_ref[j, h:h + 1, :], t_seq)
                co_ref[j, h] = c_new
                no_ref[j, h:h + 1, :] = n_new
                mo_ref[j:j + 1, h:h + 1] = m_new
                h_ref[j * t_seq:(j + 1) * t_seq, h * DH:(h + 1) * DH] = hh[0:t_seq, :]


def _sample_mlstm(z, bi, bf, c0, n0, m0, t_seq):
    nb = c0.shape[0]
    sb = MLSTM_SEQS
    rows = sb * t_seq
    kern = functools.partial(_sample_mlstm_kernel, t_seq=t_seq)
    return pl.pallas_call(
        kern,
        grid=(nb // sb,),
        in_specs=[
            pl.BlockSpec((rows, 3 * D_MODEL), lambda i: (i, 0)),
            pl.BlockSpec((rows, 2 * LANES), lambda i: (i, IG0 // (2 * LANES))),
            _const_spec((1, LANES)), _const_spec((1, LANES)),
            pl.BlockSpec((sb, N_HEADS, DH, DH), lambda i: (i, 0, 0, 0)),
            pl.BlockSpec((sb, N_HEADS, DH), lambda i: (i, 0, 0)),
            pl.BlockSpec((sb, N_HEADS), lambda i: (i, 0)),
        ],
        out_specs=[
            pl.BlockSpec((rows, D_MODEL), lambda i: (i, 0)),
            pl.BlockSpec((sb, N_HEADS, DH, DH), lambda i: (i, 0, 0, 0)),
            pl.BlockSpec((sb, N_HEADS, DH), lambda i: (i, 0, 0)),
            pl.BlockSpec((sb, N_HEADS), lambda i: (i, 0)),
        ],
        out_shape=[
            jax.ShapeDtypeStruct((nb * t_seq, D_MODEL), F32),
            jax.ShapeDtypeStruct(c0.shape, F32),
            jax.ShapeDtypeStruct(n0.shape, F32),
            jax.ShapeDtypeStruct(m0.shape, F32),
        ],
        compiler_params=pltpu.CompilerParams(
            dimension_semantics=("arbitrary",), vmem_limit_bytes=VMEM_LIMIT),
        name="sample_mlstm",
    )(z, z, bi, bf, c0, n0, m0)


def _sample_mixer_tail_kernel(x_ref, z_ref, h_ref, cst_ref, hg_ref, wmo_ref, cw4_ref, cb_ref,
                              lng_ref, lnb_ref, wco_ref, wout_ref,
                              x1_ref, cnew_ref, full_s, uc_s, *, t_seq):
    R = x_ref.shape[0]
    n_seq = R // t_seq
    hist = CONV_K - 1
    ha = jnp.concatenate(
        [_head_out(h_ref[:, h * DH:(h + 1) * DH], hg_ref[:, h * DH:(h + 1) * DH],
                   z_ref[:, O0 + h * DH:O0 + (h + 1) * DH]) for h in range(N_HEADS)], axis=1)
    y_a = _mm(ha.astype(BF16), wmo_ref[...])

    uc_s[...] = z_ref[:, GV0:GV0 + D_MODEL] * _sigmoid(z_ref[:, GG0:GG0 + D_MODEL])
    full_s[...] = jnp.zeros_like(full_s)
    for j in range(n_seq):
        full_s[0:hist, :] = cst_ref[j]
        full_s[hist:hist + t_seq, :] = uc_s[j * t_seq:(j + 1) * t_seq, :]
        full = full_s[...]
        cnew_ref[j] = full_s[t_seq:t_seq + hist, :]
        for t in range(t_seq):
            uc_s[j * t_seq + t:j * t_seq + t + 1, :] = (
                jnp.sum(cw4_ref[t] * full, axis=0, keepdims=True) + cb_ref[...])

    ucn = _layernorm(uc_s[...], lng_ref[...], lnb_ref[...])
    y_b = _mm((ucn * _sigmoid(ucn)).astype(BF16), wco_ref[...])
    mix = _sigmoid(z_ref[:, GA0:GA0 + D_MODEL]) * y_a + _sigmoid(z_ref[:, GB0:GB0 + D_MODEL]) * y_b
    x1_ref[...] = x_ref[...] + _mm(mix.astype(BF16), wout_ref[...])


def _sample_mixer_tail(x, z, h, cst, hg, wmo, cw4, cb, lng, lnb, wco, wout, t_seq):
    R, D = x.shape
    rows = SAMPLE_ROWS
    sb = rows // t_seq
    hist = CONV_K - 1
    win = cw4.shape[1]
    kern = functools.partial(_sample_mixer_tail_kernel, t_seq=t_seq)
    return pl.pallas_call(
        kern,
        grid=(R // rows,),
        in_specs=[
            pl.BlockSpec((rows, D), lambda i: (i, 0)),
            pl.BlockSpec((rows, N_Z), lambda i: (i, 0)),
            pl.BlockSpec((rows, D), lambda i: (i, 0)),
            pl.BlockSpec((sb, hist, D), lambda i: (i, 0, 0)),
            _const_spec((1, D)), _const_spec((D, D)), _const_spec((t_seq, win, D)), _const_spec((1, D)),
            _const_spec((1, D)), _const_spec((1, D)), _const_spec((D, D)), _const_spec((D, D)),
        ],
        out_specs=[
            pl.BlockSpec((rows, D), lambda i: (i, 0)),
            pl.BlockSpec((sb, hist, D), lambda i: (i, 0, 0)),
        ],
        out_shape=[
            jax.ShapeDtypeStruct((R, D), F32),
            jax.ShapeDtypeStruct(cst.shape, F32),
        ],
        scratch_shapes=[pltpu.VMEM((win, D), F32), pltpu.VMEM((rows, D), F32)],
        compiler_params=pltpu.CompilerParams(
            dimension_semantics=("arbitrary",), vmem_limit_bytes=VMEM_LIMIT),
        name="sample_mixer_tail",
    )(x, z, h, cst, hg, wmo, cw4, cb, lng, lnb, wco, wout)


def _sample_ffn_kernel(x_ref, fst_ref, fg_ref, wup_ref, fw_ref, fb_ref, wdown_ref, fing_ref,
                       y_ref, fnew_ref, up_s, seq_s, *, t_seq):
    R = x_ref.shape[0]
    n_seq = R // t_seq
    hist = FFN_K - 1
    x = x_ref[...]
    hf = _rmsnorm(x, fg_ref[...]).astype(BF16)
    for blk in range(2 * D_FF // DH):
        cs = slice(blk * DH, (blk + 1) * DH)
        up_s[:, cs] = _mm(hf, wup_ref[:, cs])
    for j in range(n_seq):
        rows = slice(j * t_seq, (j + 1) * t_seq)
        seq_s[0:hist, :] = fst_ref[j]
        seq_s[hist:hist + t_seq, :] = up_s[rows, :]
        fnew_ref[j] = seq_s[t_seq:t_seq + hist, :]
        acc = fb_ref[...] + fw_ref[0:1, :] * seq_s[0:t_seq, :]
        for kk in range(1, FFN_K):
            acc = acc + fw_ref[kk:kk + 1, :] * seq_s[kk:kk + t_seq, :]
        up_s[rows, :] = acc
    y_ref[...] = _ffn_tail(up_s[:, 0:D_FF], up_s[:, D_FF:2 * D_FF], wdown_ref, x, fing_ref)


def _sample_ffn(x1, fst, fg, wup, fw, fb, wdown, fing, t_seq):
    R, D = x1.shape
    rows = SAMPLE_ROWS
    sb = rows // t_seq
    hist = FFN_K - 1
    kern = functools.partial(_sample_ffn_kernel, t_seq=t_seq)
    return pl.pallas_call(
        kern,
        grid=(R // rows,),
        in_specs=[
            pl.BlockSpec((rows, D), lambda i: (i, 0)),
            pl.BlockSpec((sb, hist, 2 * D_FF), lambda i: (i, 0, 0)),
            _const_spec((1, D)), _const_spec((D, 2 * D_FF)), _const_spec((FFN_K, 2 * D_FF)),
            _const_spec((1, 2 * D_FF)), _const_spec((D_FF, D)), _const_spec((1, D)),
        ],
        out_specs=[
            pl.BlockSpec((rows, D), lambda i: (i, 0)),
            pl.BlockSpec((sb, hist, 2 * D_FF), lambda i: (i, 0, 0)),
        ],
        out_shape=[
            jax.ShapeDtypeStruct((R, D), F32),
            jax.ShapeDtypeStruct(fst.shape, F32),
        ],
        scratch_shapes=[pltpu.VMEM((rows, 2 * D_FF), F32), pltpu.VMEM((8, 2 * D_FF), F32)],
        compiler_params=pltpu.CompilerParams(
            dimension_semantics=("arbitrary",), vmem_limit_bytes=VMEM_LIMIT),
        name="sample_ffn",
    )(x1, fst, fg, wup, fw, fb, wdown, fing)


def _pack_w_in(w_in):
    d = w_in.shape[0]
    split = 4 * D_MODEL
    gates = w_in[:, split:split + 2 * N_HEADS]
    pad = jnp.zeros((d, LANES - N_HEADS), w_in.dtype)
    return jnp.concatenate(
        [w_in[:, :split], w_in[:, split + 2 * N_HEADS:],
         gates[:, :N_HEADS], pad, gates[:, N_HEADS:], pad], axis=1).astype(BF16)


def _pad_gate_bias(b):
    return jnp.pad(b, (0, LANES - N_HEADS)).reshape(1, LANES)


def _shifted_conv_weights(cw, t_seq):
    k = cw.shape[0]
    win = -(-(k - 1 + t_seq) // 8) * 8
    return jnp.stack([jnp.pad(cw, ((t, win - k - t), (0, 0))) for t in range(t_seq)])


def kernel(x_prompt, x_sample, state_C, state_n, state_m, state_conv, state_ffn, mix_norm_g, w_in, b_if,
           head_norm_g, w_mlstm_out, conv_w, conv_b, conv_ln_g, conv_ln_b, w_conv_out, w_out, ffn_norm_g,
           w_up, ffn_conv_w, ffn_conv_b, w_down, final_norm_g):
    depth = w_in.shape[0]
    assert depth == 1, "single-layer trunk"
    l = 0
    row = lambda a: a.reshape(1, -1)
    w_all = _pack_w_in(w_in[l])
    bi = _pad_gate_bias(b_if[l, :N_HEADS])
    bf = _pad_gate_bias(b_if[l, N_HEADS:])
    mixg, hg = row(mix_norm_g[l]), row(head_norm_g[l])
    wmo, wco, wout = (w.astype(BF16) for w in (w_mlstm_out[l], w_conv_out[l], w_out[l]))
    cw, cb = conv_w[l], row(conv_b[l])
    lng, lnb = row(conv_ln_g[l]), row(conv_ln_b[l])
    fg, fing = row(ffn_norm_g[l]), row(final_norm_g)
    wup, wdown = w_up[l].astype(BF16), w_down[l].astype(BF16)
    fw, fb = ffn_conv_w[l], row(ffn_conv_b[l])

    x1p, c_p, n_p, m_p, conv_p = _prompt_mixer(x_prompt, mixg, w_all, bi, bf, hg, wmo, cw, cb, lng, lnb, wco, wout)
    y_p, ffn_p = _prompt_ffn(x1p, fg, wup, fw, fb, wdown, fing)

    nb, t_seq, d = x_sample.shape
    xs = x_sample.reshape(nb * t_seq, d)
    z = _sample_proj(xs, mixg, w_all)
    h_s, c_s, n_s, m_s = _sample_mlstm(z, bi, bf, state_C[l], state_n[l], state_m[l], t_seq)
    cw4 = _shifted_conv_weights(cw, t_seq)
    x1s, conv_s = _sample_mixer_tail(xs, z, h_s, state_conv[l], hg, wmo, cw4, cb, lng, lnb, wco, wout, t_seq)
    y_s, ffn_s = _sample_ffn(x1s, state_ffn[l], fg, wup, fw, fb, wdown, fing, t_seq)

    return (y_p, y_s.reshape(nb, t_seq, d),
            c_p[None], n_p[None], m_p.reshape(1, -1, N_HEADS), conv_p[None], ffn_p[None],
            c_s[None], n_s[None], m_s[None], conv_s[None], ffn_s[None])
```

```python
import functools

import jax
import jax.numpy as jnp
from jax import lax
from jax.experimental import pallas as pl
from jax.experimental.pallas import tpu as pltpu

F32 = jnp.float32
BF16 = jnp.bfloat16

D_MODEL = 1024
N_HEADS = 4
DH = D_MODEL // N_HEADS
CONV_K = 31
D_FF = 2816
FFN_K = 3
EPS = 1e-6

LANES = 128
Q0, K0, V0, O0, GV0, GG0, GA0, GB0 = (i * D_MODEL for i in range(8))
IG0 = 8 * D_MODEL
FG0 = IG0 + LANES
N_Z = FG0 + LANES

SUBLANES = 8
CONV_LEAD = 32
PROMPT_CHUNK = 256
SAMPLE_ROWS = 128
MLSTM_SEQS = 8
VMEM_LIMIT = 56 * 1024 * 1024


def _mm(a, b):
    return jnp.dot(a, b, preferred_element_type=F32)


def _mm_nt(a, b):
    return lax.dot_general(a, b, (((1,), (1,)), ((), ())), preferred_element_type=F32)


def _mm_tn(a, b):
    return lax.dot_general(a, b, (((0,), (0,)), ((), ())), preferred_element_type=F32)


def _rmsnorm(x, g):
    return x * lax.rsqrt(jnp.mean(x * x, axis=-1, keepdims=True) + EPS) * g


def _layernorm(x, g, b):
    mu = jnp.mean(x, axis=-1, keepdims=True)
    xc = x - mu
    return xc * lax.rsqrt(jnp.mean(xc * xc, axis=-1, keepdims=True) + EPS) * g + b


def _sigmoid(x):
    return 0.5 * jnp.tanh(0.5 * x) + 0.5


def _log_sigmoid(x):
    return jnp.minimum(x, 0.0) - jnp.log1p(jnp.exp(-jnp.abs(x)))


def _cumsum_rows(x, period=None):
    n = x.shape[0]
    row = lax.broadcasted_iota(jnp.int32, x.shape, 0)
    pos = row if period is None else row % period
    span = n if period is None else period
    s = 1
    while s < span:
        x = x + jnp.where(pos >= s, pltpu.roll(x, s, axis=0), 0.0)
        s *= 2
    return x


def _mlstm_head(q, k, v, b_col, a_col, a_row, m_prev, c_prev, n_prev, n_valid):
    L, LK = q.shape[0], k.shape[0]
    row = lax.broadcasted_iota(jnp.int32, (L, LK), 0)
    col = lax.broadcasted_iota(jnp.int32, (L, LK), 1)
    d = jnp.where(col <= row, b_col + a_row, -jnp.inf)
    inter = b_col + m_prev
    m_t = jnp.maximum(inter, jnp.max(d, axis=1, keepdims=True))
    w_inter = jnp.exp(inter - m_t)
    p = jnp.exp(d - m_t)
    qb, kb, vb = q.astype(BF16), k.astype(BF16), v.astype(BF16)
    s = _mm_nt(qb, kb) * p
    num = w_inter * _mm_nt(qb, c_prev.astype(BF16)) + _mm(s.astype(BF16), vb)
    den = w_inter * jnp.sum(q * n_prev, axis=1, keepdims=True) + jnp.sum(s, axis=1, keepdims=True)
    h = num * (1.0 / jnp.maximum(jnp.abs(den), jnp.exp(-m_t)))
    m_new = m_t[n_valid - 1:n_valid, :]
    b_last = b_col[n_valid - 1:n_valid, :]
    g_inter = jnp.exp(b_last + m_prev - m_new)
    g_col = jnp.exp(b_last + a_col - m_new)
    if n_valid < LK:
        g_col = jnp.where(lax.broadcasted_iota(jnp.int32, (LK, 1), 0) < n_valid, g_col, 0.0)
    c_new = g_inter * c_prev + _mm_tn((g_col * v).astype(BF16), kb)
    n_new = g_inter * n_prev + jnp.sum(g_col * k, axis=0, keepdims=True)
    return h, c_new, n_new, m_new


def _head_out(h, head_g, o):
    mu = jnp.mean(h, axis=-1, keepdims=True)
    hc = h - mu
    hn = hc * lax.rsqrt(jnp.mean(hc * hc, axis=-1, keepdims=True) + EPS)
    return hn * head_g * _sigmoid(o)


def _causal_conv_cols(full_s, cwb_s, cb_ref, out_s, cs, T):
    off = CONV_LEAD - (CONV_K - 1)
    n_a = (off + CONV_K - 1) // SUBLANES + 1
    width = cs.stop - cs.start
    sub = lax.broadcasted_iota(jnp.int32, (SUBLANES, width), 0)
    bias = cb_ref[:, cs]
    tiles = {}

    def tile(i):
        if i not in tiles:
            tiles[i] = full_s[SUBLANES * i:SUBLANES * (i + 1), cs]
        return tiles[i]

    prev = None
    for i in range(T // SUBLANES + 1):
        ys = []
        for r in range(SUBLANES):
            acc = None
            for a in range(n_a):
                j = SUBLANES * a + r - off
                if 0 <= j < CONV_K:
                    term = cwb_s[j, :, cs] * tile(i + a)
                    acc = term if acc is None else acc + term
            ys.append(acc)
        tiles.pop(i, None)
        cur = [ys[0]] + [pltpu.roll(ys[r], SUBLANES - r, axis=0) for r in range(1, SUBLANES)]
        if prev is not None:
            out = prev[0] + bias
            for r in range(1, SUBLANES):
                out = out + jnp.where(sub < SUBLANES - r, prev[r], cur[r])
            out_s[SUBLANES * (i - 1):SUBLANES * i, cs] = out
        prev = cur


def _prompt_mixer_kernel(x_ref, mixg_ref, w_ref, bi_ref, bf_ref, hg_ref, wmo_ref, cw_ref, cb_ref,
                         lng_ref, lnb_ref, wco_ref, wout_ref,
                         x1_ref, c_ref, n_ref, m_ref, conv_ref,
                         xn_s, ha_s, full_s, uc_s, m_s, cwb_s, *, n_chunks):
    T = x_ref.shape[1]
    c = pl.program_id(1)

    @pl.when(c == 0)
    def _():
        c_ref[...] = jnp.zeros_like(c_ref)
        n_ref[...] = jnp.zeros_like(n_ref)
        m_s[...] = jnp.zeros_like(m_s)
        full_s[0:CONV_LEAD, :] = jnp.zeros((CONV_LEAD, D_MODEL), F32)
        full_s[CONV_LEAD + T:CONV_LEAD + T + SUBLANES, :] = jnp.zeros((SUBLANES, D_MODEL), F32)
        for j in range(CONV_K):
            cwb_s[j] = jnp.broadcast_to(cw_ref[j:j + 1, :], (SUBLANES, D_MODEL))

    x = x_ref[0]
    xn_s[...] = _rmsnorm(x, mixg_ref[...]).astype(BF16)
    xn = xn_s[...]

    zi = _mm(xn, w_ref[:, IG0:IG0 + LANES]) + bi_ref[...]
    lf = _log_sigmoid(_mm(xn, w_ref[:, FG0:FG0 + LANES]) + bf_ref[...])
    b_all = _cumsum_rows(lf)
    a_all = zi - b_all
    a_t = a_all.T

    for h in range(N_HEADS):
        cs = slice(h * DH, (h + 1) * DH)
        gv = _mm(xn, w_ref[:, GV0 + h * DH:GV0 + (h + 1) * DH])
        gg = _mm(xn, w_ref[:, GG0 + h * DH:GG0 + (h + 1) * DH])
        full_s[CONV_LEAD:CONV_LEAD + T, cs] = gv * _sigmoid(gg)
        q = _mm(xn, w_ref[:, Q0 + h * DH:Q0 + (h + 1) * DH])
        k = _mm(xn, w_ref[:, K0 + h * DH:K0 + (h + 1) * DH]) * (DH ** -0.5)
        v = _mm(xn, w_ref[:, V0 + h * DH:V0 + (h + 1) * DH])
        _causal_conv_cols(full_s, cwb_s, cb_ref, uc_s, cs, T)
        hh, c_new, n_new, m_new = _mlstm_head(
            q, k, v, b_all[:, h:h + 1], a_all[:, h:h + 1], a_t[h:h + 1, :],
            m_s[h:h + 1, 0:1], c_ref[0, h], n_ref[0, h:h + 1, :], T)
        c_ref[0, h] = c_new
        n_ref[0, h:h + 1, :] = n_new
        m_s[h:h + 1, :] = jnp.broadcast_to(m_new, (1, LANES))
        o = _mm(xn, w_ref[:, O0 + h * DH:O0 + (h + 1) * DH])
        ha_s[:, cs] = _head_out(hh, hg_ref[:, cs], o).astype(BF16)

    y_a = _mm(ha_s[...], wmo_ref[...])
    tail = full_s[T + CONV_LEAD - (CONV_K - 1):T + CONV_LEAD, :]
    conv_ref[0] = tail
    full_s[CONV_LEAD - (CONV_K - 1):CONV_LEAD, :] = tail

    ucn = _layernorm(uc_s[...], lng_ref[...], lnb_ref[...])
    y_b = _mm((ucn * _sigmoid(ucn)).astype(BF16), wco_ref[...])

    ga = _mm(xn, w_ref[:, GA0:GA0 + D_MODEL])
    gb = _mm(xn, w_ref[:, GB0:GB0 + D_MODEL])
    mix = _sigmoid(ga) * y_a + _sigmoid(gb) * y_b
    x1_ref[0] = x + _mm(mix.astype(BF16), wout_ref[...])

    @pl.when(c == n_chunks - 1)
    def _():
        for h in range(N_HEADS):
            m_ref[0, :, h:h + 1] = m_s[h:h + 1, 0:1]


def _const_spec(shape):
    nd = len(shape)
    return pl.BlockSpec(shape, lambda *_: (0,) * nd, pipeline_mode=pl.Buffered(1))


def _prompt_mixer(x, mixg, w_all, bi, bf, hg, wmo, cw, cb, lng, lnb, wco, wout):
    B, S, D = x.shape
    T = PROMPT_CHUNK
    nc = S // T
    kern = functools.partial(_prompt_mixer_kernel, n_chunks=nc)
    return pl.pallas_call(
        kern,
        grid=(B, nc),
        in_specs=[
            pl.BlockSpec((1, T, D), lambda b, c: (b, c, 0)),
            _const_spec((1, D)), _const_spec((D, N_Z)), _const_spec((1, LANES)), _const_spec((1, LANES)),
            _const_spec((1, D)), _const_spec((D, D)), _const_spec((CONV_K, D)), _const_spec((1, D)),
            _const_spec((1, D)), _const_spec((1, D)), _const_spec((D, D)), _const_spec((D, D)),
        ],
        out_specs=[
            pl.BlockSpec((1, T, D), lambda b, c: (b, c, 0)),
            pl.BlockSpec((1, N_HEADS, DH, DH), lambda b, c: (b, 0, 0, 0)),
            pl.BlockSpec((1, N_HEADS, DH), lambda b, c: (b, 0, 0)),
            pl.BlockSpec((1, 1, N_HEADS), lambda b, c: (b, 0, 0)),
            pl.BlockSpec((1, CONV_K - 1, D), lambda b, c: (b, 0, 0)),
        ],
        out_shape=[
            jax.ShapeDtypeStruct((B, S, D), F32),
            jax.ShapeDtypeStruct((B, N_HEADS, DH, DH), F32),
            jax.ShapeDtypeStruct((B, N_HEADS, DH), F32),
            jax.ShapeDtypeStruct((B, 1, N_HEADS), F32),
            jax.ShapeDtypeStruct((B, CONV_K - 1, D), F32),
        ],
        scratch_shapes=[
            pltpu.VMEM((T, D), BF16), pltpu.VMEM((T, D), BF16),
            pltpu.VMEM((T + CONV_LEAD + SUBLANES, D), F32), pltpu.VMEM((T, D), F32),
            pltpu.VMEM((SUBLANES, LANES), F32), pltpu.VMEM((CONV_K, SUBLANES, D), F32),
        ],
        compiler_params=pltpu.CompilerParams(
            dimension_semantics=("arbitrary", "arbitrary"), vmem_limit_bytes=VMEM_LIMIT),
        name="prompt_mixer",
    )(x, mixg, w_all, bi, bf, hg, wmo, cw, cb, lng, lnb, wco, wout)


def _ffn_tail(upc_a, upc_g, wdown_ref, x, fing_ref):
    act = (upc_a * _sigmoid(upc_a) * upc_g).astype(BF16)
    x2 = x + _mm(act, wdown_ref[...])
    return _rmsnorm(x2, fing_ref[...])


def _prompt_ffn_kernel(x_ref, fg_ref, wup_ref, fw_ref, fb_ref, wdown_ref, fing_ref,
                       y_ref, ffn_ref, up_s, act_s):
    T = x_ref.shape[1]
    c = pl.program_id(1)

    @pl.when(c == 0)
    def _():
        up_s[0:8, :] = jnp.zeros((8, 2 * D_FF), F32)

    x = x_ref[0]
    hf = _rmsnorm(x, fg_ref[...]).astype(BF16)
    nb = 2 * D_FF // DH
    for blk in range(nb):
        cs = slice(blk * DH, (blk + 1) * DH)
        up_s[8:8 + T, cs] = _mm(hf, wup_ref[:, cs])
    for blk in range(D_FF // DH):
        ca = slice(blk * DH, (blk + 1) * DH)
        cg = slice(D_FF + blk * DH, D_FF + (blk + 1) * DH)
        a = fb_ref[:, ca] + fw_ref[0:1, ca] * up_s[6:6 + T, ca] + fw_ref[1:2, ca] * up_s[7:7 + T, ca] \
            + fw_ref[2:3, ca] * up_s[8:8 + T, ca]
        g = fb_ref[:, cg] + fw_ref[0:1, cg] * up_s[6:6 + T, cg] + fw_ref[1:2, cg] * up_s[7:7 + T, cg] \
            + fw_ref[2:3, cg] * up_s[8:8 + T, cg]
        act_s[:, ca] = (a * _sigmoid(a) * g).astype(BF16)
    tail = up_s[T + 6:T + 8, :]
    ffn_ref[0] = tail
    up_s[6:8, :] = tail
    x2 = x + _mm(act_s[...], wdown_ref[...])
    y_ref[0] = _rmsnorm(x2, fing_ref[...])


def _prompt_ffn(x1, fg, wup, fw, fb, wdown, fing):
    B, S, D = x1.shape
    T = PROMPT_CHUNK
    nc = S // T
    return pl.pallas_call(
        _prompt_ffn_kernel,
        grid=(B, nc),
        in_specs=[
            pl.BlockSpec((1, T, D), lambda b, c: (b, c, 0)),
            _const_spec((1, D)), _const_spec((D, 2 * D_FF)), _const_spec((FFN_K, 2 * D_FF)),
            _const_spec((1, 2 * D_FF)), _const_spec((D_FF, D)), _const_spec((1, D)),
        ],
        out_specs=[
            pl.BlockSpec((1, T, D), lambda b, c: (b, c, 0)),
            pl.BlockSpec((1, FFN_K - 1, 2 * D_FF), lambda b, c: (b, 0, 0)),
        ],
        out_shape=[
            jax.ShapeDtypeStruct((B, S, D), F32),
            jax.ShapeDtypeStruct((B, FFN_K - 1, 2 * D_FF), F32),
        ],
        scratch_shapes=[pltpu.VMEM((T + 8, 2 * D_FF), F32), pltpu.VMEM((T, D_FF), BF16)],
        compiler_params=pltpu.CompilerParams(
            dimension_semantics=("arbitrary", "arbitrary"), vmem_limit_bytes=VMEM_LIMIT),
        name="prompt_ffn",
    )(x1, fg, wup, fw, fb, wdown, fing)


def _sample_proj_kernel(x_ref, mixg_ref, w_ref, z_ref):
    xn = _rmsnorm(x_ref[...], mixg_ref[...]).astype(BF16)
    for blk in range(N_Z // DH):
        cs = slice(blk * DH, (blk + 1) * DH)
        z_ref[:, cs] = _mm(xn, w_ref[:, cs])


def _sample_proj(x, mixg, w_all):
    R, D = x.shape
    return pl.pallas_call(
        _sample_proj_kernel,
        grid=(R // SAMPLE_ROWS,),
        in_specs=[pl.BlockSpec((SAMPLE_ROWS, D), lambda i: (i, 0)),
                  _const_spec((1, D)), _const_spec((D, N_Z))],
        out_specs=pl.BlockSpec((SAMPLE_ROWS, N_Z), lambda i: (i, 0)),
        out_shape=jax.ShapeDtypeStruct((R, N_Z), F32),
        compiler_params=pltpu.CompilerParams(
            dimension_semantics=("arbitrary",), vmem_limit_bytes=VMEM_LIMIT),
        name="sample_proj",
    )(x, mixg, w_all)


def _sample_mlstm_kernel(zq_ref, zg_ref, bi_ref, bf_ref, c_ref, n_ref, m_ref,
                         h_ref, co_ref, no_ref, mo_ref, *, t_seq):
    per_win = 8 // t_seq
    pad = LANES - 8
    for win in range(MLSTM_SEQS // per_win):
        zq_w = zq_ref[8 * win:8 * win + 8, :]
        zg_w = zg_ref[8 * win:8 * win + 8, :]
        for sub in range(per_win):
            j = win * per_win + sub
            if sub:
                zq = pltpu.roll(zq_w, 8 - sub * t_seq, axis=0)
                zg = pltpu.roll(zg_w, 8 - sub * t_seq, axis=0)
            else:
                zq, zg = zq_w, zg_w
            zi = zg[:, 0:LANES] + bi_ref[...]
            lf = _log_sigmoid(zg[:, LANES:2 * LANES] + bf_ref[...])
            b_all = _cumsum_rows(lf)
            a_all = jnp.concatenate([zi - b_all, jnp.zeros((pad, LANES), F32)], axis=0)
            a_t = a_all.T
            for h in range(N_HEADS):
                q = zq[:, Q0 + h * DH:Q0 + (h + 1) * DH]
                k = zq[:, K0 + h * DH:K0 + (h + 1) * DH] * (DH ** -0.5)
                v = zq[:, V0 + h * DH:V0 + (h + 1) * DH]
                zeros = jnp.zeros((pad, DH), F32)
                k = jnp.concatenate([k, zeros], axis=0)
                v = jnp.concatenate([v, zeros], axis=0)
                hh, c_new, n_new, m_new = _mlstm_head(
                    q, k, v, b_all[:, h:h + 1], a_all[:, h:h + 1], a_t[h:h + 1, :],
                    m_ref[j:j + 1, h:h + 1], c_ref[j, h], n_ref[j, h:h + 1, :], t_seq)
                co_ref[j, h] = c_new
                no_ref[j, h:h + 1, :] = n_new
                mo_ref[j:j + 1, h:h + 1] = m_new
                h_ref[j * t_seq:(j + 1) * t_seq, h * DH:(h + 1) * DH] = hh[0:t_seq, :]


def _sample_mlstm(z, bi, bf, c0, n0, m0, t_seq):
    nb = c0.shape[0]
    sb = MLSTM_SEQS
    rows = sb * t_seq
    kern = functools.partial(_sample_mlstm_kernel, t_seq=t_seq)
    return pl.pallas_call(
        kern,
        grid=(nb // sb,),
        in_specs=[
            pl.BlockSpec((rows, 3 * D_MODEL), lambda i: (i, 0)),
            pl.BlockSpec((rows, 2 * LANES), lambda i: (i, IG0 // (2 * LANES))),
            _const_spec((1, LANES)), _const_spec((1, LANES)),
            pl.BlockSpec((sb, N_HEADS, DH, DH), lambda i: (i, 0, 0, 0)),
            pl.BlockSpec((sb, N_HEADS, DH), lambda i: (i, 0, 0)),
            pl.BlockSpec((sb, N_HEADS), lambda i: (i, 0)),
        ],
        out_specs=[
            pl.BlockSpec((rows, D_MODEL), lambda i: (i, 0)),
            pl.BlockSpec((sb, N_HEADS, DH, DH), lambda i: (i, 0, 0, 0)),
            pl.BlockSpec((sb, N_HEADS, DH), lambda i: (i, 0, 0)),
            pl.BlockSpec((sb, N_HEADS), lambda i: (i, 0)),
        ],
        out_shape=[
            jax.ShapeDtypeStruct((nb * t_seq, D_MODEL), F32),
            jax.ShapeDtypeStruct(c0.shape, F32),
            jax.ShapeDtypeStruct(n0.shape, F32),
            jax.ShapeDtypeStruct(m0.shape, F32),
        ],
        compiler_params=pltpu.CompilerParams(
            dimension_semantics=("arbitrary",), vmem_limit_bytes=VMEM_LIMIT),
        name="sample_mlstm",
    )(z, z, bi, bf, c0, n0, m0)


def _sample_mixer_tail_kernel(x_ref, z_ref, h_ref, cst_ref, hg_ref, wmo_ref, cw4_ref, cb_ref,
                              lng_ref, lnb_ref, wco_ref, wout_ref,
                              x1_ref, cnew_ref, full_s, uc_s, *, t_seq):
    R = x_ref.shape[0]
    n_seq = R // t_seq
    hist = CONV_K - 1
    ha = jnp.concatenate(
        [_head_out(h_ref[:, h * DH:(h + 1) * DH], hg_ref[:, h * DH:(h + 1) * DH],
                   z_ref[:, O0 + h * DH:O0 + (h + 1) * DH]) for h in range(N_HEADS)], axis=1)
    y_a = _mm(ha.astype(BF16), wmo_ref[...])

    uc_s[...] = z_ref[:, GV0:GV0 + D_MODEL] * _sigmoid(z_ref[:, GG0:GG0 + D_MODEL])
    full_s[...] = jnp.zeros_like(full_s)
    for j in range(n_seq):
        full_s[0:hist, :] = cst_ref[j]
        full_s[hist:hist + t_seq, :] = uc_s[j * t_seq:(j + 1) * t_seq, :]
        full = full_s[...]
        cnew_ref[j] = full_s[t_seq:t_seq + hist, :]
        for t in range(t_seq):
            uc_s[j * t_seq + t:j * t_seq + t + 1, :] = (
                jnp.sum(cw4_ref[t] * full, axis=0, keepdims=True) + cb_ref[...])

    ucn = _layernorm(uc_s[...], lng_ref[...], lnb_ref[...])
    y_b = _mm((ucn * _sigmoid(ucn)).astype(BF16), wco_ref[...])
    mix = _sigmoid(z_ref[:, GA0:GA0 + D_MODEL]) * y_a + _sigmoid(z_ref[:, GB0:GB0 + D_MODEL]) * y_b
    x1_ref[...] = x_ref[...] + _mm(mix.astype(BF16), wout_ref[...])


def _sample_mixer_tail(x, z, h, cst, hg, wmo, cw4, cb, lng, lnb, wco, wout, t_seq):
    R, D = x.shape
    rows = SAMPLE_ROWS
    sb = rows // t_seq
    hist = CONV_K - 1
    win = cw4.shape[1]
    kern = functools.partial(_sample_mixer_tail_kernel, t_seq=t_seq)
    return pl.pallas_call(
        kern,
        grid=(R // rows,),
        in_specs=[
            pl.BlockSpec((rows, D), lambda i: (i, 0)),
            pl.BlockSpec((rows, N_Z), lambda i: (i, 0)),
            pl.BlockSpec((rows, D), lambda i: (i, 0)),
            pl.BlockSpec((sb, hist, D), lambda i: (i, 0, 0)),
            _const_spec((1, D)), _const_spec((D, D)), _const_spec((t_seq, win, D)), _const_spec((1, D)),
            _const_spec((1, D)), _const_spec((1, D)), _const_spec((D, D)), _const_spec((D, D)),
        ],
        out_specs=[
            pl.BlockSpec((rows, D), lambda i: (i, 0)),
            pl.BlockSpec((sb, hist, D), lambda i: (i, 0, 0)),
        ],
        out_shape=[
            jax.ShapeDtypeStruct((R, D), F32),
            jax.ShapeDtypeStruct(cst.shape, F32),
        ],
        scratch_shapes=[pltpu.VMEM((win, D), F32), pltpu.VMEM((rows, D), F32)],
        compiler_params=pltpu.CompilerParams(
            dimension_semantics=("arbitrary",), vmem_limit_bytes=VMEM_LIMIT),
        name="sample_mixer_tail",
    )(x, z, h, cst, hg, wmo, cw4, cb, lng, lnb, wco, wout)


def _sample_ffn_kernel(x_ref, fst_ref, fg_ref, wup_ref, fw_ref, fb_ref, wdown_ref, fing_ref,
                       y_ref, fnew_ref, up_s, seq_s, *, t_seq):
    R = x_ref.shape[0]
    n_seq = R // t_seq
    hist = FFN_K - 1
    x = x_ref[...]
    hf = _rmsnorm(x, fg_ref[...]).astype(BF16)
    for blk in range(2 * D_FF // DH):
        cs = slice(blk * DH, (blk + 1) * DH)
        up_s[:, cs] = _mm(hf, wup_ref[:, cs])
    for j in range(n_seq):
        rows = slice(j * t_seq, (j + 1) * t_seq)
        seq_s[0:hist, :] = fst_ref[j]
        seq_s[hist:hist + t_seq, :] = up_s[rows, :]
        fnew_ref[j] = seq_s[t_seq:t_seq + hist, :]
        acc = fb_ref[...] + fw_ref[0:1, :] * seq_s[0:t_seq, :]
        for kk in range(1, FFN_K):
            acc = acc + fw_ref[kk:kk + 1, :] * seq_s[kk:kk + t_seq, :]
        up_s[rows, :] = acc
    y_ref[...] = _ffn_tail(up_s[:, 0:D_FF], up_s[:, D_FF:2 * D_FF], wdown_ref, x, fing_ref)


def _sample_ffn(x1, fst, fg, wup, fw, fb, wdown, fing, t_seq):
    R, D = x1.shape
    rows = SAMPLE_ROWS
    sb = rows // t_seq
    hist = FFN_K - 1
    kern = functools.partial(_sample_ffn_kernel, t_seq=t_seq)
    return pl.pallas_call(
        kern,
        grid=(R // rows,),
        in_specs=[
            pl.BlockSpec((rows, D), lambda i: (i, 0)),
            pl.BlockSpec((sb, hist, 2 * D_FF), lambda i: (i, 0, 0)),
            _const_spec((1, D)), _const_spec((D, 2 * D_FF)), _const_spec((FFN_K, 2 * D_FF)),
            _const_spec((1, 2 * D_FF)), _const_spec((D_FF, D)), _const_spec((1, D)),
        ],
        out_specs=[
            pl.BlockSpec((rows, D), lambda i: (i, 0)),
            pl.BlockSpec((sb, hist, 2 * D_FF), lambda i: (i, 0, 0)),
        ],
        out_shape=[
            jax.ShapeDtypeStruct((R, D), F32),
            jax.ShapeDtypeStruct(fst.shape, F32),
        ],
        scratch_shapes=[pltpu.VMEM((rows, 2 * D_FF), F32), pltpu.VMEM((8, 2 * D_FF), F32)],
        compiler_params=pltpu.CompilerParams(
            dimension_semantics=("arbitrary",), vmem_limit_bytes=VMEM_LIMIT),
        name="sample_ffn",
    )(x1, fst, fg, wup, fw, fb, wdown, fing)


def _pack_w_in(w_in):
    d = w_in.shape[0]
    split = 4 * D_MODEL
    gates = w_in[:, split:split + 2 * N_HEADS]
    pad = jnp.zeros((d, LANES - N_HEADS), w_in.dtype)
    return jnp.concatenate(
        [w_in[:, :split], w_in[:, split + 2 * N_HEADS:],
         gates[:, :N_HEADS], pad, gates[:, N_HEADS:], pad], axis=1).astype(BF16)


def _pad_gate_bias(b):
    return jnp.pad(b, (0, LANES - N_HEADS)).reshape(1, LANES)


def _shifted_conv_weights(cw, t_seq):
    k = cw.shape[0]
    win = -(-(k - 1 + t_seq) // 8) * 8
    return jnp.stack([jnp.pad(cw, ((t, win - k - t), (0, 0))) for t in range(t_seq)])


def kernel(x_prompt, x_sample, state_C, state_n, state_m, state_conv, state_ffn, mix_norm_g, w_in, b_if,
           head_norm_g, w_mlstm_out, conv_w, conv_b, conv_ln_g, conv_ln_b, w_conv_out, w_out, ffn_norm_g,
           w_up, ffn_conv_w, ffn_conv_b, w_down, final_norm_g):
    depth = w_in.shape[0]
    assert depth == 1, "single-layer trunk"
    l = 0
    row = lambda a: a.reshape(1, -1)
    w_all = _pack_w_in(w_in[l])
    bi = _pad_gate_bias(b_if[l, :N_HEADS])
    bf = _pad_gate_bias(b_if[l, N_HEADS:])
    mixg, hg = row(mix_norm_g[l]), row(head_norm_g[l])
    wmo, wco, wout = (w.astype(BF16) for w in (w_mlstm_out[l], w_conv_out[l], w_out[l]))
    cw, cb = conv_w[l], row(conv_b[l])
    lng, lnb = row(conv_ln_g[l]), row(conv_ln_b[l])
    fg, fing = row(ffn_norm_g[l]), row(final_norm_g)
    wup, wdown = w_up[l].astype(BF16), w_down[l].astype(BF16)
    fw, fb = ffn_conv_w[l], row(ffn_conv_b[l])

    x1p, c_p, n_p, m_p, conv_p = _prompt_mixer(x_prompt, mixg, w_all, bi, bf, hg, wmo, cw, cb, lng, lnb, wco, wout)
    y_p, ffn_p = _prompt_ffn(x1p, fg, wup, fw, fb, wdown, fing)

    nb, t_seq, d = x_sample.shape
    xs = x_sample.reshape(nb * t_seq, d)
    z = _sample_proj(xs, mixg, w_all)
    h_s, c_s, n_s, m_s = _sample_mlstm(z, bi, bf, state_C[l], state_n[l], state_m[l], t_seq)
    cw4 = _shifted_conv_weights(cw, t_seq)
    x1s, conv_s = _sample_mixer_tail(xs, z, h_s, state_conv[l], hg, wmo, cw4, cb, lng, lnb, wco, wout, t_seq)
    y_s, ffn_s = _sample_ffn(x1s, state_ffn[l], fg, wup, fw, fb, wdown, fing, t_seq)

    return (y_p, y_s.reshape(nb, t_seq, d),
            c_p[None], n_p[None], m_p.reshape(1, -1, N_HEADS), conv_p[None], ffn_p[None],
            c_s[None], n_s[None], m_s[None], conv_s[None], ffn_s[None])
```

```python
import functools

import jax
import jax.numpy as jnp
from jax import lax
from jax.experimental import pallas as pl
from jax.experimental.pallas import tpu as pltpu

F32 = jnp.float32
BF16 = jnp.bfloat16

D_MODEL = 1024
N_HEADS = 4
DH = D_MODEL // N_HEADS
CONV_K = 31
D_FF = 2816
FFN_K = 3
EPS = 1e-6

LANES = 128
Q0, K0, V0, O0, GV0, GG0, GA0, GB0 = (i * D_MODEL for i in range(8))
IG0 = 8 * D_MODEL
FG0 = IG0 + LANES
N_Z = FG0 + LANES

SUBLANES = 8
CONV_LEAD = 32
PROMPT_CHUNK = 256
SAMPLE_ROWS = 128
MLSTM_SEQS = 8
PACK_ROWS = 128
VMEM_LIMIT = 58 * 1024 * 1024


def _mm(a, b):
    return jnp.dot(a, b, preferred_element_type=F32)


def _mm_nt(a, b):
    return lax.dot_general(a, b, (((1,), (1,)), ((), ())), preferred_element_type=F32)


def _mm_tn(a, b):
    return lax.dot_general(a, b, (((0,), (0,)), ((), ())), preferred_element_type=F32)


def _rmsnorm(x, g):
    return x * lax.rsqrt(jnp.mean(x * x, axis=-1, keepdims=True) + EPS) * g


def _layernorm(x, g, b):
    mu = jnp.mean(x, axis=-1, keepdims=True)
    xc = x - mu
    return xc * lax.rsqrt(jnp.mean(xc * xc, axis=-1, keepdims=True) + EPS) * g + b


def _sigmoid(x):
    return 0.5 * jnp.tanh(0.5 * x) + 0.5


def _log_sigmoid(x):
    return jnp.minimum(x, 0.0) - jnp.log1p(jnp.exp(-jnp.abs(x)))


def _cumsum_rows(x):
    n = x.shape[0]
    row = lax.broadcasted_iota(jnp.int32, x.shape, 0)
    s = 1
    while s < n:
        x = x + jnp.where(row >= s, pltpu.roll(x, s, axis=0), 0.0)
        s *= 2
    return x


def _mlstm_head(q, k, v, b_col, a_col, a_row, m_prev, c_prev, n_prev, n_valid):
    L, LK = q.shape[0], k.shape[0]
    row = lax.broadcasted_iota(jnp.int32, (L, LK), 0)
    col = lax.broadcasted_iota(jnp.int32, (L, LK), 1)
    d = jnp.where(col <= row, b_col + a_row, -jnp.inf)
    inter = b_col + m_prev
    m_t = jnp.maximum(inter, jnp.max(d, axis=1, keepdims=True))
    w_inter = jnp.exp(inter - m_t)
    p = jnp.exp(d - m_t)
    qb, kb, vb = q.astype(BF16), k.astype(BF16), v.astype(BF16)
    s = _mm_nt(qb, kb) * p
    num = w_inter * _mm_nt(qb, c_prev.astype(BF16)) + _mm(s.astype(BF16), vb)
    den = w_inter * jnp.sum(q * n_prev, axis=1, keepdims=True) + jnp.sum(s, axis=1, keepdims=True)
    h = num * (1.0 / jnp.maximum(jnp.abs(den), jnp.exp(-m_t)))
    m_new = m_t[n_valid - 1:n_valid, :]
    b_last = b_col[n_valid - 1:n_valid, :]
    g_inter = jnp.exp(b_last + m_prev - m_new)
    g_col = jnp.exp(b_last + a_col - m_new)
    if n_valid < LK:
        g_col = jnp.where(lax.broadcasted_iota(jnp.int32, (LK, 1), 0) < n_valid, g_col, 0.0)
    c_new = g_inter * c_prev + _mm_tn((g_col * v).astype(BF16), kb)
    n_new = g_inter * n_prev + jnp.sum(g_col * k, axis=0, keepdims=True)
    return h, c_new, n_new, m_new


def _head_out(h, head_g, o):
    mu = jnp.mean(h, axis=-1, keepdims=True)
    hc = h - mu
    hn = hc * lax.rsqrt(jnp.mean(hc * hc, axis=-1, keepdims=True) + EPS)
    return hn * head_g * _sigmoid(o)


def _causal_conv_cols(full_v, cwb_s, cb_ref, out_v, cs, T):
    off = CONV_LEAD - (CONV_K - 1)
    n_a = (off + CONV_K - 1) // SUBLANES + 1
    width = cs.stop - cs.start
    sub = lax.broadcasted_iota(jnp.int32, (SUBLANES, width), 0)
    bias = cb_ref[:, cs]
    tiles = {}

    def tile(i):
        if i not in tiles:
            tiles[i] = full_v[SUBLANES * i:SUBLANES * (i + 1), :]
        return tiles[i]

    prev = None
    for i in range(T // SUBLANES + 1):
        ys = []
        for r in range(SUBLANES):
            acc = None
            for a in range(n_a):
                j = SUBLANES * a + r - off
                if 0 <= j < CONV_K:
                    term = cwb_s[j, :, cs] * tile(i + a)
                    acc = term if acc is None else acc + term
            ys.append(acc)
        tiles.pop(i, None)
        cur = [ys[0]] + [pltpu.roll(ys[r], SUBLANES - r, axis=0) for r in range(1, SUBLANES)]
        if prev is not None:
            out = prev[0] + bias
            for r in range(1, SUBLANES):
                out = out + jnp.where(sub < SUBLANES - r, prev[r], cur[r])
            out_v[SUBLANES * (i - 1):SUBLANES * i, :] = out
        prev = cur


def _const_spec(shape):
    nd = len(shape)
    return pl.BlockSpec(shape, lambda *_: (0,) * nd, pipeline_mode=pl.Buffered(1))


def _pack_kernel(w_ref, o_ref):
    split = 4 * D_MODEL
    n_gate = 2 * N_HEADS
    o_ref[:, 0:split] = w_ref[:, 0:split].astype(BF16)
    o_ref[:, split:2 * split] = w_ref[:, split + n_gate:2 * split + n_gate].astype(BF16)
    g = w_ref[:, split:split + LANES]
    lane = lax.broadcasted_iota(jnp.int32, g.shape, 1)
    o_ref[:, IG0:IG0 + LANES] = jnp.where(lane < N_HEADS, g, 0.0).astype(BF16)
    o_ref[:, FG0:FG0 + LANES] = jnp.where(
        lane < N_HEADS, pltpu.roll(g, LANES - N_HEADS, axis=1), 0.0).astype(BF16)


def _pack_w_in(w_in):
    _, d, n_in = w_in.shape
    return pl.pallas_call(
        _pack_kernel,
        grid=(d // PACK_ROWS,),
        in_specs=[pl.BlockSpec((None, PACK_ROWS, n_in), lambda i: (0, i, 0))],
        out_specs=pl.BlockSpec((PACK_ROWS, N_Z), lambda i: (i, 0)),
        out_shape=jax.ShapeDtypeStruct((d, N_Z), BF16),
        compiler_params=pltpu.CompilerParams(
            dimension_semantics=("arbitrary",), vmem_limit_bytes=VMEM_LIMIT),
        name="pack_w_in",
    )(w_in)


def _prompt_mixer_kernel(x_ref, mixg_ref, w_ref, bi_ref, bf_ref, hg_ref, wmo_ref, cw_ref, cb_ref,
                         lng_ref, lnb_ref, wco_ref, wout_ref,
                         x1_ref, c_ref, n_ref, m_ref, conv_ref,
                         xn_s, ha_s, full_s, uc_s, m_s, cwb_s, *, n_chunks):
    T = x_ref.shape[1]
    c = pl.program_id(1)

    @pl.when(c == 0)
    def _():
        c_ref[...] = jnp.zeros_like(c_ref)
        n_ref[...] = jnp.zeros_like(n_ref)
        m_s[...] = jnp.zeros_like(m_s)
        full_s[0:CONV_LEAD, :] = jnp.zeros((CONV_LEAD, D_MODEL), F32)
        full_s[CONV_LEAD + T:CONV_LEAD + T + SUBLANES, :] = jnp.zeros((SUBLANES, D_MODEL), F32)
        for j in range(CONV_K):
            cwb_s[j] = jnp.broadcast_to(cw_ref[j:j + 1, :], (SUBLANES, D_MODEL))

    x = x_ref[0]
    xn_s[...] = _rmsnorm(x, mixg_ref[...]).astype(BF16)
    xn = xn_s[...]

    zi = _mm(xn, w_ref[:, IG0:IG0 + LANES]) + bi_ref[...]
    lf = _log_sigmoid(_mm(xn, w_ref[:, FG0:FG0 + LANES]) + bf_ref[...])
    b_all = _cumsum_rows(lf)
    a_all = zi - b_all
    a_t = a_all.T

    for h in range(N_HEADS):
        cs = slice(h * DH, (h + 1) * DH)
        gv = _mm(xn, w_ref[:, GV0 + h * DH:GV0 + (h + 1) * DH])
        gg = _mm(xn, w_ref[:, GG0 + h * DH:GG0 + (h + 1) * DH])
        full_s[CONV_LEAD:CONV_LEAD + T, cs] = gv * _sigmoid(gg)
        q = _mm(xn, w_ref[:, Q0 + h * DH:Q0 + (h + 1) * DH])
        k = _mm(xn, w_ref[:, K0 + h * DH:K0 + (h + 1) * DH]) * (DH ** -0.5)
        v = _mm(xn, w_ref[:, V0 + h * DH:V0 + (h + 1) * DH])
        _causal_conv_cols(full_s.at[:, cs], cwb_s, cb_ref, uc_s.at[:, cs], cs, T)
        hh, c_new, n_new, m_new = _mlstm_head(
            q, k, v, b_all[:, h:h + 1], a_all[:, h:h + 1], a_t[h:h + 1, :],
            m_s[h:h + 1, 0:1], c_ref[0, h], n_ref[0, h:h + 1, :], T)
        c_ref[0, h] = c_new
        n_ref[0, h:h + 1, :] = n_new
        m_s[h:h + 1, :] = jnp.broadcast_to(m_new, (1, LANES))
        o = _mm(xn, w_ref[:, O0 + h * DH:O0 + (h + 1) * DH])
        ha_s[:, cs] = _head_out(hh, hg_ref[:, cs], o).astype(BF16)

    y_a = _mm(ha_s[...], wmo_ref[...])
    tail = full_s[T + CONV_LEAD - (CONV_K - 1):T + CONV_LEAD, :]
    conv_ref[0] = tail
    full_s[CONV_LEAD - (CONV_K - 1):CONV_LEAD, :] = tail

    ucn = _layernorm(uc_s[...], lng_ref[...], lnb_ref[...])
    y_b = _mm((ucn * _sigmoid(ucn)).astype(BF16), wco_ref[...])

    ga = _mm(xn, w_ref[:, GA0:GA0 + D_MODEL])
    gb = _mm(xn, w_ref[:, GB0:GB0 + D_MODEL])
    mix = _sigmoid(ga) * y_a + _sigmoid(gb) * y_b
    x1_ref[0] = x + _mm(mix.astype(BF16), wout_ref[...])

    @pl.when(c == n_chunks - 1)
    def _():
        for h in range(N_HEADS):
            m_ref[0, :, h:h + 1] = m_s[h:h + 1, 0:1]


def _prompt_mixer(x, mixg, w_all, bi, bf, hg, wmo, cw, cb, lng, lnb, wco, wout):
    B, S, D = x.shape
    T = PROMPT_CHUNK
    nc = S // T
    kern = functools.partial(_prompt_mixer_kernel, n_chunks=nc)
    return pl.pallas_call(
        kern,
        grid=(B, nc),
        in_specs=[
            pl.BlockSpec((1, T, D), lambda b, c: (b, c, 0)),
            _const_spec((1, D)), _const_spec((D, N_Z)), _const_spec((1, LANES)), _const_spec((1, LANES)),
            _const_spec((1, D)), _const_spec((D, D)), _const_spec((CONV_K, D)), _const_spec((1, D)),
            _const_spec((1, D)), _const_spec((1, D)), _const_spec((D, D)), _const_spec((D, D)),
        ],
        out_specs=[
            pl.BlockSpec((1, T, D), lambda b, c: (b, c, 0)),
            pl.BlockSpec((None, 1, N_HEADS, DH, DH), lambda b, c: (0, b, 0, 0, 0)),
            pl.BlockSpec((None, 1, N_HEADS, DH), lambda b, c: (0, b, 0, 0)),
            pl.BlockSpec((1, 1, N_HEADS), lambda b, c: (b, 0, 0)),
            pl.BlockSpec((None, 1, CONV_K - 1, D), lambda b, c: (0, b, 0, 0)),
        ],
        out_shape=[
            jax.ShapeDtypeStruct((B, S, D), F32),
            jax.ShapeDtypeStruct((1, B, N_HEADS, DH, DH), F32),
            jax.ShapeDtypeStruct((1, B, N_HEADS, DH), F32),
            jax.ShapeDtypeStruct((B, 1, N_HEADS), F32),
            jax.ShapeDtypeStruct((1, B, CONV_K - 1, D), F32),
        ],
        scratch_shapes=[
            pltpu.VMEM((T, D), BF16), pltpu.VMEM((T, D), BF16),
            pltpu.VMEM((T + CONV_LEAD + SUBLANES, D), F32), pltpu.VMEM((T, D), F32),
            pltpu.VMEM((SUBLANES, LANES), F32), pltpu.VMEM((CONV_K, SUBLANES, D), F32),
        ],
        compiler_params=pltpu.CompilerParams(
            dimension_semantics=("arbitrary", "arbitrary"), vmem_limit_bytes=VMEM_LIMIT),
        name="prompt_mixer",
    )(x, mixg, w_all, bi, bf, hg, wmo, cw, cb, lng, lnb, wco, wout)


def _ffn_tail(upc_a, upc_g, wdown_ref, x, fing_ref):
    act = (upc_a * _sigmoid(upc_a) * upc_g).astype(BF16)
    x2 = x + _mm(act, wdown_ref[...])
    return _rmsnorm(x2, fing_ref[...])


def _prompt_ffn_kernel(x_ref, fg_ref, wup_ref, fw_ref, fb_ref, wdown_ref, fing_ref,
                       y_ref, ffn_ref, up_s, act_s):
    T = x_ref.shape[1]
    c = pl.program_id(1)

    @pl.when(c == 0)
    def _():
        up_s[0:8, :] = jnp.zeros((8, 2 * D_FF), F32)

    x = x_ref[0]
    hf = _rmsnorm(x, fg_ref[...]).astype(BF16)
    nb = 2 * D_FF // DH
    for blk in range(nb):
        cs = slice(blk * DH, (blk + 1) * DH)
        up_s[8:8 + T, cs] = _mm(hf, wup_ref[:, cs])
    for blk in range(D_FF // DH):
        ca = slice(blk * DH, (blk + 1) * DH)
        cg = slice(D_FF + blk * DH, D_FF + (blk + 1) * DH)
        a = fb_ref[:, ca] + fw_ref[0:1, ca] * up_s[6:6 + T, ca] + fw_ref[1:2, ca] * up_s[7:7 + T, ca] \
            + fw_ref[2:3, ca] * up_s[8:8 + T, ca]
        g = fb_ref[:, cg] + fw_ref[0:1, cg] * up_s[6:6 + T, cg] + fw_ref[1:2, cg] * up_s[7:7 + T, cg] \
            + fw_ref[2:3, cg] * up_s[8:8 + T, cg]
        act_s[:, ca] = (a * _sigmoid(a) * g).astype(BF16)
    tail = up_s[T + 6:T + 8, :]
    ffn_ref[0] = tail
    up_s[6:8, :] = tail
    x2 = x + _mm(act_s[...], wdown_ref[...])
    y_ref[0] = _rmsnorm(x2, fing_ref[...])


def _prompt_ffn(x1, fg, wup, fw, fb, wdown, fing):
    B, S, D = x1.shape
    T = PROMPT_CHUNK
    nc = S // T
    return pl.pallas_call(
        _prompt_ffn_kernel,
        grid=(B, nc),
        in_specs=[
            pl.BlockSpec((1, T, D), lambda b, c: (b, c, 0)),
            _const_spec((1, D)), _const_spec((D, 2 * D_FF)), _const_spec((FFN_K, 2 * D_FF)),
            _const_spec((1, 2 * D_FF)), _const_spec((D_FF, D)), _const_spec((1, D)),
        ],
        out_specs=[
            pl.BlockSpec((1, T, D), lambda b, c: (b, c, 0)),
            pl.BlockSpec((None, 1, FFN_K - 1, 2 * D_FF), lambda b, c: (0, b, 0, 0)),
        ],
        out_shape=[
            jax.ShapeDtypeStruct((B, S, D), F32),
            jax.ShapeDtypeStruct((1, B, FFN_K - 1, 2 * D_FF), F32),
        ],
        scratch_shapes=[pltpu.VMEM((T + 8, 2 * D_FF), F32), pltpu.VMEM((T, D_FF), BF16)],
        compiler_params=pltpu.CompilerParams(
            dimension_semantics=("arbitrary", "arbitrary"), vmem_limit_bytes=VMEM_LIMIT),
        name="prompt_ffn",
    )(x1, fg, wup, fw, fb, wdown, fing)


def _sample_proj_kernel(x_ref, mixg_ref, w_ref, z_ref):
    xn = _rmsnorm(x_ref[...], mixg_ref[...]).astype(BF16)
    for blk in range(N_Z // DH):
        cs = slice(blk * DH, (blk + 1) * DH)
        z_ref[:, cs] = _mm(xn, w_ref[:, cs])


def _sample_proj(x, mixg, w_all):
    R, D = x.shape
    return pl.pallas_call(
        _sample_proj_kernel,
        grid=(R // SAMPLE_ROWS,),
        in_specs=[pl.BlockSpec((SAMPLE_ROWS, D), lambda i: (i, 0)),
                  _const_spec((1, D)), _const_spec((D, N_Z))],
        out_specs=pl.BlockSpec((SAMPLE_ROWS, N_Z), lambda i: (i, 0)),
        out_shape=jax.ShapeDtypeStruct((R, N_Z), F32),
        compiler_params=pltpu.CompilerParams(
            dimension_semantics=("arbitrary",), vmem_limit_bytes=VMEM_LIMIT),
        name="sample_proj",
    )(x, mixg, w_all)


def _sample_mlstm_kernel(zq_ref, zg_ref, bi_ref, bf_ref, c_ref, n_ref, m_ref,
                         h_ref, co_ref, no_ref, mo_ref, *, t_seq):
    per_win = SUBLANES // t_seq
    pad = LANES - SUBLANES
    for win in range(MLSTM_SEQS // per_win):
        zq_w = zq_ref[SUBLANES * win:SUBLANES * (win + 1), :]
        zg_w = zg_ref[SUBLANES * win:SUBLANES * (win + 1), :]
        for sub in range(per_win):
            j = win * per_win + sub
            if sub:
                zq = pltpu.roll(zq_w, SUBLANES - sub * t_seq, axis=0)
                zg = pltpu.roll(zg_w, SUBLANES - sub * t_seq, axis=0)
            else:
                zq, zg = zq_w, zg_w
            zi = zg[:, 0:LANES] + bi_ref[...]
            lf = _log_sigmoid(zg[:, LANES:2 * LANES] + bf_ref[...])
            b_all = _cumsum_rows(lf)
            a_all = jnp.concatenate([zi - b_all, jnp.zeros((pad, LANES), F32)], axis=0)
            a_t = a_all.T
            for h in range(N_HEADS):
                q = zq[:, Q0 + h * DH:Q0 + (h + 1) * DH]
                k = zq[:, K0 + h * DH:K0 + (h + 1) * DH] * (DH ** -0.5)
                v = zq[:, V0 + h * DH:V0 + (h + 1) * DH]
                zeros = jnp.zeros((pad, DH), F32)
                k = jnp.concatenate([k, zeros], axis=0)
                v = jnp.concatenate([v, zeros], axis=0)
                hh, c_new, n_new, m_new = _mlstm_head(
                    q, k, v, b_all[:, h:h + 1], a_all[:, h:h + 1], a_t[h:h + 1, :],
                    m_ref[j:j + 1, h:h + 1], c_ref[j, h], n_ref[j, h:h + 1, :], t_seq)
                co_ref[j, h] = c_new
                no_ref[j, h:h + 1, :] = n_new
                mo_ref[j:j + 1, h:h + 1] = m_new
                h_ref[j * t_seq:(j + 1) * t_seq, h * DH:(h + 1) * DH] = hh[0:t_seq, :]


def _sample_mlstm(z, bi, bf, c0, n0, m0, t_seq):
    nb = c0.shape[1]
    sb = MLSTM_SEQS
    rows = sb * t_seq
    kern = functools.partial(_sample_mlstm_kernel, t_seq=t_seq)
    c_spec = pl.BlockSpec((None, sb, N_HEADS, DH, DH), lambda i: (0, i, 0, 0, 0))
    n_spec = pl.BlockSpec((None, sb, N_HEADS, DH), lambda i: (0, i, 0, 0))
    m_spec = pl.BlockSpec((None, sb, N_HEADS), lambda i: (0, i, 0))
    return pl.pallas_call(
        kern,
        grid=(nb // sb,),
        in_specs=[
            pl.BlockSpec((rows, 3 * D_MODEL), lambda i: (i, 0)),
            pl.BlockSpec((rows, 2 * LANES), lambda i: (i, IG0 // (2 * LANES))),
            _const_spec((1, LANES)), _const_spec((1, LANES)),
            c_spec, n_spec, m_spec,
        ],
        out_specs=[pl.BlockSpec((rows, D_MODEL), lambda i: (i, 0)), c_spec, n_spec, m_spec],
        out_shape=[
            jax.ShapeDtypeStruct((nb * t_seq, D_MODEL), F32),
            jax.ShapeDtypeStruct(c0.shape, F32),
            jax.ShapeDtypeStruct(n0.shape, F32),
            jax.ShapeDtypeStruct(m0.shape, F32),
        ],
        compiler_params=pltpu.CompilerParams(
            dimension_semantics=("arbitrary",), vmem_limit_bytes=VMEM_LIMIT),
        name="sample_mlstm",
    )(z, z, bi, bf, c0, n0, m0)


def _sample_mixer_tail_kernel(x_ref, z_ref, h_ref, cst_ref, hg_ref, wmo_ref, cw4_ref, cb_ref,
                              lng_ref, lnb_ref, wco_ref, wout_ref,
                              x1_ref, cnew_ref, full_s, uc_s, *, t_seq):
    R = x_ref.shape[0]
    n_seq = R // t_seq
    hist = CONV_K - 1
    ha = jnp.concatenate(
        [_head_out(h_ref[:, h * DH:(h + 1) * DH], hg_ref[:, h * DH:(h + 1) * DH],
                   z_ref[:, O0 + h * DH:O0 + (h + 1) * DH]) for h in range(N_HEADS)], axis=1)
    y_a = _mm(ha.astype(BF16), wmo_ref[...])

    uc_s[...] = z_ref[:, GV0:GV0 + D_MODEL] * _sigmoid(z_ref[:, GG0:GG0 + D_MODEL])
    full_s[...] = jnp.zeros_like(full_s)
    for j in range(n_seq):
        full_s[0:hist, :] = cst_ref[j]
        full_s[hist:hist + t_seq, :] = uc_s[j * t_seq:(j + 1) * t_seq, :]
        full = full_s[...]
        cnew_ref[j] = full_s[t_seq:t_seq + hist, :]
        for t in range(t_seq):
            uc_s[j * t_seq + t:j * t_seq + t + 1, :] = (
                jnp.sum(cw4_ref[t] * full, axis=0, keepdims=True) + cb_ref[...])

    ucn = _layernorm(uc_s[...], lng_ref[...], lnb_ref[...])
    y_b = _mm((ucn * _sigmoid(ucn)).astype(BF16), wco_ref[...])
    mix = _sigmoid(z_ref[:, GA0:GA0 + D_MODEL]) * y_a + _sigmoid(z_ref[:, GB0:GB0 + D_MODEL]) * y_b
    x1_ref[...] = x_ref[...] + _mm(mix.astype(BF16), wout_ref[...])


def _sample_mixer_tail(x, z, h, cst, hg, wmo, cw4, cb, lng, lnb, wco, wout, t_seq):
    R, D = x.shape
    rows = SAMPLE_ROWS
    sb = rows // t_seq
    hist = CONV_K - 1
    win = cw4.shape[1]
    kern = functools.partial(_sample_mixer_tail_kernel, t_seq=t_seq)
    st_spec = pl.BlockSpec((None, sb, hist, D), lambda i: (0, i, 0, 0))
    return pl.pallas_call(
        kern,
        grid=(R // rows,),
        in_specs=[
            pl.BlockSpec((rows, D), lambda i: (i, 0)),
            pl.BlockSpec((rows, N_Z), lambda i: (i, 0)),
            pl.BlockSpec((rows, D), lambda i: (i, 0)),
            st_spec,
            _const_spec((1, D)), _const_spec((D, D)), _const_spec((t_seq, win, D)), _const_spec((1, D)),
            _const_spec((1, D)), _const_spec((1, D)), _const_spec((D, D)), _const_spec((D, D)),
        ],
        out_specs=[pl.BlockSpec((rows, D), lambda i: (i, 0)), st_spec],
        out_shape=[
            jax.ShapeDtypeStruct((R, D), F32),
            jax.ShapeDtypeStruct(cst.shape, F32),
        ],
        scratch_shapes=[pltpu.VMEM((win, D), F32), pltpu.VMEM((rows, D), F32)],
        compiler_params=pltpu.CompilerParams(
            dimension_semantics=("arbitrary",), vmem_limit_bytes=VMEM_LIMIT),
        name="sample_mixer_tail",
    )(x, z, h, cst, hg, wmo, cw4, cb, lng, lnb, wco, wout)


def _sample_ffn_kernel(x_ref, fst_ref, fg_ref, wup_ref, fw_ref, fb_ref, wdown_ref, fing_ref,
                       y_ref, fnew_ref, up_s, seq_s, *, t_seq):
    R = x_ref.shape[0]
    n_seq = R // t_seq
    hist = FFN_K - 1
    x = x_ref[...]
    hf = _rmsnorm(x, fg_ref[...]).astype(BF16)
    for blk in range(2 * D_FF // DH):
        cs = slice(blk * DH, (blk + 1) * DH)
        up_s[:, cs] = _mm(hf, wup_ref[:, cs])
    for j in range(n_seq):
        rows = slice(j * t_seq, (j + 1) * t_seq)
        seq_s[0:hist, :] = fst_ref[j]
        seq_s[hist:hist + t_seq, :] = up_s[rows, :]
        fnew_ref[j] = seq_s[t_seq:t_seq + hist, :]
        acc = fb_ref[...] + fw_ref[0:1, :] * seq_s[0:t_seq, :]
        for kk in range(1, FFN_K):
            acc = acc + fw_ref[kk:kk + 1, :] * seq_s[kk:kk + t_seq, :]
        up_s[rows, :] = acc
    y_ref[...] = _ffn_tail(up_s[:, 0:D_FF], up_s[:, D_FF:2 * D_FF], wdown_ref, x, fing_ref)


def _sample_ffn(x1, fst, fg, wup, fw, fb, wdown, fing, t_seq):
    R, D = x1.shape
    rows = SAMPLE_ROWS
    sb = rows // t_seq
    hist = FFN_K - 1
    kern = functools.partial(_sample_ffn_kernel, t_seq=t_seq)
    st_spec = pl.BlockSpec((None, sb, hist, 2 * D_FF), lambda i: (0, i, 0, 0))
    return pl.pallas_call(
        kern,
        grid=(R // rows,),
        in_specs=[
            pl.BlockSpec((rows, D), lambda i: (i, 0)),
            st_spec,
            _const_spec((1, D)), _const_spec((D, 2 * D_FF)), _const_spec((FFN_K, 2 * D_FF)),
            _const_spec((1, 2 * D_FF)), _const_spec((D_FF, D)), _const_spec((1, D)),
        ],
        out_specs=[pl.BlockSpec((rows, D), lambda i: (i, 0)), st_spec],
        out_shape=[
            jax.ShapeDtypeStruct((R, D), F32),
            jax.ShapeDtypeStruct(fst.shape, F32),
        ],
        scratch_shapes=[pltpu.VMEM((rows, 2 * D_FF), F32), pltpu.VMEM((SUBLANES, 2 * D_FF), F32)],
        compiler_params=pltpu.CompilerParams(
            dimension_semantics=("arbitrary",), vmem_limit_bytes=VMEM_LIMIT),
        name="sample_ffn",
    )(x1, fst, fg, wup, fw, fb, wdown, fing)


def _pad_gate_bias(b):
    return jnp.pad(b, (0, LANES - N_HEADS)).reshape(1, LANES)


def _shifted_conv_weights(cw, t_seq):
    k = cw.shape[0]
    win = -(-(k - 1 + t_seq) // SUBLANES) * SUBLANES
    return jnp.stack([jnp.pad(cw, ((t, win - k - t), (0, 0))) for t in range(t_seq)])


def kernel(x_prompt, x_sample, state_C, state_n, state_m, state_conv, state_ffn, mix_norm_g, w_in, b_if,
           head_norm_g, w_mlstm_out, conv_w, conv_b, conv_ln_g, conv_ln_b, w_conv_out, w_out, ffn_norm_g,
           w_up, ffn_conv_w, ffn_conv_b, w_down, final_norm_g):
    depth = w_in.shape[0]
    assert depth == 1, "single-layer trunk"
    l = 0
    row = lambda a: a.reshape(1, -1)
    w_all = _pack_w_in(w_in)
    bi = _pad_gate_bias(b_if[l, :N_HEADS])
    bf = _pad_gate_bias(b_if[l, N_HEADS:])
    mixg, hg = row(mix_norm_g[l]), row(head_norm_g[l])
    wmo, wco, wout = (w.astype(BF16) for w in (w_mlstm_out[l], w_conv_out[l], w_out[l]))
    cw, cb = conv_w[l], row(conv_b[l])
    lng, lnb = row(conv_ln_g[l]), row(conv_ln_b[l])
    fg, fing = row(ffn_norm_g[l]), row(final_norm_g)
    wup, wdown = w_up[l].astype(BF16), w_down[l].astype(BF16)
    fw, fb = ffn_conv_w[l], row(ffn_conv_b[l])

    x1p, c_p, n_p, m_p, conv_p = _prompt_mixer(x_prompt, mixg, w_all, bi, bf, hg, wmo, cw, cb, lng, lnb, wco, wout)
    y_p, ffn_p = _prompt_ffn(x1p, fg, wup, fw, fb, wdown, fing)

    nb, t_seq, d = x_sample.shape
    xs = x_sample.reshape(nb * t_seq, d)
    z = _sample_proj(xs, mixg, w_all)
    h_s, c_s, n_s, m_s = _sample_mlstm(z, bi, bf, state_C, state_n, state_m, t_seq)
    cw4 = _shifted_conv_weights(cw, t_seq)
    x1s, conv_s = _sample_mixer_tail(xs, z, h_s, state_conv, hg, wmo, cw4, cb, lng, lnb, wco, wout, t_seq)
    y_s, ffn_s = _sample_ffn(x1s, state_ffn, fg, wup, fw, fb, wdown, fing, t_seq)

    return (y_p, y_s.reshape(nb, t_seq, d),
            c_p, n_p, m_p.reshape(1, -1, N_HEADS), conv_p, ffn_p,
            c_s, n_s, m_s, conv_s, ffn_s)
```

```python
import functools

import jax
import jax.numpy as jnp
from jax import lax
from jax.experimental import pallas as pl
from jax.experimental.pallas import tpu as pltpu

F32 = jnp.float32
BF16 = jnp.bfloat16

D_MODEL = 1024
N_HEADS = 4
DH = D_MODEL // N_HEADS
CONV_K = 31
D_FF = 2816
FFN_K = 3
EPS = 1e-6

LANES = 128
Q0, K0, V0, O0, GV0, GG0, GA0, GB0 = (i * D_MODEL for i in range(8))
IG0 = 8 * D_MODEL
FG0 = IG0 + LANES
N_Z = FG0 + LANES

SUBLANES = 8
CONV_LEAD = 32
PROMPT_CHUNK = 256
SAMPLE_ROWS = 128
MLSTM_SEQS = 8
PACK_ROWS = 128
VMEM_LIMIT = 58 * 1024 * 1024


def _mm(a, b):
    return jnp.dot(a, b, preferred_element_type=F32)


def _mm_nt(a, b):
    return lax.dot_general(a, b, (((1,), (1,)), ((), ())), preferred_element_type=F32)


def _mm_tn(a, b):
    return lax.dot_general(a, b, (((0,), (0,)), ((), ())), preferred_element_type=F32)


def _rmsnorm(x, g):
    return x * lax.rsqrt(jnp.mean(x * x, axis=-1, keepdims=True) + EPS) * g


def _layernorm(x, g, b):
    mu = jnp.mean(x, axis=-1, keepdims=True)
    xc = x - mu
    return xc * lax.rsqrt(jnp.mean(xc * xc, axis=-1, keepdims=True) + EPS) * g + b


def _sigmoid(x):
    return 0.5 * jnp.tanh(0.5 * x) + 0.5


def _log_sigmoid(x):
    return jnp.minimum(x, 0.0) - jnp.log1p(jnp.exp(-jnp.abs(x)))


def _cumsum_rows(x):
    n = x.shape[0]
    row = lax.broadcasted_iota(jnp.int32, x.shape, 0)
    s = 1
    while s < n:
        x = x + jnp.where(row >= s, pltpu.roll(x, s, axis=0), 0.0)
        s *= 2
    return x


def _mlstm_head_steps(q, k, v, b_col, a_col, a_row, m_prev, c_prev, n_prev, n_valid, out):
    L, LK = q.shape[0], k.shape[0]
    row = lax.broadcasted_iota(jnp.int32, (L, LK), 0)
    col = lax.broadcasted_iota(jnp.int32, (L, LK), 1)
    d = jnp.where(col <= row, b_col + a_row, -jnp.inf)
    inter = b_col + m_prev
    d_max = jnp.max(d, axis=1, keepdims=True)
    qb, kb, vb = q.astype(BF16), k.astype(BF16), v.astype(BF16)
    qk = _mm_nt(qb, kb)
    yield
    m_t = jnp.maximum(inter, d_max)
    w_inter = jnp.exp(inter - m_t)
    p = jnp.exp(d - m_t)
    s = qk * p
    qc = _mm_nt(qb, c_prev().astype(BF16))
    yield
    num = w_inter * qc + _mm(s.astype(BF16), vb)
    den = w_inter * jnp.sum(q * n_prev, axis=1, keepdims=True) + jnp.sum(s, axis=1, keepdims=True)
    m_new = m_t[n_valid - 1:n_valid, :]
    b_last = b_col[n_valid - 1:n_valid, :]
    g_inter = jnp.exp(b_last + m_prev - m_new)
    g_col = jnp.exp(b_last + a_col - m_new)
    if n_valid < LK:
        g_col = jnp.where(lax.broadcasted_iota(jnp.int32, (LK, 1), 0) < n_valid, g_col, 0.0)
    yield
    out["h"] = num * (1.0 / jnp.maximum(jnp.abs(den), jnp.exp(-m_t)))
    out["c"] = g_inter * c_prev() + _mm_tn((g_col * v).astype(BF16), kb)
    out["n"] = g_inter * n_prev + jnp.sum(g_col * k, axis=0, keepdims=True)
    out["m"] = m_new
    yield


def _mlstm_head(q, k, v, b_col, a_col, a_row, m_prev, c_prev, n_prev, n_valid):
    out = {}
    for _ in _mlstm_head_steps(q, k, v, b_col, a_col, a_row, m_prev, c_prev, n_prev, n_valid, out):
        pass
    return out["h"], out["c"], out["n"], out["m"]


def _round_robin(streams):
    streams = list(streams)
    while streams:
        alive = []
        for gen in streams:
            try:
                next(gen)
                alive.append(gen)
            except StopIteration:
                pass
        streams = alive


def _head_out(h, head_g, o):
    mu = jnp.mean(h, axis=-1, keepdims=True)
    hc = h - mu
    hn = hc * lax.rsqrt(jnp.mean(hc * hc, axis=-1, keepdims=True) + EPS)
    return hn * head_g * _sigmoid(o)


def _causal_conv_cols(full_v, cwb_s, cb_ref, out_v, cs, T):
    off = CONV_LEAD - (CONV_K - 1)
    n_a = (off + CONV_K - 1) // SUBLANES + 1
    width = cs.stop - cs.start
    sub = lax.broadcasted_iota(jnp.int32, (SUBLANES, width), 0)
    bias = cb_ref[:, cs]
    tiles = {}

    def tile(i):
        if i not in tiles:
            tiles[i] = full_v[SUBLANES * i:SUBLANES * (i + 1), :]
        return tiles[i]

    prev = None
    for i in range(T // SUBLANES + 1):
        ys = []
        for r in range(SUBLANES):
            acc = None
            for a in range(n_a):
                j = SUBLANES * a + r - off
                if 0 <= j < CONV_K:
                    term = cwb_s[j, :, cs] * tile(i + a)
                    acc = term if acc is None else acc + term
            ys.append(acc)
        tiles.pop(i, None)
        cur = [ys[0]] + [pltpu.roll(ys[r], SUBLANES - r, axis=0) for r in range(1, SUBLANES)]
        if prev is not None:
            out = prev[0] + bias
            for r in range(1, SUBLANES):
                out = out + jnp.where(sub < SUBLANES - r, prev[r], cur[r])
            out_v[SUBLANES * (i - 1):SUBLANES * i, :] = out
        prev = cur


def _const_spec(shape):
    nd = len(shape)
    return pl.BlockSpec(shape, lambda *_: (0,) * nd, pipeline_mode=pl.Buffered(1))


def _pack_kernel(w_ref, o_ref):
    split = 4 * D_MODEL
    n_gate = 2 * N_HEADS
    rows = w_ref.shape[1]
    for src, dst in ((0, 0), (split + n_gate, split)):
        for t in range(split // LANES):
            blk = w_ref[src + t * LANES:src + (t + 1) * LANES, :]
            o_ref[:, dst + t * LANES:dst + (t + 1) * LANES] = blk.T.astype(BF16)
    g = jnp.concatenate([w_ref[split:split + n_gate, :], jnp.zeros((LANES - n_gate, rows), F32)], axis=0).T
    lane = lax.broadcasted_iota(jnp.int32, g.shape, 1)
    o_ref[:, IG0:IG0 + LANES] = jnp.where(lane < N_HEADS, g, 0.0).astype(BF16)
    o_ref[:, FG0:FG0 + LANES] = jnp.where(
        lane < N_HEADS, pltpu.roll(g, LANES - N_HEADS, axis=1), 0.0).astype(BF16)


def _pack_w_in(w_t):
    _, n_in, d = w_t.shape
    return pl.pallas_call(
        _pack_kernel,
        grid=(d // PACK_ROWS,),
        in_specs=[pl.BlockSpec((None, n_in, PACK_ROWS), lambda i: (0, 0, i))],
        out_specs=pl.BlockSpec((PACK_ROWS, N_Z), lambda i: (i, 0)),
        out_shape=jax.ShapeDtypeStruct((d, N_Z), BF16),
        compiler_params=pltpu.CompilerParams(
            dimension_semantics=("arbitrary",), vmem_limit_bytes=VMEM_LIMIT),
        name="pack_w_in",
    )(w_t)


def _prompt_mixer_kernel(x_ref, mixg_ref, w_ref, bi_ref, bf_ref, hg_ref, wmo_ref, cw_ref, cb_ref,
                         lng_ref, lnb_ref, wco_ref, wout_ref,
                         x1_ref, c_ref, n_ref, m_ref, conv_ref,
                         xn_s, ha_s, full_s, uc_s, m_s, cwb_s, *, n_chunks):
    T = x_ref.shape[1]
    c = pl.program_id(1)

    @pl.when(c == 0)
    def _():
        c_ref[...] = jnp.zeros_like(c_ref)
        n_ref[...] = jnp.zeros_like(n_ref)
        m_s[...] = jnp.zeros_like(m_s)
        full_s[0:CONV_LEAD, :] = jnp.zeros((CONV_LEAD, D_MODEL), F32)
        full_s[CONV_LEAD + T:CONV_LEAD + T + SUBLANES, :] = jnp.zeros((SUBLANES, D_MODEL), F32)
        for j in range(CONV_K):
            cwb_s[j] = jnp.broadcast_to(cw_ref[j:j + 1, :], (SUBLANES, D_MODEL))

    x = x_ref[0]
    xn_s[...] = _rmsnorm(x, mixg_ref[...]).astype(BF16)
    xn = xn_s[...]

    zi = _mm(xn, w_ref[:, IG0:IG0 + LANES]) + bi_ref[...]
    lf = _log_sigmoid(_mm(xn, w_ref[:, FG0:FG0 + LANES]) + bf_ref[...])
    b_all = _cumsum_rows(lf)
    a_all = zi - b_all
    a_t = a_all.T

    for h in range(N_HEADS):
        cs = slice(h * DH, (h + 1) * DH)
        gv = _mm(xn, w_ref[:, GV0 + h * DH:GV0 + (h + 1) * DH])
        gg = _mm(xn, w_ref[:, GG0 + h * DH:GG0 + (h + 1) * DH])
        full_s[CONV_LEAD:CONV_LEAD + T, cs] = gv * _sigmoid(gg)
        q = _mm(xn, w_ref[:, Q0 + h * DH:Q0 + (h + 1) * DH])
        k = _mm(xn, w_ref[:, K0 + h * DH:K0 + (h + 1) * DH]) * (DH ** -0.5)
        v = _mm(xn, w_ref[:, V0 + h * DH:V0 + (h + 1) * DH])
        _causal_conv_cols(full_s.at[:, cs], cwb_s, cb_ref, uc_s.at[:, cs], cs, T)
        hh, c_new, n_new, m_new = _mlstm_head(
            q, k, v, b_all[:, h:h + 1], a_all[:, h:h + 1], a_t[h:h + 1, :],
            m_s[h:h + 1, 0:1], functools.partial(c_ref.__getitem__, (0, h)), n_ref[0, h:h + 1, :], T)
        c_ref[0, h] = c_new
        n_ref[0, h:h + 1, :] = n_new
        m_s[h:h + 1, :] = jnp.broadcast_to(m_new, (1, LANES))
        o = _mm(xn, w_ref[:, O0 + h * DH:O0 + (h + 1) * DH])
        ha_s[:, cs] = _head_out(hh, hg_ref[:, cs], o).astype(BF16)

    y_a = _mm(ha_s[...], wmo_ref[...])
    tail = full_s[T + CONV_LEAD - (CONV_K - 1):T + CONV_LEAD, :]
    conv_ref[0] = tail
    full_s[CONV_LEAD - (CONV_K - 1):CONV_LEAD, :] = tail

    ucn = _layernorm(uc_s[...], lng_ref[...], lnb_ref[...])
    y_b = _mm((ucn * _sigmoid(ucn)).astype(BF16), wco_ref[...])

    ga = _mm(xn, w_ref[:, GA0:GA0 + D_MODEL])
    gb = _mm(xn, w_ref[:, GB0:GB0 + D_MODEL])
    mix = _sigmoid(ga) * y_a + _sigmoid(gb) * y_b
    x1_ref[0] = x + _mm(mix.astype(BF16), wout_ref[...])

    @pl.when(c == n_chunks - 1)
    def _():
        for h in range(N_HEADS):
            m_ref[0, :, h:h + 1] = m_s[h:h + 1, 0:1]


def _prompt_mixer(x, mixg, w_all, bi, bf, hg, wmo, cw, cb, lng, lnb, wco, wout):
    B, S, D = x.shape
    T = PROMPT_CHUNK
    nc = S // T
    kern = functools.partial(_prompt_mixer_kernel, n_chunks=nc)
    return pl.pallas_call(
        kern,
        grid=(B, nc),
        in_specs=[
            pl.BlockSpec((1, T, D), lambda b, c: (b, c, 0)),
            _const_spec((1, D)), _const_spec((D, N_Z)), _const_spec((1, LANES)), _const_spec((1, LANES)),
            _const_spec((1, D)), _const_spec((D, D)), _const_spec((CONV_K, D)), _const_spec((1, D)),
            _const_spec((1, D)), _const_spec((1, D)), _const_spec((D, D)), _const_spec((D, D)),
        ],
        out_specs=[
            pl.BlockSpec((1, T, D), lambda b, c: (b, c, 0)),
            pl.BlockSpec((None, 1, N_HEADS, DH, DH), lambda b, c: (0, b, 0, 0, 0)),
            pl.BlockSpec((None, 1, N_HEADS, DH), lambda b, c: (0, b, 0, 0)),
            pl.BlockSpec((1, 1, N_HEADS), lambda b, c: (b, 0, 0)),
            pl.BlockSpec((None, 1, CONV_K - 1, D), lambda b, c: (0, b, 0, 0)),
        ],
        out_shape=[
            jax.ShapeDtypeStruct((B, S, D), F32),
            jax.ShapeDtypeStruct((1, B, N_HEADS, DH, DH), F32),
            jax.ShapeDtypeStruct((1, B, N_HEADS, DH), F32),
            jax.ShapeDtypeStruct((B, 1, N_HEADS), F32),
            jax.ShapeDtypeStruct((1, B, CONV_K - 1, D), F32),
        ],
        scratch_shapes=[
            pltpu.VMEM((T, D), BF16), pltpu.VMEM((T, D), BF16),
            pltpu.VMEM((T + CONV_LEAD + SUBLANES, D), F32), pltpu.VMEM((T, D), F32),
            pltpu.VMEM((SUBLANES, LANES), F32), pltpu.VMEM((CONV_K, SUBLANES, D), F32),
        ],
        compiler_params=pltpu.CompilerParams(
            dimension_semantics=("arbitrary", "arbitrary"), vmem_limit_bytes=VMEM_LIMIT),
        name="prompt_mixer",
    )(x, mixg, w_all, bi, bf, hg, wmo, cw, cb, lng, lnb, wco, wout)


def _ffn_tail(upc_a, upc_g, wdown_ref, x, fing_ref):
    act = (upc_a * _sigmoid(upc_a) * upc_g).astype(BF16)
    x2 = x + _mm(act, wdown_ref[...])
    return _rmsnorm(x2, fing_ref[...])


def _prompt_ffn_kernel(x_ref, fg_ref, wup_ref, fw_ref, fb_ref, wdown_ref, fing_ref,
                       y_ref, ffn_ref, up_s, act_s):
    T = x_ref.shape[1]
    c = pl.program_id(1)

    @pl.when(c == 0)
    def _():
        up_s[0:8, :] = jnp.zeros((8, 2 * D_FF), F32)

    x = x_ref[0]
    hf = _rmsnorm(x, fg_ref[...]).astype(BF16)
    nb = 2 * D_FF // DH
    for blk in range(nb):
        cs = slice(blk * DH, (blk + 1) * DH)
        up_s[8:8 + T, cs] = _mm(hf, wup_ref[:, cs])
    for blk in range(D_FF // DH):
        ca = slice(blk * DH, (blk + 1) * DH)
        cg = slice(D_FF + blk * DH, D_FF + (blk + 1) * DH)
        a = fb_ref[:, ca] + fw_ref[0:1, ca] * up_s[6:6 + T, ca] + fw_ref[1:2, ca] * up_s[7:7 + T, ca] \
            + fw_ref[2:3, ca] * up_s[8:8 + T, ca]
        g = fb_ref[:, cg] + fw_ref[0:1, cg] * up_s[6:6 + T, cg] + fw_ref[1:2, cg] * up_s[7:7 + T, cg] \
            + fw_ref[2:3, cg] * up_s[8:8 + T, cg]
        act_s[:, ca] = (a * _sigmoid(a) * g).astype(BF16)
    tail = up_s[T + 6:T + 8, :]
    ffn_ref[0] = tail
    up_s[6:8, :] = tail
    x2 = x + _mm(act_s[...], wdown_ref[...])
    y_ref[0] = _rmsnorm(x2, fing_ref[...])


def _prompt_ffn(x1, fg, wup, fw, fb, wdown, fing):
    B, S, D = x1.shape
    T = PROMPT_CHUNK
    nc = S // T
    return pl.pallas_call(
        _prompt_ffn_kernel,
        grid=(B, nc),
        in_specs=[
            pl.BlockSpec((1, T, D), lambda b, c: (b, c, 0)),
            _const_spec((1, D)), _const_spec((D, 2 * D_FF)), _const_spec((FFN_K, 2 * D_FF)),
            _const_spec((1, 2 * D_FF)), _const_spec((D_FF, D)), _const_spec((1, D)),
        ],
        out_specs=[
            pl.BlockSpec((1, T, D), lambda b, c: (b, c, 0)),
            pl.BlockSpec((None, 1, FFN_K - 1, 2 * D_FF), lambda b, c: (0, b, 0, 0)),
        ],
        out_shape=[
            jax.ShapeDtypeStruct((B, S, D), F32),
            jax.ShapeDtypeStruct((1, B, FFN_K - 1, 2 * D_FF), F32),
        ],
        scratch_shapes=[pltpu.VMEM((T + 8, 2 * D_FF), F32), pltpu.VMEM((T, D_FF), BF16)],
        compiler_params=pltpu.CompilerParams(
            dimension_semantics=("arbitrary", "arbitrary"), vmem_limit_bytes=VMEM_LIMIT),
        name="prompt_ffn",
    )(x1, fg, wup, fw, fb, wdown, fing)


def _sample_proj_kernel(x_ref, mixg_ref, w_ref, z_ref):
    xn = _rmsnorm(x_ref[...], mixg_ref[...]).astype(BF16)
    for blk in range(N_Z // DH):
        cs = slice(blk * DH, (blk + 1) * DH)
        z_ref[:, cs] = _mm(xn, w_ref[:, cs])


def _sample_proj(x, mixg, w_all):
    R, D = x.shape
    return pl.pallas_call(
        _sample_proj_kernel,
        grid=(R // SAMPLE_ROWS,),
        in_specs=[pl.BlockSpec((SAMPLE_ROWS, D), lambda i: (i, 0)),
                  _const_spec((1, D)), _const_spec((D, N_Z))],
        out_specs=pl.BlockSpec((SAMPLE_ROWS, N_Z), lambda i: (i, 0)),
        out_shape=jax.ShapeDtypeStruct((R, N_Z), F32),
        compiler_params=pltpu.CompilerParams(
            dimension_semantics=("arbitrary",), vmem_limit_bytes=VMEM_LIMIT),
        name="sample_proj",
    )(x, mixg, w_all)


def _sample_mlstm_kernel(zq_ref, zg_ref, bi_ref, bf_ref, c_ref, n_ref, m_ref,
                         h_ref, co_ref, no_ref, mo_ref, *, t_seq):
    per_win = SUBLANES // t_seq
    pad = LANES - SUBLANES
    items = []
    for win in range(MLSTM_SEQS // per_win):
        zq_w = zq_ref[SUBLANES * win:SUBLANES * (win + 1), :]
        zg_w = zg_ref[SUBLANES * win:SUBLANES * (win + 1), :]
        for sub in range(per_win):
            j = win * per_win + sub
            if sub:
                zq = pltpu.roll(zq_w, SUBLANES - sub * t_seq, axis=0)
                zg = pltpu.roll(zg_w, SUBLANES - sub * t_seq, axis=0)
            else:
                zq, zg = zq_w, zg_w
            zi = zg[:, 0:LANES] + bi_ref[...]
            lf = _log_sigmoid(zg[:, LANES:2 * LANES] + bf_ref[...])
            b_all = _cumsum_rows(lf)
            a_all = jnp.concatenate([zi - b_all, jnp.zeros((pad, LANES), F32)], axis=0)
            a_t = a_all.T
            for h in range(N_HEADS):
                q = zq[:, Q0 + h * DH:Q0 + (h + 1) * DH]
                k = zq[:, K0 + h * DH:K0 + (h + 1) * DH] * (DH ** -0.5)
                v = zq[:, V0 + h * DH:V0 + (h + 1) * DH]
                zeros = jnp.zeros((pad, DH), F32)
                k = jnp.concatenate([k, zeros], axis=0)
                v = jnp.concatenate([v, zeros], axis=0)
                out = {}
                steps = _mlstm_head_steps(
                    q, k, v, b_all[:, h:h + 1], a_all[:, h:h + 1], a_t[h:h + 1, :],
                    m_ref[j:j + 1, h:h + 1], functools.partial(c_ref.__getitem__, (j, h)),
                    n_ref[j, h:h + 1, :], t_seq, out)
                items.append((j, h, out, steps))
    _round_robin(steps for _, _, _, steps in items)
    for j, h, out, _ in items:
        co_ref[j, h] = out["c"]
        no_ref[j, h:h + 1, :] = out["n"]
        mo_ref[j:j + 1, h:h + 1] = out["m"]
        h_ref[j * t_seq:(j + 1) * t_seq, h * DH:(h + 1) * DH] = out["h"][0:t_seq, :]


def _sample_mlstm(z, bi, bf, c0, n0, m0, t_seq):
    nb = c0.shape[1]
    sb = MLSTM_SEQS
    rows = sb * t_seq
    kern = functools.partial(_sample_mlstm_kernel, t_seq=t_seq)
    c_spec = pl.BlockSpec((None, sb, N_HEADS, DH, DH), lambda i: (0, i, 0, 0, 0))
    n_spec = pl.BlockSpec((None, sb, N_HEADS, DH), lambda i: (0, i, 0, 0))
    m_spec = pl.BlockSpec((None, sb, N_HEADS), lambda i: (0, i, 0))
    return pl.pallas_call(
        kern,
        grid=(nb // sb,),
        in_specs=[
            pl.BlockSpec((rows, 3 * D_MODEL), lambda i: (i, 0)),
            pl.BlockSpec((rows, 2 * LANES), lambda i: (i, IG0 // (2 * LANES))),
            _const_spec((1, LANES)), _const_spec((1, LANES)),
            c_spec, n_spec, m_spec,
        ],
        out_specs=[pl.BlockSpec((rows, D_MODEL), lambda i: (i, 0)), c_spec, n_spec, m_spec],
        out_shape=[
            jax.ShapeDtypeStruct((nb * t_seq, D_MODEL), F32),
            jax.ShapeDtypeStruct(c0.shape, F32),
            jax.ShapeDtypeStruct(n0.shape, F32),
            jax.ShapeDtypeStruct(m0.shape, F32),
        ],
        compiler_params=pltpu.CompilerParams(
            dimension_semantics=("arbitrary",), vmem_limit_bytes=VMEM_LIMIT),
        name="sample_mlstm",
    )(z, z, bi, bf, c0, n0, m0)


def _sample_mixer_tail_kernel(x_ref, z_ref, h_ref, cst_ref, hg_ref, wmo_ref, cw4_ref, cb_ref,
                              lng_ref, lnb_ref, wco_ref, wout_ref,
                              x1_ref, cnew_ref, full_s, uc_s, *, t_seq):
    R = x_ref.shape[0]
    n_seq = R // t_seq
    hist = CONV_K - 1
    ha = jnp.concatenate(
        [_head_out(h_ref[:, h * DH:(h + 1) * DH], hg_ref[:, h * DH:(h + 1) * DH],
                   z_ref[:, O0 + h * DH:O0 + (h + 1) * DH]) for h in range(N_HEADS)], axis=1)
    y_a = _mm(ha.astype(BF16), wmo_ref[...])

    uc_s[...] = z_ref[:, GV0:GV0 + D_MODEL] * _sigmoid(z_ref[:, GG0:GG0 + D_MODEL])
    full_s[...] = jnp.zeros_like(full_s)
    for j in range(n_seq):
        full_s[0:hist, :] = cst_ref[j]
        full_s[hist:hist + t_seq, :] = uc_s[j * t_seq:(j + 1) * t_seq, :]
        full = full_s[...]
        cnew_ref[j] = full_s[t_seq:t_seq + hist, :]
        for t in range(t_seq):
            uc_s[j * t_seq + t:j * t_seq + t + 1, :] = (
                jnp.sum(cw4_ref[t] * full, axis=0, keepdims=True) + cb_ref[...])

    ucn = _layernorm(uc_s[...], lng_ref[...], lnb_ref[...])
    y_b = _mm((ucn * _sigmoid(ucn)).astype(BF16), wco_ref[...])
    mix = _sigmoid(z_ref[:, GA0:GA0 + D_MODEL]) * y_a + _sigmoid(z_ref[:, GB0:GB0 + D_MODEL]) * y_b
    x1_ref[...] = x_ref[...] + _mm(mix.astype(BF16), wout_ref[...])


def _sample_mixer_tail(x, z, h, cst, hg, wmo, cw4, cb, lng, lnb, wco, wout, t_seq):
    R, D = x.shape
    rows = SAMPLE_ROWS
    sb = rows // t_seq
    hist = CONV_K - 1
    win = cw4.shape[1]
    kern = functools.partial(_sample_mixer_tail_kernel, t_seq=t_seq)
    st_spec = pl.BlockSpec((None, sb, hist, D), lambda i: (0, i, 0, 0))
    return pl.pallas_call(
        kern,
        grid=(R // rows,),
        in_specs=[
            pl.BlockSpec((rows, D), lambda i: (i, 0)),
            pl.BlockSpec((rows, N_Z), lambda i: (i, 0)),
            pl.BlockSpec((rows, D), lambda i: (i, 0)),
            st_spec,
            _const_spec((1, D)), _const_spec((D, D)), _const_spec((t_seq, win, D)), _const_spec((1, D)),
            _const_spec((1, D)), _const_spec((1, D)), _const_spec((D, D)), _const_spec((D, D)),
        ],
        out_specs=[pl.BlockSpec((rows, D), lambda i: (i, 0)), st_spec],
        out_shape=[
            jax.ShapeDtypeStruct((R, D), F32),
            jax.ShapeDtypeStruct(cst.shape, F32),
        ],
        scratch_shapes=[pltpu.VMEM((win, D), F32), pltpu.VMEM((rows, D), F32)],
        compiler_params=pltpu.CompilerParams(
            dimension_semantics=("arbitrary",), vmem_limit_bytes=VMEM_LIMIT),
        name="sample_mixer_tail",
    )(x, z, h, cst, hg, wmo, cw4, cb, lng, lnb, wco, wout)


def _sample_ffn_kernel(x_ref, fst_ref, fg_ref, wup_ref, fw_ref, fb_ref, wdown_ref, fing_ref,
                       y_ref, fnew_ref, up_s, seq_s, *, t_seq):
    R = x_ref.shape[0]
    n_seq = R // t_seq
    hist = FFN_K - 1
    x = x_ref[...]
    hf = _rmsnorm(x, fg_ref[...]).astype(BF16)
    for blk in range(2 * D_FF // DH):
        cs = slice(blk * DH, (blk + 1) * DH)
        up_s[:, cs] = _mm(hf, wup_ref[:, cs])
    for j in range(n_seq):
        rows = slice(j * t_seq, (j + 1) * t_seq)
        seq_s[0:hist, :] = fst_ref[j]
        seq_s[hist:hist + t_seq, :] = up_s[rows, :]
        fnew_ref[j] = seq_s[t_seq:t_seq + hist, :]
        acc = fb_ref[...] + fw_ref[0:1, :] * seq_s[0:t_seq, :]
        for kk in range(1, FFN_K):
            acc = acc + fw_ref[kk:kk + 1, :] * seq_s[kk:kk + t_seq, :]
        up_s[rows, :] = acc
    y_ref[...] = _ffn_tail(up_s[:, 0:D_FF], up_s[:, D_FF:2 * D_FF], wdown_ref, x, fing_ref)


def _sample_ffn(x1, fst, fg, wup, fw, fb, wdown, fing, t_seq):
    R, D = x1.shape
    rows = SAMPLE_ROWS
    sb = rows // t_seq
    hist = FFN_K - 1
    kern = functools.partial(_sample_ffn_kernel, t_seq=t_seq)
    st_spec = pl.BlockSpec((None, sb, hist, 2 * D_FF), lambda i: (0, i, 0, 0))
    return pl.pallas_call(
        kern,
        grid=(R // rows,),
        in_specs=[
            pl.BlockSpec((rows, D), lambda i: (i, 0)),
            st_spec,
            _const_spec((1, D)), _const_spec((D, 2 * D_FF)), _const_spec((FFN_K, 2 * D_FF)),
            _const_spec((1, 2 * D_FF)), _const_spec((D_FF, D)), _const_spec((1, D)),
        ],
        out_specs=[pl.BlockSpec((rows, D), lambda i: (i, 0)), st_spec],
        out_shape=[
            jax.ShapeDtypeStruct((R, D), F32),
            jax.ShapeDtypeStruct(fst.shape, F32),
        ],
        scratch_shapes=[pltpu.VMEM((rows, 2 * D_FF), F32), pltpu.VMEM((SUBLANES, 2 * D_FF), F32)],
        compiler_params=pltpu.CompilerParams(
            dimension_semantics=("arbitrary",), vmem_limit_bytes=VMEM_LIMIT),
        name="sample_ffn",
    )(x1, fst, fg, wup, fw, fb, wdown, fing)


def _pad_gate_bias(b):
    return jnp.pad(b, (0, LANES - N_HEADS)).reshape(1, LANES)


def _shifted_conv_weights(cw, t_seq):
    k = cw.shape[0]
    win = -(-(k - 1 + t_seq) // SUBLANES) * SUBLANES
    return jnp.stack([jnp.pad(cw, ((t, win - k - t), (0, 0))) for t in range(t_seq)])


def kernel(x_prompt, x_sample, state_C, state_n, state_m, state_conv, state_ffn, mix_norm_g, w_in, b_if,
           head_norm_g, w_mlstm_out, conv_w, conv_b, conv_ln_g, conv_ln_b, w_conv_out, w_out, ffn_norm_g,
           w_up, ffn_conv_w, ffn_conv_b, w_down, final_norm_g):
    depth = w_in.shape[0]
    assert depth == 1, "single-layer trunk"
    l = 0
    row = lambda a: a.reshape(1, -1)
    w_all = _pack_w_in(jnp.swapaxes(w_in, 1, 2))
    bi = _pad_gate_bias(b_if[l, :N_HEADS])
    bf = _pad_gate_bias(b_if[l, N_HEADS:])
    mixg, hg = row(mix_norm_g[l]), row(head_norm_g[l])
    wmo, wco, wout = (w.astype(BF16) for w in (w_mlstm_out[l], w_conv_out[l], w_out[l]))
    cw, cb = conv_w[l], row(conv_b[l])
    lng, lnb = row(conv_ln_g[l]), row(conv_ln_b[l])
    fg, fing = row(ffn_norm_g[l]), row(final_norm_g)
    wup, wdown = w_up[l].astype(BF16), w_down[l].astype(BF16)
    fw, fb = ffn_conv_w[l], row(ffn_conv_b[l])

    x1p, c_p, n_p, m_p, conv_p = _prompt_mixer(x_prompt, mixg, w_all, bi, bf, hg, wmo, cw, cb, lng, lnb, wco, wout)
    y_p, ffn_p = _prompt_ffn(x1p, fg, wup, fw, fb, wdown, fing)

    nb, t_seq, d = x_sample.shape
    xs = x_sample.reshape(nb * t_seq, d)
    z = _sample_proj(xs, mixg, w_all)
    h_s, c_s, n_s, m_s = _sample_mlstm(z, bi, bf, state_C, state_n, state_m, t_seq)
    cw4 = _shifted_conv_weights(cw, t_seq)
    x1s, conv_s = _sample_mixer_tail(xs, z, h_s, state_conv, hg, wmo, cw4, cb, lng, lnb, wco, wout, t_seq)
    y_s, ffn_s = _sample_ffn(x1s, state_ffn, fg, wup, fw, fb, wdown, fing, t_seq)

    return (y_p, y_s.reshape(nb, t_seq, d),
            c_p, n_p, m_p.reshape(1, -1, N_HEADS), conv_p, ffn_p,
            c_s, n_s, m_s, conv_s, ffn_s)
```

```python
import functools

import jax
import jax.numpy as jnp
from jax import lax
from jax.experimental import pallas as pl
from jax.experimental.pallas import tpu as pltpu

F32 = jnp.float32
BF16 = jnp.bfloat16

D_MODEL = 1024
N_HEADS = 4
DH = D_MODEL // N_HEADS
CONV_K = 31
D_FF = 2816
FFN_K = 3
EPS = 1e-6

LANES = 128
Q0, K0, V0, O0, GV0, GG0, GA0, GB0 = (i * D_MODEL for i in range(8))
IG0 = 8 * D_MODEL
FG0 = IG0 + LANES
N_Z = FG0 + LANES

SUBLANES = 8
CONV_LEAD = 32
PROMPT_CHUNK = 256
SAMPLE_ROWS = 128
MLSTM_SEQS = 8
PACK_ROWS = 128
VMEM_LIMIT = 58 * 1024 * 1024


def _mm(a, b):
    return jnp.dot(a, b, preferred_element_type=F32)


def _mm_nt(a, b):
    return lax.dot_general(a, b, (((1,), (1,)), ((), ())), preferred_element_type=F32)


def _mm_tn(a, b):
    return lax.dot_general(a, b, (((0,), (0,)), ((), ())), preferred_element_type=F32)


def _rmsnorm(x, g):
    return x * lax.rsqrt(jnp.mean(x * x, axis=-1, keepdims=True) + EPS) * g


def _layernorm(x, g, b):
    mu = jnp.mean(x, axis=-1, keepdims=True)
    xc = x - mu
    return xc * lax.rsqrt(jnp.mean(xc * xc, axis=-1, keepdims=True) + EPS) * g + b


def _sigmoid(x):
    return 0.5 * jnp.tanh(0.5 * x) + 0.5


def _log_sigmoid(x):
    return jnp.minimum(x, 0.0) - jnp.log1p(jnp.exp(-jnp.abs(x)))


def _cumsum_rows(x):
    n = x.shape[0]
    row = lax.broadcasted_iota(jnp.int32, x.shape, 0)
    s = 1
    while s < n:
        x = x + jnp.where(row >= s, pltpu.roll(x, s, axis=0), 0.0)
        s *= 2
    return x


def _mlstm_head_steps(q, k, v, b_col, a_col, a_row, m_prev, c_prev, n_prev, n_valid, out):
    L, LK = q.shape[0], k.shape[0]
    row = lax.broadcasted_iota(jnp.int32, (L, LK), 0)
    col = lax.broadcasted_iota(jnp.int32, (L, LK), 1)
    d = jnp.where(col <= row, b_col + a_row, -jnp.inf)
    inter = b_col + m_prev
    d_max = jnp.max(d, axis=1, keepdims=True)
    qb, kb, vb = q.astype(BF16), k.astype(BF16), v.astype(BF16)
    qk = _mm_nt(qb, kb)
    yield
    m_t = jnp.maximum(inter, d_max)
    w_inter = jnp.exp(inter - m_t)
    p = jnp.exp(d - m_t)
    s = qk * p
    qc = _mm_nt(qb, c_prev().astype(BF16))
    yield
    num = w_inter * qc + _mm(s.astype(BF16), vb)
    den = w_inter * jnp.sum(q * n_prev, axis=1, keepdims=True) + jnp.sum(s, axis=1, keepdims=True)
    m_new = m_t[n_valid - 1:n_valid, :]
    b_last = b_col[n_valid - 1:n_valid, :]
    g_inter = jnp.exp(b_last + m_prev - m_new)
    g_col = jnp.exp(b_last + a_col - m_new)
    if n_valid < LK:
        g_col = jnp.where(lax.broadcasted_iota(jnp.int32, (LK, 1), 0) < n_valid, g_col, 0.0)
    yield
    out["h"] = num * (1.0 / jnp.maximum(jnp.abs(den), jnp.exp(-m_t)))
    out["c"] = g_inter * c_prev() + _mm_tn((g_col * v).astype(BF16), kb)
    out["n"] = g_inter * n_prev + jnp.sum(g_col * k, axis=0, keepdims=True)
    out["m"] = m_new
    yield


def _mlstm_head(q, k, v, b_col, a_col, a_row, m_prev, c_prev, n_prev, n_valid):
    out = {}
    for _ in _mlstm_head_steps(q, k, v, b_col, a_col, a_row, m_prev, c_prev, n_prev, n_valid, out):
        pass
    return out["h"], out["c"], out["n"], out["m"]


def _round_robin(streams):
    streams = list(streams)
    while streams:
        alive = []
        for gen in streams:
            try:
                next(gen)
                alive.append(gen)
            except StopIteration:
                pass
        streams = alive


def _head_out(h, head_g, o):
    mu = jnp.mean(h, axis=-1, keepdims=True)
    hc = h - mu
    hn = hc * lax.rsqrt(jnp.mean(hc * hc, axis=-1, keepdims=True) + EPS)
    return hn * head_g * _sigmoid(o)


def _causal_conv_cols(full_v, cwb_s, cb_ref, out_v, cs, T):
    off = CONV_LEAD - (CONV_K - 1)
    n_a = (off + CONV_K - 1) // SUBLANES + 1
    width = cs.stop - cs.start
    sub = lax.broadcasted_iota(jnp.int32, (SUBLANES, width), 0)
    bias = cb_ref[:, cs]
    tiles = {}

    def tile(i):
        if i not in tiles:
            tiles[i] = full_v[SUBLANES * i:SUBLANES * (i + 1), :]
        return tiles[i]

    prev = None
    for i in range(T // SUBLANES + 1):
        ys = []
        for r in range(SUBLANES):
            acc = None
            for a in range(n_a):
                j = SUBLANES * a + r - off
                if 0 <= j < CONV_K:
                    term = cwb_s[j, :, cs] * tile(i + a)
                    acc = term if acc is None else acc + term
            ys.append(acc)
        tiles.pop(i, None)
        cur = [ys[0]] + [pltpu.roll(ys[r], SUBLANES - r, axis=0) for r in range(1, SUBLANES)]
        if prev is not None:
            out = prev[0] + bias
            for r in range(1, SUBLANES):
                out = out + jnp.where(sub < SUBLANES - r, prev[r], cur[r])
            out_v[SUBLANES * (i - 1):SUBLANES * i, :] = out
        prev = cur


def _const_spec(shape):
    nd = len(shape)
    return pl.BlockSpec(shape, lambda *_: (0,) * nd, pipeline_mode=pl.Buffered(1))


def _pack_kernel(w_ref, o_ref):
    split = 4 * D_MODEL
    n_gate = 2 * N_HEADS
    rows = w_ref.shape[1]
    for src, dst in ((0, 0), (split + n_gate, split)):
        for t in range(split // LANES):
            blk = w_ref[src + t * LANES:src + (t + 1) * LANES, :]
            o_ref[:, dst + t * LANES:dst + (t + 1) * LANES] = blk.T.astype(BF16)
    g = jnp.concatenate([w_ref[split:split + n_gate, :], jnp.zeros((LANES - n_gate, rows), F32)], axis=0).T
    lane = lax.broadcasted_iota(jnp.int32, g.shape, 1)
    o_ref[:, IG0:IG0 + LANES] = jnp.where(lane < N_HEADS, g, 0.0).astype(BF16)
    o_ref[:, FG0:FG0 + LANES] = jnp.where(
        lane < N_HEADS, pltpu.roll(g, LANES - N_HEADS, axis=1), 0.0).astype(BF16)


def _pack_w_in(w_t):
    _, n_in, d = w_t.shape
    return pl.pallas_call(
        _pack_kernel,
        grid=(d // PACK_ROWS,),
        in_specs=[pl.BlockSpec((None, n_in, PACK_ROWS), lambda i: (0, 0, i))],
        out_specs=pl.BlockSpec((PACK_ROWS, N_Z), lambda i: (i, 0)),
        out_shape=jax.ShapeDtypeStruct((d, N_Z), BF16),
        compiler_params=pltpu.CompilerParams(
            dimension_semantics=("arbitrary",), vmem_limit_bytes=VMEM_LIMIT),
        name="pack_w_in",
    )(w_t)


def _prompt_mixer_kernel(x_ref, mixg_ref, w_ref, bi_ref, bf_ref, hg_ref, wmo_ref, cw_ref, cb_ref,
                         lng_ref, lnb_ref, wco_ref, wout_ref,
                         x1_ref, c_ref, n_ref, m_ref, conv_ref,
                         xn_s, ha_s, ga_s, gb_s, full_s, uc_s, m_s, cwb_s, *, n_chunks):
    T = x_ref.shape[1]
    c = pl.program_id(1)

    @pl.when(c == 0)
    def _():
        c_ref[...] = jnp.zeros_like(c_ref)
        n_ref[...] = jnp.zeros_like(n_ref)
        m_s[...] = jnp.zeros_like(m_s)
        full_s[0:CONV_LEAD, :] = jnp.zeros((CONV_LEAD, D_MODEL), F32)
        full_s[CONV_LEAD + T:CONV_LEAD + T + SUBLANES, :] = jnp.zeros((SUBLANES, D_MODEL), F32)
        for j in range(CONV_K):
            cwb_s[j] = jnp.broadcast_to(cw_ref[j:j + 1, :], (SUBLANES, D_MODEL))

    x = x_ref[0]
    xn_s[...] = _rmsnorm(x, mixg_ref[...]).astype(BF16)
    xn = xn_s[...]

    zi = _mm(xn, w_ref[:, IG0:IG0 + LANES]) + bi_ref[...]
    lf = _log_sigmoid(_mm(xn, w_ref[:, FG0:FG0 + LANES]) + bf_ref[...])
    b_all = _cumsum_rows(lf)
    a_all = zi - b_all
    a_t = a_all.T

    for h in range(N_HEADS):
        cs = slice(h * DH, (h + 1) * DH)
        gv = _mm(xn, w_ref[:, GV0 + h * DH:GV0 + (h + 1) * DH])
        gg = _mm(xn, w_ref[:, GG0 + h * DH:GG0 + (h + 1) * DH])
        full_s[CONV_LEAD:CONV_LEAD + T, cs] = gv * _sigmoid(gg)
        q = _mm(xn, w_ref[:, Q0 + h * DH:Q0 + (h + 1) * DH])
        k = _mm(xn, w_ref[:, K0 + h * DH:K0 + (h + 1) * DH]) * (DH ** -0.5)
        v = _mm(xn, w_ref[:, V0 + h * DH:V0 + (h + 1) * DH])
        _causal_conv_cols(full_s.at[:, cs], cwb_s, cb_ref, uc_s.at[:, cs], cs, T)
        hh, c_new, n_new, m_new = _mlstm_head(
            q, k, v, b_all[:, h:h + 1], a_all[:, h:h + 1], a_t[h:h + 1, :],
            m_s[h:h + 1, 0:1], functools.partial(c_ref.__getitem__, (0, h)), n_ref[0, h:h + 1, :], T)
        c_ref[0, h] = c_new
        n_ref[0, h:h + 1, :] = n_new
        m_s[h:h + 1, :] = jnp.broadcast_to(m_new, (1, LANES))
        o = _mm(xn, w_ref[:, O0 + h * DH:O0 + (h + 1) * DH])
        ha_s[:, cs] = _head_out(hh, hg_ref[:, cs], o).astype(BF16)
        ga_s[:, cs] = _mm(xn, w_ref[:, GA0 + h * DH:GA0 + (h + 1) * DH])
        gb_s[:, cs] = _mm(xn, w_ref[:, GB0 + h * DH:GB0 + (h + 1) * DH])

    y_a = _mm(ha_s[...], wmo_ref[...])
    tail = full_s[T + CONV_LEAD - (CONV_K - 1):T + CONV_LEAD, :]
    conv_ref[0] = tail
    full_s[CONV_LEAD - (CONV_K - 1):CONV_LEAD, :] = tail

    ucn = _layernorm(uc_s[...], lng_ref[...], lnb_ref[...])
    y_b = _mm((ucn * _sigmoid(ucn)).astype(BF16), wco_ref[...])

    mix = _sigmoid(ga_s[...]) * y_a + _sigmoid(gb_s[...]) * y_b
    x1_ref[0] = x + _mm(mix.astype(BF16), wout_ref[...])

    @pl.when(c == n_chunks - 1)
    def _():
        for h in range(N_HEADS):
            m_ref[0, :, h:h + 1] = m_s[h:h + 1, 0:1]


def _prompt_mixer(x, mixg, w_all, bi, bf, hg, wmo, cw, cb, lng, lnb, wco, wout):
    B, S, D = x.shape
    T = PROMPT_CHUNK
    nc = S // T
    kern = functools.partial(_prompt_mixer_kernel, n_chunks=nc)
    return pl.pallas_call(
        kern,
        grid=(B, nc),
        in_specs=[
            pl.BlockSpec((1, T, D), lambda b, c: (b, c, 0)),
            _const_spec((1, D)), _const_spec((D, N_Z)), _const_spec((1, LANES)), _const_spec((1, LANES)),
            _const_spec((1, D)), _const_spec((D, D)), _const_spec((CONV_K, D)), _const_spec((1, D)),
            _const_spec((1, D)), _const_spec((1, D)), _const_spec((D, D)), _const_spec((D, D)),
        ],
        out_specs=[
            pl.BlockSpec((1, T, D), lambda b, c: (b, c, 0)),
            pl.BlockSpec((None, 1, N_HEADS, DH, DH), lambda b, c: (0, b, 0, 0, 0)),
            pl.BlockSpec((None, 1, N_HEADS, DH), lambda b, c: (0, b, 0, 0)),
            pl.BlockSpec((1, 1, N_HEADS), lambda b, c: (b, 0, 0)),
            pl.BlockSpec((None, 1, CONV_K - 1, D), lambda b, c: (0, b, 0, 0)),
        ],
        out_shape=[
            jax.ShapeDtypeStruct((B, S, D), F32),
            jax.ShapeDtypeStruct((1, B, N_HEADS, DH, DH), F32),
            jax.ShapeDtypeStruct((1, B, N_HEADS, DH), F32),
            jax.ShapeDtypeStruct((B, 1, N_HEADS), F32),
            jax.ShapeDtypeStruct((1, B, CONV_K - 1, D), F32),
        ],
        scratch_shapes=[
            pltpu.VMEM((T, D), BF16), pltpu.VMEM((T, D), BF16),
            pltpu.VMEM((T, D), F32), pltpu.VMEM((T, D), F32),
            pltpu.VMEM((T + CONV_LEAD + SUBLANES, D), F32), pltpu.VMEM((T, D), F32),
            pltpu.VMEM((SUBLANES, LANES), F32), pltpu.VMEM((CONV_K, SUBLANES, D), F32),
        ],
        compiler_params=pltpu.CompilerParams(
            dimension_semantics=("arbitrary", "arbitrary"), vmem_limit_bytes=VMEM_LIMIT),
        name="prompt_mixer",
    )(x, mixg, w_all, bi, bf, hg, wmo, cw, cb, lng, lnb, wco, wout)


def _ffn_tail(upc_a, upc_g, wdown_ref, x, fing_ref):
    act = (upc_a * _sigmoid(upc_a) * upc_g).astype(BF16)
    x2 = x + _mm(act, wdown_ref[...])
    return _rmsnorm(x2, fing_ref[...])


def _prompt_ffn_kernel(x_ref, fg_ref, wup_ref, fw_ref, fb_ref, wdown_ref, fing_ref,
                       y_ref, ffn_ref, up_s, act_s):
    T = x_ref.shape[1]
    c = pl.program_id(1)

    @pl.when(c == 0)
    def _():
        up_s[0:8, :] = jnp.zeros((8, 2 * D_FF), F32)

    x = x_ref[0]
    hf = _rmsnorm(x, fg_ref[...]).astype(BF16)
    nb = 2 * D_FF // DH
    for blk in range(nb):
        cs = slice(blk * DH, (blk + 1) * DH)
        up_s[8:8 + T, cs] = _mm(hf, wup_ref[:, cs])
    for blk in range(D_FF // DH):
        ca = slice(blk * DH, (blk + 1) * DH)
        cg = slice(D_FF + blk * DH, D_FF + (blk + 1) * DH)
        a = fb_ref[:, ca] + fw_ref[0:1, ca] * up_s[6:6 + T, ca] + fw_ref[1:2, ca] * up_s[7:7 + T, ca] \
            + fw_ref[2:3, ca] * up_s[8:8 + T, ca]
        g = fb_ref[:, cg] + fw_ref[0:1, cg] * up_s[6:6 + T, cg] + fw_ref[1:2, cg] * up_s[7:7 + T, cg] \
            + fw_ref[2:3, cg] * up_s[8:8 + T, cg]
        act_s[:, ca] = (a * _sigmoid(a) * g).astype(BF16)
    tail = up_s[T + 6:T + 8, :]
    ffn_ref[0] = tail
    up_s[6:8, :] = tail
    x2 = x + _mm(act_s[...], wdown_ref[...])
    y_ref[0] = _rmsnorm(x2, fing_ref[...])


def _prompt_ffn(x1, fg, wup, fw, fb, wdown, fing):
    B, S, D = x1.shape
    T = PROMPT_CHUNK
    nc = S // T
    return pl.pallas_call(
        _prompt_ffn_kernel,
        grid=(B, nc),
        in_specs=[
            pl.BlockSpec((1, T, D), lambda b, c: (b, c, 0)),
            _const_spec((1, D)), _const_spec((D, 2 * D_FF)), _const_spec((FFN_K, 2 * D_FF)),
            _const_spec((1, 2 * D_FF)), _const_spec((D_FF, D)), _const_spec((1, D)),
        ],
        out_specs=[
            pl.BlockSpec((1, T, D), lambda b, c: (b, c, 0)),
            pl.BlockSpec((None, 1, FFN_K - 1, 2 * D_FF), lambda b, c: (0, b, 0, 0)),
        ],
        out_shape=[
            jax.ShapeDtypeStruct((B, S, D), F32),
            jax.ShapeDtypeStruct((1, B, FFN_K - 1, 2 * D_FF), F32),
        ],
        scratch_shapes=[pltpu.VMEM((T + 8, 2 * D_FF), F32), pltpu.VMEM((T, D_FF), BF16)],
        compiler_params=pltpu.CompilerParams(
            dimension_semantics=("arbitrary", "arbitrary"), vmem_limit_bytes=VMEM_LIMIT),
        name="prompt_ffn",
    )(x1, fg, wup, fw, fb, wdown, fing)


def _sample_proj_kernel(x_ref, mixg_ref, w_ref, z_ref):
    xn = _rmsnorm(x_ref[...], mixg_ref[...]).astype(BF16)
    for blk in range(N_Z // DH):
        cs = slice(blk * DH, (blk + 1) * DH)
        z_ref[:, cs] = _mm(xn, w_ref[:, cs])


def _sample_proj(x, mixg, w_all):
    R, D = x.shape
    return pl.pallas_call(
        _sample_proj_kernel,
        grid=(R // SAMPLE_ROWS,),
        in_specs=[pl.BlockSpec((SAMPLE_ROWS, D), lambda i: (i, 0)),
                  _const_spec((1, D)), _const_spec((D, N_Z))],
        out_specs=pl.BlockSpec((SAMPLE_ROWS, N_Z), lambda i: (i, 0)),
        out_shape=jax.ShapeDtypeStruct((R, N_Z), F32),
        compiler_params=pltpu.CompilerParams(
            dimension_semantics=("arbitrary",), vmem_limit_bytes=VMEM_LIMIT),
        name="sample_proj",
    )(x, mixg, w_all)


def _sample_mlstm_kernel(zq_ref, zg_ref, bi_ref, bf_ref, c_ref, n_ref, m_ref,
                         h_ref, co_ref, no_ref, mo_ref, *, t_seq):
    per_win = SUBLANES // t_seq
    pad = LANES - SUBLANES
    items = []
    for win in range(MLSTM_SEQS // per_win):
        zq_w = zq_ref[SUBLANES * win:SUBLANES * (win + 1), :]
        zg_w = zg_ref[SUBLANES * win:SUBLANES * (win + 1), :]
        for sub in range(per_win):
            j = win * per_win + sub
            if sub:
                zq = pltpu.roll(zq_w, SUBLANES - sub * t_seq, axis=0)
                zg = pltpu.roll(zg_w, SUBLANES - sub * t_seq, axis=0)
            else:
                zq, zg = zq_w, zg_w
            zi = zg[:, 0:LANES] + bi_ref[...]
            lf = _log_sigmoid(zg[:, LANES:2 * LANES] + bf_ref[...])
            b_all = _cumsum_rows(lf)
            a_all = jnp.concatenate([zi - b_all, jnp.zeros((pad, LANES), F32)], axis=0)
            a_t = a_all.T
            for h in range(N_HEADS):
                q = zq[:, Q0 + h * DH:Q0 + (h + 1) * DH]
                k = zq[:, K0 + h * DH:K0 + (h + 1) * DH] * (DH ** -0.5)
                v = zq[:, V0 + h * DH:V0 + (h + 1) * DH]
                zeros = jnp.zeros((pad, DH), F32)
                k = jnp.concatenate([k, zeros], axis=0)
                v = jnp.concatenate([v, zeros], axis=0)
                out = {}
                steps = _mlstm_head_steps(
                    q, k, v, b_all[:, h:h + 1], a_all[:, h:h + 1], a_t[h:h + 1, :],
                    m_ref[j:j + 1, h:h + 1], functools.partial(c_ref.__getitem__, (j, h)),
                    n_ref[j, h:h + 1, :], t_seq, out)
                items.append((j, h, out, steps))
    _round_robin(steps for _, _, _, steps in items)
    for j, h, out, _ in items:
        co_ref[j, h] = out["c"]
        no_ref[j, h:h + 1, :] = out["n"]
        mo_ref[j:j + 1, h:h + 1] = out["m"]
        h_ref[j * t_seq:(j + 1) * t_seq, h * DH:(h + 1) * DH] = out["h"][0:t_seq, :]


def _sample_mlstm(z, bi, bf, c0, n0, m0, t_seq):
    nb = c0.shape[1]
    sb = MLSTM_SEQS
    rows = sb * t_seq
    kern = functools.partial(_sample_mlstm_kernel, t_seq=t_seq)
    c_spec = pl.BlockSpec((None, sb, N_HEADS, DH, DH), lambda i: (0, i, 0, 0, 0))
    n_spec = pl.BlockSpec((None, sb, N_HEADS, DH), lambda i: (0, i, 0, 0))
    m_spec = pl.BlockSpec((None, sb, N_HEADS), lambda i: (0, i, 0))
    return pl.pallas_call(
        kern,
        grid=(nb // sb,),
        in_specs=[
            pl.BlockSpec((rows, 3 * D_MODEL), lambda i: (i, 0)),
            pl.BlockSpec((rows, 2 * LANES), lambda i: (i, IG0 // (2 * LANES))),
            _const_spec((1, LANES)), _const_spec((1, LANES)),
            c_spec, n_spec, m_spec,
        ],
        out_specs=[pl.BlockSpec((rows, D_MODEL), lambda i: (i, 0)), c_spec, n_spec, m_spec],
        out_shape=[
            jax.ShapeDtypeStruct((nb * t_seq, D_MODEL), F32),
            jax.ShapeDtypeStruct(c0.shape, F32),
            jax.ShapeDtypeStruct(n0.shape, F32),
            jax.ShapeDtypeStruct(m0.shape, F32),
        ],
        compiler_params=pltpu.CompilerParams(
            dimension_semantics=("arbitrary",), vmem_limit_bytes=VMEM_LIMIT),
        name="sample_mlstm",
    )(z, z, bi, bf, c0, n0, m0)


def _sample_mixer_tail_kernel(x_ref, z_ref, h_ref, cst_ref, hg_ref, wmo_ref, cw4_ref, cb_ref,
                              lng_ref, lnb_ref, wco_ref, wout_ref,
                              x1_ref, cnew_ref, full_s, uc_s, *, t_seq):
    R = x_ref.shape[0]
    n_seq = R // t_seq
    hist = CONV_K - 1
    ha = jnp.concatenate(
        [_head_out(h_ref[:, h * DH:(h + 1) * DH], hg_ref[:, h * DH:(h + 1) * DH],
                   z_ref[:, O0 + h * DH:O0 + (h + 1) * DH]) for h in range(N_HEADS)], axis=1)
    y_a = _mm(ha.astype(BF16), wmo_ref[...])

    uc_s[...] = z_ref[:, GV0:GV0 + D_MODEL] * _sigmoid(z_ref[:, GG0:GG0 + D_MODEL])
    full_s[...] = jnp.zeros_like(full_s)
    for j in range(n_seq):
        full_s[0:hist, :] = cst_ref[j]
        full_s[hist:hist + t_seq, :] = uc_s[j * t_seq:(j + 1) * t_seq, :]
        full = full_s[...]
        cnew_ref[j] = full_s[t_seq:t_seq + hist, :]
        for t in range(t_seq):
            uc_s[j * t_seq + t:j * t_seq + t + 1, :] = (
                jnp.sum(cw4_ref[t] * full, axis=0, keepdims=True) + cb_ref[...])

    ucn = _layernorm(uc_s[...], lng_ref[...], lnb_ref[...])
    y_b = _mm((ucn * _sigmoid(ucn)).astype(BF16), wco_ref[...])
    mix = _sigmoid(z_ref[:, GA0:GA0 + D_MODEL]) * y_a + _sigmoid(z_ref[:, GB0:GB0 + D_MODEL]) * y_b
    x1_ref[...] = x_ref[...] + _mm(mix.astype(BF16), wout_ref[...])


def _sample_mixer_tail(x, z, h, cst, hg, wmo, cw4, cb, lng, lnb, wco, wout, t_seq):
    R, D = x.shape
    rows = SAMPLE_ROWS
    sb = rows // t_seq
    hist = CONV_K - 1
    win = cw4.shape[1]
    kern = functools.partial(_sample_mixer_tail_kernel, t_seq=t_seq)
    st_spec = pl.BlockSpec((None, sb, hist, D), lambda i: (0, i, 0, 0))
    return pl.pallas_call(
        kern,
        grid=(R // rows,),
        in_specs=[
            pl.BlockSpec((rows, D), lambda i: (i, 0)),
            pl.BlockSpec((rows, N_Z), lambda i: (i, 0)),
            pl.BlockSpec((rows, D), lambda i: (i, 0)),
            st_spec,
            _const_spec((1, D)), _const_spec((D, D)), _const_spec((t_seq, win, D)), _const_spec((1, D)),
            _const_spec((1, D)), _const_spec((1, D)), _const_spec((D, D)), _const_spec((D, D)),
        ],
        out_specs=[pl.BlockSpec((rows, D), lambda i: (i, 0)), st_spec],
        out_shape=[
            jax.ShapeDtypeStruct((R, D), F32),
            jax.ShapeDtypeStruct(cst.shape, F32),
        ],
        scratch_shapes=[pltpu.VMEM((win, D), F32), pltpu.VMEM((rows, D), F32)],
        compiler_params=pltpu.CompilerParams(
            dimension_semantics=("arbitrary",), vmem_limit_bytes=VMEM_LIMIT),
        name="sample_mixer_tail",
    )(x, z, h, cst, hg, wmo, cw4, cb, lng, lnb, wco, wout)


def _sample_ffn_kernel(x_ref, fst_ref, fg_ref, wup_ref, fw_ref, fb_ref, wdown_ref, fing_ref,
                       y_ref, fnew_ref, up_s, seq_s, *, t_seq):
    R = x_ref.shape[0]
    n_seq = R // t_seq
    hist = FFN_K - 1
    x = x_ref[...]
    hf = _rmsnorm(x, fg_ref[...]).astype(BF16)
    for blk in range(2 * D_FF // DH):
        cs = slice(blk * DH, (blk + 1) * DH)
        up_s[:, cs] = _mm(hf, wup_ref[:, cs])
    for j in range(n_seq):
        rows = slice(j * t_seq, (j + 1) * t_seq)
        seq_s[0:hist, :] = fst_ref[j]
        seq_s[hist:hist + t_seq, :] = up_s[rows, :]
        fnew_ref[j] = seq_s[t_seq:t_seq + hist, :]
        acc = fb_ref[...] + fw_ref[0:1, :] * seq_s[0:t_seq, :]
        for kk in range(1, FFN_K):
            acc = acc + fw_ref[kk:kk + 1, :] * seq_s[kk:kk + t_seq, :]
        up_s[rows, :] = acc
    y_ref[...] = _ffn_tail(up_s[:, 0:D_FF], up_s[:, D_FF:2 * D_FF], wdown_ref, x, fing_ref)


def _sample_ffn(x1, fst, fg, wup, fw, fb, wdown, fing, t_seq):
    R, D = x1.shape
    rows = SAMPLE_ROWS
    sb = rows // t_seq
    hist = FFN_K - 1
    kern = functools.partial(_sample_ffn_kernel, t_seq=t_seq)
    st_spec = pl.BlockSpec((None, sb, hist, 2 * D_FF), lambda i: (0, i, 0, 0))
    return pl.pallas_call(
        kern,
        grid=(R // rows,),
        in_specs=[
            pl.BlockSpec((rows, D), lambda i: (i, 0)),
            st_spec,
            _const_spec((1, D)), _const_spec((D, 2 * D_FF)), _const_spec((FFN_K, 2 * D_FF)),
            _const_spec((1, 2 * D_FF)), _const_spec((D_FF, D)), _const_spec((1, D)),
        ],
        out_specs=[pl.BlockSpec((rows, D), lambda i: (i, 0)), st_spec],
        out_shape=[
            jax.ShapeDtypeStruct((R, D), F32),
            jax.ShapeDtypeStruct(fst.shape, F32),
        ],
        scratch_shapes=[pltpu.VMEM((rows, 2 * D_FF), F32), pltpu.VMEM((SUBLANES, 2 * D_FF), F32)],
        compiler_params=pltpu.CompilerParams(
            dimension_semantics=("arbitrary",), vmem_limit_bytes=VMEM_LIMIT),
        name="sample_ffn",
    )(x1, fst, fg, wup, fw, fb, wdown, fing)


def _pad_gate_bias(b):
    return jnp.pad(b, (0, LANES - N_HEADS)).reshape(1, LANES)


def _shifted_conv_weights(cw, t_seq):
    k = cw.shape[0]
    win = -(-(k - 1 + t_seq) // SUBLANES) * SUBLANES
    return jnp.stack([jnp.pad(cw, ((t, win - k - t), (0, 0))) for t in range(t_seq)])


def kernel(x_prompt, x_sample, state_C, state_n, state_m, state_conv, state_ffn, mix_norm_g, w_in, b_if,
           head_norm_g, w_mlstm_out, conv_w, conv_b, conv_ln_g, conv_ln_b, w_conv_out, w_out, ffn_norm_g,
           w_up, ffn_conv_w, ffn_conv_b, w_down, final_norm_g):
    depth = w_in.shape[0]
    assert depth == 1, "single-layer trunk"
    l = 0
    row = lambda a: a.reshape(1, -1)
    w_all = _pack_w_in(jnp.swapaxes(w_in, 1, 2))
    bi = _pad_gate_bias(b_if[l, :N_HEADS])
    bf = _pad_gate_bias(b_if[l, N_HEADS:])
    mixg, hg = row(mix_norm_g[l]), row(head_norm_g[l])
    wmo, wco, wout = (w.astype(BF16) for w in (w_mlstm_out[l], w_conv_out[l], w_out[l]))
    cw, cb = conv_w[l], row(conv_b[l])
    lng, lnb = row(conv_ln_g[l]), row(conv_ln_b[l])
    fg, fing = row(ffn_norm_g[l]), row(final_norm_g)
    wup, wdown = w_up[l].astype(BF16), w_down[l].astype(BF16)
    fw, fb = ffn_conv_w[l], row(ffn_conv_b[l])

    x1p, c_p, n_p, m_p, conv_p = _prompt_mixer(x_prompt, mixg, w_all, bi, bf, hg, wmo, cw, cb, lng, lnb, wco, wout)
    y_p, ffn_p = _prompt_ffn(x1p, fg, wup, fw, fb, wdown, fing)

    nb, t_seq, d = x_sample.shape
    xs = x_sample.reshape(nb * t_seq, d)
    z = _sample_proj(xs, mixg, w_all)
    h_s, c_s, n_s, m_s = _sample_mlstm(z, bi, bf, state_C, state_n, state_m, t_seq)
    cw4 = _shifted_conv_weights(cw, t_seq)
    x1s, conv_s = _sample_mixer_tail(xs, z, h_s, state_conv, hg, wmo, cw4, cb, lng, lnb, wco, wout, t_seq)
    y_s, ffn_s = _sample_ffn(x1s, state_ffn, fg, wup, fw, fb, wdown, fing, t_seq)

    return (y_p, y_s.reshape(nb, t_seq, d),
            c_p, n_p, m_p.reshape(1, -1, N_HEADS), conv_p, ffn_p,
            c_s, n_s, m_s, conv_s, ffn_s)
```

```python
import functools

import jax
import jax.numpy as jnp
from jax import lax
from jax.experimental import pallas as pl
from jax.experimental.pallas import tpu as pltpu

F32 = jnp.float32
BF16 = jnp.bfloat16

D_MODEL = 1024
N_HEADS = 4
DH = D_MODEL // N_HEADS
CONV_K = 31
D_FF = 2816
FFN_K = 3
EPS = 1e-6

LANES = 128
Q0, K0, V0, O0, GV0, GG0, GA0, GB0 = (i * D_MODEL for i in range(8))
IG0 = 8 * D_MODEL
FG0 = IG0 + LANES
N_Z = FG0 + LANES

SUBLANES = 8
CONV_LEAD = 32
PROMPT_CHUNK = 256
SAMPLE_ROWS = 128
MLSTM_SEQS = 8
PACK_ROWS = 128
VMEM_LIMIT = 58 * 1024 * 1024


def _mm(a, b):
    return jnp.dot(a, b, preferred_element_type=F32)


def _mm_nt(a, b):
    return lax.dot_general(a, b, (((1,), (1,)), ((), ())), preferred_element_type=F32)


def _mm_tn(a, b):
    return lax.dot_general(a, b, (((0,), (0,)), ((), ())), preferred_element_type=F32)


def _rmsnorm(x, g):
    return x * lax.rsqrt(jnp.mean(x * x, axis=-1, keepdims=True) + EPS) * g


def _layernorm(x, g, b):
    mu = jnp.mean(x, axis=-1, keepdims=True)
    xc = x - mu
    return xc * lax.rsqrt(jnp.mean(xc * xc, axis=-1, keepdims=True) + EPS) * g + b


def _sigmoid(x):
    return 0.5 * jnp.tanh(0.5 * x) + 0.5


def _log_sigmoid(x):
    return jnp.minimum(x, 0.0) - jnp.log1p(jnp.exp(-jnp.abs(x)))


def _cumsum_rows(x):
    n = x.shape[0]
    row = lax.broadcasted_iota(jnp.int32, x.shape, 0)
    s = 1
    while s < n:
        x = x + jnp.where(row >= s, pltpu.roll(x, s, axis=0), 0.0)
        s *= 2
    return x


def _mlstm_head_steps(q, k, v, b_col, a_col, a_row, m_prev, c_prev, n_prev, n_valid, out):
    L, LK = q.shape[0], k.shape[0]
    row = lax.broadcasted_iota(jnp.int32, (L, LK), 0)
    col = lax.broadcasted_iota(jnp.int32, (L, LK), 1)
    d = jnp.where(col <= row, b_col + a_row, -jnp.inf)
    inter = b_col + m_prev
    d_max = jnp.max(d, axis=1, keepdims=True)
    qb, kb, vb = q.astype(BF16), k.astype(BF16), v.astype(BF16)
    qk = _mm_nt(qb, kb)
    yield
    m_t = jnp.maximum(inter, d_max)
    w_inter = jnp.exp(inter - m_t)
    p = jnp.exp(d - m_t)
    s = qk * p
    qc = _mm_nt(qb, c_prev().astype(BF16))
    yield
    num = w_inter * qc + _mm(s.astype(BF16), vb)
    den = w_inter * jnp.sum(q * n_prev, axis=1, keepdims=True) + jnp.sum(s, axis=1, keepdims=True)
    m_new = m_t[n_valid - 1:n_valid, :]
    b_last = b_col[n_valid - 1:n_valid, :]
    g_inter = jnp.exp(b_last + m_prev - m_new)
    g_col = jnp.exp(b_last + a_col - m_new)
    if n_valid < LK:
        g_col = jnp.where(lax.broadcasted_iota(jnp.int32, (LK, 1), 0) < n_valid, g_col, 0.0)
    yield
    out["h"] = num * (1.0 / jnp.maximum(jnp.abs(den), jnp.exp(-m_t)))
    out["c"] = g_inter * c_prev() + _mm_tn((g_col * v).astype(BF16), kb)
    out["n"] = g_inter * n_prev + jnp.sum(g_col * k, axis=0, keepdims=True)
    out["m"] = m_new
    yield


def _mlstm_head(q, k, v, b_col, a_col, a_row, m_prev, c_prev, n_prev, n_valid):
    out = {}
    for _ in _mlstm_head_steps(q, k, v, b_col, a_col, a_row, m_prev, c_prev, n_prev, n_valid, out):
        pass
    return out["h"], out["c"], out["n"], out["m"]


def _round_robin(streams):
    streams = list(streams)
    while streams:
        alive = []
        for gen in streams:
            try:
                next(gen)
                alive.append(gen)
            except StopIteration:
                pass
        streams = alive


def _head_out(h, head_g, o):
    mu = jnp.mean(h, axis=-1, keepdims=True)
    hc = h - mu
    hn = hc * lax.rsqrt(jnp.mean(hc * hc, axis=-1, keepdims=True) + EPS)
    return hn * head_g * _sigmoid(o)


def _causal_conv_cols(full_v, cwb_s, cb_ref, out_v, cs, T):
    off = CONV_LEAD - (CONV_K - 1)
    n_a = (off + CONV_K - 1) // SUBLANES + 1
    width = cs.stop - cs.start
    sub = lax.broadcasted_iota(jnp.int32, (SUBLANES, width), 0)
    bias = cb_ref[:, cs]
    tiles = {}

    def tile(i):
        if i not in tiles:
            tiles[i] = full_v[SUBLANES * i:SUBLANES * (i + 1), :]
        return tiles[i]

    prev = None
    for i in range(T // SUBLANES + 1):
        ys = []
        for r in range(SUBLANES):
            acc = None
            for a in range(n_a):
                j = SUBLANES * a + r - off
                if 0 <= j < CONV_K:
                    term = cwb_s[j, :, cs] * tile(i + a)
                    acc = term if acc is None else acc + term
            ys.append(acc)
        tiles.pop(i, None)
        cur = [ys[0]] + [pltpu.roll(ys[r], SUBLANES - r, axis=0) for r in range(1, SUBLANES)]
        if prev is not None:
            out = prev[0] + bias
            for r in range(1, SUBLANES):
                out = out + jnp.where(sub < SUBLANES - r, prev[r], cur[r])
            out_v[SUBLANES * (i - 1):SUBLANES * i, :] = out
        prev = cur


def _const_spec(shape):
    nd = len(shape)
    return pl.BlockSpec(shape, lambda *_: (0,) * nd, pipeline_mode=pl.Buffered(1))


def _pack_kernel(w_ref, o_ref):
    split = 4 * D_MODEL
    n_gate = 2 * N_HEADS
    rows = w_ref.shape[1]
    for src, dst in ((0, 0), (split + n_gate, split)):
        for t in range(split // LANES):
            blk = w_ref[src + t * LANES:src + (t + 1) * LANES, :]
            o_ref[:, dst + t * LANES:dst + (t + 1) * LANES] = blk.T.astype(BF16)
    g = jnp.concatenate([w_ref[split:split + n_gate, :], jnp.zeros((LANES - n_gate, rows), F32)], axis=0).T
    lane = lax.broadcasted_iota(jnp.int32, g.shape, 1)
    o_ref[:, IG0:IG0 + LANES] = jnp.where(lane < N_HEADS, g, 0.0).astype(BF16)
    o_ref[:, FG0:FG0 + LANES] = jnp.where(
        lane < N_HEADS, pltpu.roll(g, LANES - N_HEADS, axis=1), 0.0).astype(BF16)


def _pack_w_in(w_t):
    _, n_in, d = w_t.shape
    return pl.pallas_call(
        _pack_kernel,
        grid=(d // PACK_ROWS,),
        in_specs=[pl.BlockSpec((None, n_in, PACK_ROWS), lambda i: (0, 0, i))],
        out_specs=pl.BlockSpec((PACK_ROWS, N_Z), lambda i: (i, 0)),
        out_shape=jax.ShapeDtypeStruct((d, N_Z), BF16),
        compiler_params=pltpu.CompilerParams(
            dimension_semantics=("arbitrary",), vmem_limit_bytes=VMEM_LIMIT),
        name="pack_w_in",
    )(w_t)


def _prompt_mixer_kernel(x_ref, mixg_ref, w_ref, bi_ref, bf_ref, hg_ref, wmo_ref, cw_ref, cb_ref,
                         lng_ref, lnb_ref, wco_ref, wout_ref,
                         x1_ref, c_ref, n_ref, m_ref, conv_ref,
                         xn_s, ha_s, ga_s, gb_s, full_s, uc_s, m_s, cwb_s, *, n_chunks):
    T = x_ref.shape[1]
    c = pl.program_id(1)

    @pl.when(c == 0)
    def _():
        c_ref[...] = jnp.zeros_like(c_ref)
        n_ref[...] = jnp.zeros_like(n_ref)
        m_s[...] = jnp.zeros_like(m_s)
        full_s[0:CONV_LEAD, :] = jnp.zeros((CONV_LEAD, D_MODEL), F32)
        full_s[CONV_LEAD + T:CONV_LEAD + T + SUBLANES, :] = jnp.zeros((SUBLANES, D_MODEL), F32)
        for j in range(CONV_K):
            cwb_s[j] = jnp.broadcast_to(cw_ref[j:j + 1, :], (SUBLANES, D_MODEL))

    x = x_ref[0]
    xn_s[...] = _rmsnorm(x, mixg_ref[...]).astype(BF16)
    xn = xn_s[...]

    zi = _mm(xn, w_ref[:, IG0:IG0 + LANES]) + bi_ref[...]
    lf = _log_sigmoid(_mm(xn, w_ref[:, FG0:FG0 + LANES]) + bf_ref[...])
    b_all = _cumsum_rows(lf)
    a_all = zi - b_all
    a_t = a_all.T

    for h in range(N_HEADS):
        cs = slice(h * DH, (h + 1) * DH)
        gv = _mm(xn, w_ref[:, GV0 + h * DH:GV0 + (h + 1) * DH])
        gg = _mm(xn, w_ref[:, GG0 + h * DH:GG0 + (h + 1) * DH])
        full_s[CONV_LEAD:CONV_LEAD + T, cs] = gv * _sigmoid(gg)
        q = _mm(xn, w_ref[:, Q0 + h * DH:Q0 + (h + 1) * DH])
        k = _mm(xn, w_ref[:, K0 + h * DH:K0 + (h + 1) * DH]) * (DH ** -0.5)
        v = _mm(xn, w_ref[:, V0 + h * DH:V0 + (h + 1) * DH])
        o = _mm(xn, w_ref[:, O0 + h * DH:O0 + (h + 1) * DH])
        _causal_conv_cols(full_s.at[:, cs], cwb_s, cb_ref, uc_s.at[:, cs], cs, T)
        hh, c_new, n_new, m_new = _mlstm_head(
            q, k, v, b_all[:, h:h + 1], a_all[:, h:h + 1], a_t[h:h + 1, :],
            m_s[h:h + 1, 0:1], functools.partial(c_ref.__getitem__, (0, h)), n_ref[0, h:h + 1, :], T)
        c_ref[0, h] = c_new
        n_ref[0, h:h + 1, :] = n_new
        m_s[h:h + 1, :] = jnp.broadcast_to(m_new, (1, LANES))
        ha_s[:, cs] = _head_out(hh, hg_ref[:, cs], o).astype(BF16)
        ga_s[:, cs] = _mm(xn, w_ref[:, GA0 + h * DH:GA0 + (h + 1) * DH])
        gb_s[:, cs] = _mm(xn, w_ref[:, GB0 + h * DH:GB0 + (h + 1) * DH])

    y_a = _mm(ha_s[...], wmo_ref[...])
    tail = full_s[T + CONV_LEAD - (CONV_K - 1):T + CONV_LEAD, :]
    conv_ref[0] = tail
    full_s[CONV_LEAD - (CONV_K - 1):CONV_LEAD, :] = tail

    ucn = _layernorm(uc_s[...], lng_ref[...], lnb_ref[...])
    y_b = _mm((ucn * _sigmoid(ucn)).astype(BF16), wco_ref[...])

    mix = _sigmoid(ga_s[...]) * y_a + _sigmoid(gb_s[...]) * y_b
    x1_ref[0] = x + _mm(mix.astype(BF16), wout_ref[...])

    @pl.when(c == n_chunks - 1)
    def _():
        for h in range(N_HEADS):
            m_ref[0, :, h:h + 1] = m_s[h:h + 1, 0:1]


def _prompt_mixer(x, mixg, w_all, bi, bf, hg, wmo, cw, cb, lng, lnb, wco, wout):
    B, S, D = x.shape
    T = PROMPT_CHUNK
    nc = S // T
    kern = functools.partial(_prompt_mixer_kernel, n_chunks=nc)
    return pl.pallas_call(
        kern,
        grid=(B, nc),
        in_specs=[
            pl.BlockSpec((1, T, D), lambda b, c: (b, c, 0)),
            _const_spec((1, D)), _const_spec((D, N_Z)), _const_spec((1, LANES)), _const_spec((1, LANES)),
            _const_spec((1, D)), _const_spec((D, D)), _const_spec((CONV_K, D)), _const_spec((1, D)),
            _const_spec((1, D)), _const_spec((1, D)), _const_spec((D, D)), _const_spec((D, D)),
        ],
        out_specs=[
            pl.BlockSpec((1, T, D), lambda b, c: (b, c, 0)),
            pl.BlockSpec((None, 1, N_HEADS, DH, DH), lambda b, c: (0, b, 0, 0, 0)),
            pl.BlockSpec((None, 1, N_HEADS, DH), lambda b, c: (0, b, 0, 0)),
            pl.BlockSpec((1, 1, N_HEADS), lambda b, c: (b, 0, 0)),
            pl.BlockSpec((None, 1, CONV_K - 1, D), lambda b, c: (0, b, 0, 0)),
        ],
        out_shape=[
            jax.ShapeDtypeStruct((B, S, D), F32),
            jax.ShapeDtypeStruct((1, B, N_HEADS, DH, DH), F32),
            jax.ShapeDtypeStruct((1, B, N_HEADS, DH), F32),
            jax.ShapeDtypeStruct((B, 1, N_HEADS), F32),
            jax.ShapeDtypeStruct((1, B, CONV_K - 1, D), F32),
        ],
        scratch_shapes=[
            pltpu.VMEM((T, D), BF16), pltpu.VMEM((T, D), BF16),
            pltpu.VMEM((T, D), F32), pltpu.VMEM((T, D), F32),
            pltpu.VMEM((T + CONV_LEAD + SUBLANES, D), F32), pltpu.VMEM((T, D), F32),
            pltpu.VMEM((SUBLANES, LANES), F32), pltpu.VMEM((CONV_K, SUBLANES, D), F32),
        ],
        compiler_params=pltpu.CompilerParams(
            dimension_semantics=("arbitrary", "arbitrary"), vmem_limit_bytes=VMEM_LIMIT),
        name="prompt_mixer",
    )(x, mixg, w_all, bi, bf, hg, wmo, cw, cb, lng, lnb, wco, wout)


def _ffn_tail(upc_a, upc_g, wdown_ref, x, fing_ref):
    act = (upc_a * _sigmoid(upc_a) * upc_g).astype(BF16)
    x2 = x + _mm(act, wdown_ref[...])
    return _rmsnorm(x2, fing_ref[...])


def _prompt_ffn_kernel(x_ref, fg_ref, wup_ref, fw_ref, fb_ref, wdown_ref, fing_ref,
                       y_ref, ffn_ref, up_s, act_s):
    T = x_ref.shape[1]
    c = pl.program_id(1)

    @pl.when(c == 0)
    def _():
        up_s[0:8, :] = jnp.zeros((8, 2 * D_FF), F32)

    x = x_ref[0]
    hf = _rmsnorm(x, fg_ref[...]).astype(BF16)
    nb = 2 * D_FF // DH
    for blk in range(nb):
        cs = slice(blk * DH, (blk + 1) * DH)
        up_s[8:8 + T, cs] = _mm(hf, wup_ref[:, cs])
    for blk in range(D_FF // DH):
        ca = slice(blk * DH, (blk + 1) * DH)
        cg = slice(D_FF + blk * DH, D_FF + (blk + 1) * DH)
        a = fb_ref[:, ca] + fw_ref[0:1, ca] * up_s[6:6 + T, ca] + fw_ref[1:2, ca] * up_s[7:7 + T, ca] \
            + fw_ref[2:3, ca] * up_s[8:8 + T, ca]
        g = fb_ref[:, cg] + fw_ref[0:1, cg] * up_s[6:6 + T, cg] + fw_ref[1:2, cg] * up_s[7:7 + T, cg] \
            + fw_ref[2:3, cg] * up_s[8:8 + T, cg]
        act_s[:, ca] = (a * _sigmoid(a) * g).astype(BF16)
    tail = up_s[T + 6:T + 8, :]
    ffn_ref[0] = tail
    up_s[6:8, :] = tail
    x2 = x + _mm(act_s[...], wdown_ref[...])
    y_ref[0] = _rmsnorm(x2, fing_ref[...])


def _prompt_ffn(x1, fg, wup, fw, fb, wdown, fing):
    B, S, D = x1.shape
    T = PROMPT_CHUNK
    nc = S // T
    return pl.pallas_call(
        _prompt_ffn_kernel,
        grid=(B, nc),
        in_specs=[
            pl.BlockSpec((1, T, D), lambda b, c: (b, c, 0)),
            _const_spec((1, D)), _const_spec((D, 2 * D_FF)), _const_spec((FFN_K, 2 * D_FF)),
            _const_spec((1, 2 * D_FF)), _const_spec((D_FF, D)), _const_spec((1, D)),
        ],
        out_specs=[
            pl.BlockSpec((1, T, D), lambda b, c: (b, c, 0)),
            pl.BlockSpec((None, 1, FFN_K - 1, 2 * D_FF), lambda b, c: (0, b, 0, 0)),
        ],
        out_shape=[
            jax.ShapeDtypeStruct((B, S, D), F32),
            jax.ShapeDtypeStruct((1, B, FFN_K - 1, 2 * D_FF), F32),
        ],
        scratch_shapes=[pltpu.VMEM((T + 8, 2 * D_FF), F32), pltpu.VMEM((T, D_FF), BF16)],
        compiler_params=pltpu.CompilerParams(
            dimension_semantics=("arbitrary", "arbitrary"), vmem_limit_bytes=VMEM_LIMIT),
        name="prompt_ffn",
    )(x1, fg, wup, fw, fb, wdown, fing)


def _sample_proj_kernel(x_ref, mixg_ref, w_ref, z_ref):
    xn = _rmsnorm(x_ref[...], mixg_ref[...]).astype(BF16)
    for blk in range(N_Z // DH):
        cs = slice(blk * DH, (blk + 1) * DH)
        z_ref[:, cs] = _mm(xn, w_ref[:, cs])


def _sample_proj(x, mixg, w_all):
    R, D = x.shape
    return pl.pallas_call(
        _sample_proj_kernel,
        grid=(R // SAMPLE_ROWS,),
        in_specs=[pl.BlockSpec((SAMPLE_ROWS, D), lambda i: (i, 0)),
                  _const_spec((1, D)), _const_spec((D, N_Z))],
        out_specs=pl.BlockSpec((SAMPLE_ROWS, N_Z), lambda i: (i, 0)),
        out_shape=jax.ShapeDtypeStruct((R, N_Z), F32),
        compiler_params=pltpu.CompilerParams(
            dimension_semantics=("arbitrary",), vmem_limit_bytes=VMEM_LIMIT),
        name="sample_proj",
    )(x, mixg, w_all)


def _sample_mlstm_kernel(zq_ref, zg_ref, bi_ref, bf_ref, c_ref, n_ref, m_ref,
                         h_ref, co_ref, no_ref, mo_ref, *, t_seq):
    per_win = SUBLANES // t_seq
    pad = LANES - SUBLANES
    items = []
    for win in range(MLSTM_SEQS // per_win):
        zq_w = zq_ref[SUBLANES * win:SUBLANES * (win + 1), :]
        zg_w = zg_ref[SUBLANES * win:SUBLANES * (win + 1), :]
        for sub in range(per_win):
            j = win * per_win + sub
            if sub:
                zq = pltpu.roll(zq_w, SUBLANES - sub * t_seq, axis=0)
                zg = pltpu.roll(zg_w, SUBLANES - sub * t_seq, axis=0)
            else:
                zq, zg = zq_w, zg_w
            zi = zg[:, 0:LANES] + bi_ref[...]
            lf = _log_sigmoid(zg[:, LANES:2 * LANES] + bf_ref[...])
            b_all = _cumsum_rows(lf)
            a_all = jnp.concatenate([zi - b_all, jnp.zeros((pad, LANES), F32)], axis=0)
            a_t = a_all.T
            for h in range(N_HEADS):
                q = zq[:, Q0 + h * DH:Q0 + (h + 1) * DH]
                k = zq[:, K0 + h * DH:K0 + (h + 1) * DH] * (DH ** -0.5)
                v = zq[:, V0 + h * DH:V0 + (h + 1) * DH]
                zeros = jnp.zeros((pad, DH), F32)
                k = jnp.concatenate([k, zeros], axis=0)
                v = jnp.concatenate([v, zeros], axis=0)
                out = {}
                steps = _mlstm_head_steps(
                    q, k, v, b_all[:, h:h + 1], a_all[:, h:h + 1], a_t[h:h + 1, :],
                    m_ref[j:j + 1, h:h + 1], functools.partial(c_ref.__getitem__, (j, h)),
                    n_ref[j, h:h + 1, :], t_seq, out)
                items.append((j, h, out, steps))
    _round_robin(steps for _, _, _, steps in items)
    for j, h, out, _ in items:
        co_ref[j, h] = out["c"]
        no_ref[j, h:h + 1, :] = out["n"]
        mo_ref[j:j + 1, h:h + 1] = out["m"]
        h_ref[j * t_seq:(j + 1) * t_seq, h * DH:(h + 1) * DH] = out["h"][0:t_seq, :]


def _sample_mlstm(z, bi, bf, c0, n0, m0, t_seq):
    nb = c0.shape[1]
    sb = MLSTM_SEQS
    rows = sb * t_seq
    kern = functools.partial(_sample_mlstm_kernel, t_seq=t_seq)
    c_spec = pl.BlockSpec((None, sb, N_HEADS, DH, DH), lambda i: (0, i, 0, 0, 0))
    n_spec = pl.BlockSpec((None, sb, N_HEADS, DH), lambda i: (0, i, 0, 0))
    m_spec = pl.BlockSpec((None, sb, N_HEADS), lambda i: (0, i, 0))
    return pl.pallas_call(
        kern,
        grid=(nb // sb,),
        in_specs=[
            pl.BlockSpec((rows, 3 * D_MODEL), lambda i: (i, 0)),
            pl.BlockSpec((rows, 2 * LANES), lambda i: (i, IG0 // (2 * LANES))),
            _const_spec((1, LANES)), _const_spec((1, LANES)),
            c_spec, n_spec, m_spec,
        ],
        out_specs=[pl.BlockSpec((rows, D_MODEL), lambda i: (i, 0)), c_spec, n_spec, m_spec],
        out_shape=[
            jax.ShapeDtypeStruct((nb * t_seq, D_MODEL), F32),
            jax.ShapeDtypeStruct(c0.shape, F32),
            jax.ShapeDtypeStruct(n0.shape, F32),
            jax.ShapeDtypeStruct(m0.shape, F32),
        ],
        compiler_params=pltpu.CompilerParams(
            dimension_semantics=("arbitrary",), vmem_limit_bytes=VMEM_LIMIT),
        name="sample_mlstm",
    )(z, z, bi, bf, c0, n0, m0)


def _sample_mixer_tail_kernel(x_ref, z_ref, h_ref, cst_ref, hg_ref, wmo_ref, cw4_ref, cb_ref,
                              lng_ref, lnb_ref, wco_ref, wout_ref,
                              x1_ref, cnew_ref, full_s, uc_s, *, t_seq):
    R = x_ref.shape[0]
    n_seq = R // t_seq
    hist = CONV_K - 1
    ha = jnp.concatenate(
        [_head_out(h_ref[:, h * DH:(h + 1) * DH], hg_ref[:, h * DH:(h + 1) * DH],
                   z_ref[:, O0 + h * DH:O0 + (h + 1) * DH]) for h in range(N_HEADS)], axis=1)
    y_a = _mm(ha.astype(BF16), wmo_ref[...])

    uc_s[...] = z_ref[:, GV0:GV0 + D_MODEL] * _sigmoid(z_ref[:, GG0:GG0 + D_MODEL])
    full_s[...] = jnp.zeros_like(full_s)
    for j in range(n_seq):
        full_s[0:hist, :] = cst_ref[j]
        full_s[hist:hist + t_seq, :] = uc_s[j * t_seq:(j + 1) * t_seq, :]
        full = full_s[...]
        cnew_ref[j] = full_s[t_seq:t_seq + hist, :]
        for t in range(t_seq):
            uc_s[j * t_seq + t:j * t_seq + t + 1, :] = (
                jnp.sum(cw4_ref[t] * full, axis=0, keepdims=True) + cb_ref[...])

    ucn = _layernorm(uc_s[...], lng_ref[...], lnb_ref[...])
    y_b = _mm((ucn * _sigmoid(ucn)).astype(BF16), wco_ref[...])
    mix = _sigmoid(z_ref[:, GA0:GA0 + D_MODEL]) * y_a + _sigmoid(z_ref[:, GB0:GB0 + D_MODEL]) * y_b
    x1_ref[...] = x_ref[...] + _mm(mix.astype(BF16), wout_ref[...])


def _sample_mixer_tail(x, z, h, cst, hg, wmo, cw4, cb, lng, lnb, wco, wout, t_seq):
    R, D = x.shape
    rows = SAMPLE_ROWS
    sb = rows // t_seq
    hist = CONV_K - 1
    win = cw4.shape[1]
    kern = functools.partial(_sample_mixer_tail_kernel, t_seq=t_seq)
    st_spec = pl.BlockSpec((None, sb, hist, D), lambda i: (0, i, 0, 0))
    return pl.pallas_call(
        kern,
        grid=(R // rows,),
        in_specs=[
            pl.BlockSpec((rows, D), lambda i: (i, 0)),
            pl.BlockSpec((rows, N_Z), lambda i: (i, 0)),
            pl.BlockSpec((rows, D), lambda i: (i, 0)),
            st_spec,
            _const_spec((1, D)), _const_spec((D, D)), _const_spec((t_seq, win, D)), _const_spec((1, D)),
            _const_spec((1, D)), _const_spec((1, D)), _const_spec((D, D)), _const_spec((D, D)),
        ],
        out_specs=[pl.BlockSpec((rows, D), lambda i: (i, 0)), st_spec],
        out_shape=[
            jax.ShapeDtypeStruct((R, D), F32),
            jax.ShapeDtypeStruct(cst.shape, F32),
        ],
        scratch_shapes=[pltpu.VMEM((win, D), F32), pltpu.VMEM((rows, D), F32)],
        compiler_params=pltpu.CompilerParams(
            dimension_semantics=("arbitrary",), vmem_limit_bytes=VMEM_LIMIT),
        name="sample_mixer_tail",
    )(x, z, h, cst, hg, wmo, cw4, cb, lng, lnb, wco, wout)


def _sample_ffn_kernel(x_ref, fst_ref, fg_ref, wup_ref, fw_ref, fb_ref, wdown_ref, fing_ref,
                       y_ref, fnew_ref, up_s, seq_s, *, t_seq):
    R = x_ref.shape[0]
    n_seq = R // t_seq
    hist = FFN_K - 1
    x = x_ref[...]
    hf = _rmsnorm(x, fg_ref[...]).astype(BF16)
    for blk in range(2 * D_FF // DH):
        cs = slice(blk * DH, (blk + 1) * DH)
        up_s[:, cs] = _mm(hf, wup_ref[:, cs])
    for j in range(n_seq):
        rows = slice(j * t_seq, (j + 1) * t_seq)
        seq_s[0:hist, :] = fst_ref[j]
        seq_s[hist:hist + t_seq, :] = up_s[rows, :]
        fnew_ref[j] = seq_s[t_seq:t_seq + hist, :]
        acc = fb_ref[...] + fw_ref[0:1, :] * seq_s[0:t_seq, :]
        for kk in range(1, FFN_K):
            acc = acc + fw_ref[kk:kk + 1, :] * seq_s[kk:kk + t_seq, :]
        up_s[rows, :] = acc
    y_ref[...] = _ffn_tail(up_s[:, 0:D_FF], up_s[:, D_FF:2 * D_FF], wdown_ref, x, fing_ref)


def _sample_ffn(x1, fst, fg, wup, fw, fb, wdown, fing, t_seq):
    R, D = x1.shape
    rows = SAMPLE_ROWS
    sb = rows // t_seq
    hist = FFN_K - 1
    kern = functools.partial(_sample_ffn_kernel, t_seq=t_seq)
    st_spec = pl.BlockSpec((None, sb, hist, 2 * D_FF), lambda i: (0, i, 0, 0))
    return pl.pallas_call(
        kern,
        grid=(R // rows,),
        in_specs=[
            pl.BlockSpec((rows, D), lambda i: (i, 0)),
            st_spec,
            _const_spec((1, D)), _const_spec((D, 2 * D_FF)), _const_spec((FFN_K, 2 * D_FF)),
            _const_spec((1, 2 * D_FF)), _const_spec((D_FF, D)), _const_spec((1, D)),
        ],
        out_specs=[pl.BlockSpec((rows, D), lambda i: (i, 0)), st_spec],
        out_shape=[
            jax.ShapeDtypeStruct((R, D), F32),
            jax.ShapeDtypeStruct(fst.shape, F32),
        ],
        scratch_shapes=[pltpu.VMEM((rows, 2 * D_FF), F32), pltpu.VMEM((SUBLANES, 2 * D_FF), F32)],
        compiler_params=pltpu.CompilerParams(
            dimension_semantics=("arbitrary",), vmem_limit_bytes=VMEM_LIMIT),
        name="sample_ffn",
    )(x1, fst, fg, wup, fw, fb, wdown, fing)


def _pad_gate_bias(b):
    return jnp.pad(b, (0, LANES - N_HEADS)).reshape(1, LANES)


def _shifted_conv_weights(cw, t_seq):
    k = cw.shape[0]
    win = -(-(k - 1 + t_seq) // SUBLANES) * SUBLANES
    return jnp.stack([jnp.pad(cw, ((t, win - k - t), (0, 0))) for t in range(t_seq)])


def kernel(x_prompt, x_sample, state_C, state_n, state_m, state_conv, state_ffn, mix_norm_g, w_in, b_if,
           head_norm_g, w_mlstm_out, conv_w, conv_b, conv_ln_g, conv_ln_b, w_conv_out, w_out, ffn_norm_g,
           w_up, ffn_conv_w, ffn_conv_b, w_down, final_norm_g):
    depth = w_in.shape[0]
    assert depth == 1, "single-layer trunk"
    l = 0
    row = lambda a: a.reshape(1, -1)
    w_all = _pack_w_in(jnp.swapaxes(w_in, 1, 2))
    bi = _pad_gate_bias(b_if[l, :N_HEADS])
    bf = _pad_gate_bias(b_if[l, N_HEADS:])
    mixg, hg = row(mix_norm_g[l]), row(head_norm_g[l])
    wmo, wco, wout = (w.astype(BF16) for w in (w_mlstm_out[l], w_conv_out[l], w_out[l]))
    cw, cb = conv_w[l], row(conv_b[l])
    lng, lnb = row(conv_ln_g[l]), row(conv_ln_b[l])
    fg, fing = row(ffn_norm_g[l]), row(final_norm_g)
    wup, wdown = w_up[l].astype(BF16), w_down[l].astype(BF16)
    fw, fb = ffn_conv_w[l], row(ffn_conv_b[l])

    x1p, c_p, n_p, m_p, conv_p = _prompt_mixer(x_prompt, mixg, w_all, bi, bf, hg, wmo, cw, cb, lng, lnb, wco, wout)
    y_p, ffn_p = _prompt_ffn(x1p, fg, wup, fw, fb, wdown, fing)

    nb, t_seq, d = x_sample.shape
    xs = x_sample.reshape(nb * t_seq, d)
    z = _sample_proj(xs, mixg, w_all)
    h_s, c_s, n_s, m_s = _sample_mlstm(z, bi, bf, state_C, state_n, state_m, t_seq)
    cw4 = _shifted_conv_weights(cw, t_seq)
    x1s, conv_s = _sample_mixer_tail(xs, z, h_s, state_conv, hg, wmo, cw4, cb, lng, lnb, wco, wout, t_seq)
    y_s, ffn_s = _sample_ffn(x1s, state_ffn, fg, wup, fw, fb, wdown, fing, t_seq)

    return (y_p, y_s.reshape(nb, t_seq, d),
            c_p, n_p, m_p.reshape(1, -1, N_HEADS), conv_p, ffn_p,
            c_s, n_s, m_s, conv_s, ffn_s)
```

```python
import functools

import jax
import jax.numpy as jnp
from jax import lax
from jax.experimental import pallas as pl
from jax.experimental.pallas import tpu as pltpu

F32 = jnp.float32
BF16 = jnp.bfloat16

D_MODEL = 1024
N_HEADS = 4
DH = D_MODEL // N_HEADS
CONV_K = 31
D_FF = 2816
FFN_K = 3
EPS = 1e-6

LANES = 128
Q0, K0, V0, O0, GV0, GG0, GA0, GB0 = (i * D_MODEL for i in range(8))
IG0 = 8 * D_MODEL
FG0 = IG0 + LANES
N_Z = FG0 + LANES

SUBLANES = 8
CONV_LEAD = 32
PROMPT_CHUNK = 256
SAMPLE_ROWS = 128
MLSTM_SEQS = 8
PACK_ROWS = 128
VMEM_LIMIT = 58 * 1024 * 1024


def _mm(a, b):
    return jnp.dot(a, b, preferred_element_type=F32)


def _mm_nt(a, b):
    return lax.dot_general(a, b, (((1,), (1,)), ((), ())), preferred_element_type=F32)


def _mm_tn(a, b):
    return lax.dot_general(a, b, (((0,), (0,)), ((), ())), preferred_element_type=F32)


def _rmsnorm(x, g):
    return x * lax.rsqrt(jnp.mean(x * x, axis=-1, keepdims=True) + EPS) * g


def _layernorm(x, g, b):
    mu = jnp.mean(x, axis=-1, keepdims=True)
    xc = x - mu
    return xc * lax.rsqrt(jnp.mean(xc * xc, axis=-1, keepdims=True) + EPS) * g + b


def _sigmoid(x):
    return 0.5 * jnp.tanh(0.5 * x) + 0.5


def _log_sigmoid(x):
    return jnp.minimum(x, 0.0) - jnp.log1p(jnp.exp(-jnp.abs(x)))


def _cumsum_rows(x):
    n = x.shape[0]
    row = lax.broadcasted_iota(jnp.int32, x.shape, 0)
    s = 1
    while s < n:
        x = x + jnp.where(row >= s, pltpu.roll(x, s, axis=0), 0.0)
        s *= 2
    return x


def _mlstm_head_steps(q, k, v, b_col, a_col, a_row, m_prev, c_prev, n_prev, n_valid, out):
    L, LK = q.shape[0], k.shape[0]
    row = lax.broadcasted_iota(jnp.int32, (L, LK), 0)
    col = lax.broadcasted_iota(jnp.int32, (L, LK), 1)
    d = jnp.where(col <= row, b_col + a_row, -jnp.inf)
    inter = b_col + m_prev
    d_max = jnp.max(d, axis=1, keepdims=True)
    qb, kb, vb = q.astype(BF16), k.astype(BF16), v.astype(BF16)
    qk = _mm_nt(qb, kb)
    yield
    m_t = jnp.maximum(inter, d_max)
    w_inter = jnp.exp(inter - m_t)
    p = jnp.exp(d - m_t)
    s = qk * p
    qc = _mm_nt(qb, c_prev().astype(BF16))
    yield
    num = w_inter * qc + _mm(s.astype(BF16), vb)
    den = w_inter * jnp.sum(q * n_prev, axis=1, keepdims=True) + jnp.sum(s, axis=1, keepdims=True)
    m_new = m_t[n_valid - 1:n_valid, :]
    b_last = b_col[n_valid - 1:n_valid, :]
    g_inter = jnp.exp(b_last + m_prev - m_new)
    g_col = jnp.exp(b_last + a_col - m_new)
    if n_valid < LK:
        g_col = jnp.where(lax.broadcasted_iota(jnp.int32, (LK, 1), 0) < n_valid, g_col, 0.0)
    yield
    out["h"] = num * (1.0 / jnp.maximum(jnp.abs(den), jnp.exp(-m_t)))
    out["c"] = g_inter * c_prev() + _mm_tn((g_col * v).astype(BF16), kb)
    out["n"] = g_inter * n_prev + jnp.sum(g_col * k, axis=0, keepdims=True)
    out["m"] = m_new
    yield


def _mlstm_head(q, k, v, b_col, a_col, a_row, m_prev, c_prev, n_prev, n_valid):
    out = {}
    for _ in _mlstm_head_steps(q, k, v, b_col, a_col, a_row, m_prev, c_prev, n_prev, n_valid, out):
        pass
    return out["h"], out["c"], out["n"], out["m"]


def _round_robin(streams):
    streams = list(streams)
    while streams:
        alive = []
        for gen in streams:
            try:
                next(gen)
                alive.append(gen)
            except StopIteration:
                pass
        streams = alive


def _head_out(h, head_g, o):
    mu = jnp.mean(h, axis=-1, keepdims=True)
    hc = h - mu
    hn = hc * lax.rsqrt(jnp.mean(hc * hc, axis=-1, keepdims=True) + EPS)
    return hn * head_g * _sigmoid(o)


def _causal_conv_cols(full_v, cwb_s, cb_ref, out_v, cs, T):
    off = CONV_LEAD - (CONV_K - 1)
    n_a = (off + CONV_K - 1) // SUBLANES + 1
    width = cs.stop - cs.start
    sub = lax.broadcasted_iota(jnp.int32, (SUBLANES, width), 0)
    bias = cb_ref[:, cs]
    tiles = {}

    def tile(i):
        if i not in tiles:
            tiles[i] = full_v[SUBLANES * i:SUBLANES * (i + 1), :]
        return tiles[i]

    prev = None
    for i in range(T // SUBLANES + 1):
        ys = []
        for r in range(SUBLANES):
            acc = None
            for a in range(n_a):
                j = SUBLANES * a + r - off
                if 0 <= j < CONV_K:
                    term = cwb_s[j, :, cs] * tile(i + a)
                    acc = term if acc is None else acc + term
            ys.append(acc)
        tiles.pop(i, None)
        cur = [ys[0]] + [pltpu.roll(ys[r], SUBLANES - r, axis=0) for r in range(1, SUBLANES)]
        if prev is not None:
            out = prev[0] + bias
            for r in range(1, SUBLANES):
                out = out + jnp.where(sub < SUBLANES - r, prev[r], cur[r])
            out_v[SUBLANES * (i - 1):SUBLANES * i, :] = out
        prev = cur


def _const_spec(shape):
    nd = len(shape)
    return pl.BlockSpec(shape, lambda *_: (0,) * nd, pipeline_mode=pl.Buffered(1))


def _pack_kernel(w_ref, o_ref):
    split = 4 * D_MODEL
    n_gate = 2 * N_HEADS
    rows = w_ref.shape[1]
    for src, dst in ((0, 0), (split + n_gate, split)):
        for t in range(split // LANES):
            blk = w_ref[src + t * LANES:src + (t + 1) * LANES, :]
            o_ref[:, dst + t * LANES:dst + (t + 1) * LANES] = blk.T.astype(BF16)
    g = jnp.concatenate([w_ref[split:split + n_gate, :], jnp.zeros((LANES - n_gate, rows), F32)], axis=0).T
    lane = lax.broadcasted_iota(jnp.int32, g.shape, 1)
    o_ref[:, IG0:IG0 + LANES] = jnp.where(lane < N_HEADS, g, 0.0).astype(BF16)
    o_ref[:, FG0:FG0 + LANES] = jnp.where(
        lane < N_HEADS, pltpu.roll(g, LANES - N_HEADS, axis=1), 0.0).astype(BF16)


def _pack_w_in(w_t):
    _, n_in, d = w_t.shape
    return pl.pallas_call(
        _pack_kernel,
        grid=(d // PACK_ROWS,),
        in_specs=[pl.BlockSpec((None, n_in, PACK_ROWS), lambda i: (0, 0, i))],
        out_specs=pl.BlockSpec((PACK_ROWS, N_Z), lambda i: (i, 0)),
        out_shape=jax.ShapeDtypeStruct((d, N_Z), BF16),
        compiler_params=pltpu.CompilerParams(
            dimension_semantics=("arbitrary",), vmem_limit_bytes=VMEM_LIMIT),
        name="pack_w_in",
    )(w_t)


def _prompt_mixer_kernel(x_ref, mixg_ref, w_ref, bi_ref, bf_ref, hg_ref, wmo_ref, cw_ref, cb_ref,
                         lng_ref, lnb_ref, wco_ref, wout_ref,
                         x1_ref, c_ref, n_ref, m_ref, conv_ref,
                         xn_s, ha_s, ga_s, gb_s, full_s, uc_s, m_s, cwb_s, *, n_chunks):
    T = x_ref.shape[1]
    c = pl.program_id(1)

    @pl.when(c == 0)
    def _():
        c_ref[...] = jnp.zeros_like(c_ref)
        n_ref[...] = jnp.zeros_like(n_ref)
        m_s[...] = jnp.zeros_like(m_s)
        full_s[0:CONV_LEAD, :] = jnp.zeros((CONV_LEAD, D_MODEL), F32)
        full_s[CONV_LEAD + T:CONV_LEAD + T + SUBLANES, :] = jnp.zeros((SUBLANES, D_MODEL), F32)
        for j in range(CONV_K):
            cwb_s[j] = jnp.broadcast_to(cw_ref[j:j + 1, :], (SUBLANES, D_MODEL))

    x = x_ref[0]
    xn_s[...] = _rmsnorm(x, mixg_ref[...]).astype(BF16)
    xn = xn_s[...]

    zi = _mm(xn, w_ref[:, IG0:IG0 + LANES]) + bi_ref[...]
    lf = _log_sigmoid(_mm(xn, w_ref[:, FG0:FG0 + LANES]) + bf_ref[...])
    b_all = _cumsum_rows(lf)
    a_all = zi - b_all
    a_t = a_all.T

    for h in range(N_HEADS):
        cs = slice(h * DH, (h + 1) * DH)
        gv = _mm(xn, w_ref[:, GV0 + h * DH:GV0 + (h + 1) * DH])
        gg = _mm(xn, w_ref[:, GG0 + h * DH:GG0 + (h + 1) * DH])
        full_s[CONV_LEAD:CONV_LEAD + T, cs] = gv * _sigmoid(gg)
        q = _mm(xn, w_ref[:, Q0 + h * DH:Q0 + (h + 1) * DH])
        k = _mm(xn, w_ref[:, K0 + h * DH:K0 + (h + 1) * DH]) * (DH ** -0.5)
        v = _mm(xn, w_ref[:, V0 + h * DH:V0 + (h + 1) * DH])
        o = _mm(xn, w_ref[:, O0 + h * DH:O0 + (h + 1) * DH])
        _causal_conv_cols(full_s.at[:, cs], cwb_s, cb_ref, uc_s.at[:, cs], cs, T)
        hh, c_new, n_new, m_new = _mlstm_head(
            q, k, v, b_all[:, h:h + 1], a_all[:, h:h + 1], a_t[h:h + 1, :],
            m_s[h:h + 1, 0:1], functools.partial(c_ref.__getitem__, (0, h)), n_ref[0, h:h + 1, :], T)
        c_ref[0, h] = c_new
        n_ref[0, h:h + 1, :] = n_new
        m_s[h:h + 1, :] = jnp.broadcast_to(m_new, (1, LANES))
        ha_s[:, cs] = _head_out(hh, hg_ref[:, cs], o).astype(BF16)
        ga_s[:, cs] = _mm(xn, w_ref[:, GA0 + h * DH:GA0 + (h + 1) * DH])
        gb_s[:, cs] = _mm(xn, w_ref[:, GB0 + h * DH:GB0 + (h + 1) * DH])

    y_a = _mm(ha_s[...], wmo_ref[...])
    tail = full_s[T + CONV_LEAD - (CONV_K - 1):T + CONV_LEAD, :]
    conv_ref[0] = tail
    full_s[CONV_LEAD - (CONV_K - 1):CONV_LEAD, :] = tail

    ucn = _layernorm(uc_s[...], lng_ref[...], lnb_ref[...])
    y_b = _mm((ucn * _sigmoid(ucn)).astype(BF16), wco_ref[...])

    mix = _sigmoid(ga_s[...]) * y_a + _sigmoid(gb_s[...]) * y_b
    x1_ref[0] = x + _mm(mix.astype(BF16), wout_ref[...])

    @pl.when(c == n_chunks - 1)
    def _():
        for h in range(N_HEADS):
            m_ref[0, :, h:h + 1] = m_s[h:h + 1, 0:1]


def _prompt_mixer(x, mixg, w_all, bi, bf, hg, wmo, cw, cb, lng, lnb, wco, wout):
    B, S, D = x.shape
    T = PROMPT_CHUNK
    nc = S // T
    kern = functools.partial(_prompt_mixer_kernel, n_chunks=nc)
    return pl.pallas_call(
        kern,
        grid=(B, nc),
        in_specs=[
            pl.BlockSpec((1, T, D), lambda b, c: (b, c, 0)),
            _const_spec((1, D)), _const_spec((D, N_Z)), _const_spec((1, LANES)), _const_spec((1, LANES)),
            _const_spec((1, D)), _const_spec((D, D)), _const_spec((CONV_K, D)), _const_spec((1, D)),
            _const_spec((1, D)), _const_spec((1, D)), _const_spec((D, D)), _const_spec((D, D)),
        ],
        out_specs=[
            pl.BlockSpec((1, T, D), lambda b, c: (b, c, 0)),
            pl.BlockSpec((None, 1, N_HEADS, DH, DH), lambda b, c: (0, b, 0, 0, 0)),
            pl.BlockSpec((None, 1, N_HEADS, DH), lambda b, c: (0, b, 0, 0)),
            pl.BlockSpec((1, 1, N_HEADS), lambda b, c: (b, 0, 0)),
            pl.BlockSpec((None, 1, CONV_K - 1, D), lambda b, c: (0, b, 0, 0)),
        ],
        out_shape=[
            jax.ShapeDtypeStruct((B, S, D), F32),
            jax.ShapeDtypeStruct((1, B, N_HEADS, DH, DH), F32),
            jax.ShapeDtypeStruct((1, B, N_HEADS, DH), F32),
            jax.ShapeDtypeStruct((B, 1, N_HEADS), F32),
            jax.ShapeDtypeStruct((1, B, CONV_K - 1, D), F32),
        ],
        scratch_shapes=[
            pltpu.VMEM((T, D), BF16), pltpu.VMEM((T, D), BF16),
            pltpu.VMEM((T, D), F32), pltpu.VMEM((T, D), F32),
            pltpu.VMEM((T + CONV_LEAD + SUBLANES, D), F32), pltpu.VMEM((T, D), F32),
            pltpu.VMEM((SUBLANES, LANES), F32), pltpu.VMEM((CONV_K, SUBLANES, D), F32),
        ],
        compiler_params=pltpu.CompilerParams(
            dimension_semantics=("arbitrary", "arbitrary"), vmem_limit_bytes=VMEM_LIMIT),
        name="prompt_mixer",
    )(x, mixg, w_all, bi, bf, hg, wmo, cw, cb, lng, lnb, wco, wout)


def _ffn_tail(upc_a, upc_g, wdown_ref, x, fing_ref):
    act = (upc_a * _sigmoid(upc_a) * upc_g).astype(BF16)
    x2 = x + _mm(act, wdown_ref[...])
    return _rmsnorm(x2, fing_ref[...])


def _prompt_ffn_kernel(x_ref, fg_ref, wup_ref, fw_ref, fb_ref, wdown_ref, fing_ref,
                       y_ref, ffn_ref, up_s, act_s):
    T = x_ref.shape[1]
    c = pl.program_id(1)

    @pl.when(c == 0)
    def _():
        up_s[0:8, :] = jnp.zeros((8, 2 * D_FF), F32)

    x = x_ref[0]
    hf = _rmsnorm(x, fg_ref[...]).astype(BF16)
    nb = 2 * D_FF // DH
    for blk in range(nb):
        cs = slice(blk * DH, (blk + 1) * DH)
        up_s[8:8 + T, cs] = _mm(hf, wup_ref[:, cs])
    for blk in range(D_FF // DH):
        ca = slice(blk * DH, (blk + 1) * DH)
        cg = slice(D_FF + blk * DH, D_FF + (blk + 1) * DH)
        a = fb_ref[:, ca] + fw_ref[0:1, ca] * up_s[6:6 + T, ca] + fw_ref[1:2, ca] * up_s[7:7 + T, ca] \
            + fw_ref[2:3, ca] * up_s[8:8 + T, ca]
        g = fb_ref[:, cg] + fw_ref[0:1, cg] * up_s[6:6 + T, cg] + fw_ref[1:2, cg] * up_s[7:7 + T, cg] \
            + fw_ref[2:3, cg] * up_s[8:8 + T, cg]
        act_s[:, ca] = (a * _sigmoid(a) * g).astype(BF16)
    tail = up_s[T + 6:T + 8, :]
    ffn_ref[0] = tail
    up_s[6:8, :] = tail
    x2 = x + _mm(act_s[...], wdown_ref[...])
    y_ref[0] = _rmsnorm(x2, fing_ref[...])


def _prompt_ffn(x1, fg, wup, fw, fb, wdown, fing):
    B, S, D = x1.shape
    T = PROMPT_CHUNK
    nc = S // T
    return pl.pallas_call(
        _prompt_ffn_kernel,
        grid=(B, nc),
        in_specs=[
            pl.BlockSpec((1, T, D), lambda b, c: (b, c, 0)),
            _const_spec((1, D)), _const_spec((D, 2 * D_FF)), _const_spec((FFN_K, 2 * D_FF)),
            _const_spec((1, 2 * D_FF)), _const_spec((D_FF, D)), _const_spec((1, D)),
        ],
        out_specs=[
            pl.BlockSpec((1, T, D), lambda b, c: (b, c, 0)),
            pl.BlockSpec((None, 1, FFN_K - 1, 2 * D_FF), lambda b, c: (0, b, 0, 0)),
        ],
        out_shape=[
            jax.ShapeDtypeStruct((B, S, D), F32),
            jax.ShapeDtypeStruct((1, B, FFN_K - 1, 2 * D_FF), F32),
        ],
        scratch_shapes=[pltpu.VMEM((T + 8, 2 * D_FF), F32), pltpu.VMEM((T, D_FF), BF16)],
        compiler_params=pltpu.CompilerParams(
            dimension_semantics=("arbitrary", "arbitrary"), vmem_limit_bytes=VMEM_LIMIT),
        name="prompt_ffn",
    )(x1, fg, wup, fw, fb, wdown, fing)


def _sample_proj_kernel(x_ref, mixg_ref, w_ref, z_ref):
    xn = _rmsnorm(x_ref[...], mixg_ref[...]).astype(BF16)
    for blk in range(N_Z // DH):
        cs = slice(blk * DH, (blk + 1) * DH)
        z_ref[:, cs] = _mm(xn, w_ref[:, cs])


def _sample_proj(x, mixg, w_all):
    R, D = x.shape
    return pl.pallas_call(
        _sample_proj_kernel,
        grid=(R // SAMPLE_ROWS,),
        in_specs=[pl.BlockSpec((SAMPLE_ROWS, D), lambda i: (i, 0)),
                  _const_spec((1, D)), _const_spec((D, N_Z))],
        out_specs=pl.BlockSpec((SAMPLE_ROWS, N_Z), lambda i: (i, 0)),
        out_shape=jax.ShapeDtypeStruct((R, N_Z), F32),
        compiler_params=pltpu.CompilerParams(
            dimension_semantics=("arbitrary",), vmem_limit_bytes=VMEM_LIMIT),
        name="sample_proj",
    )(x, mixg, w_all)


def _sample_mlstm_kernel(zq_ref, zg_ref, bi_ref, bf_ref, c_ref, n_ref, m_ref,
                         h_ref, co_ref, no_ref, mo_ref, seq_s, *, t_seq):
    n_seq = zq_ref.shape[1]
    pad = LANES - SUBLANES
    wq = zq_ref.shape[2]
    seq_s[...] = jnp.zeros_like(seq_s)
    for j in range(n_seq):
        for t in range(t_seq):
            seq_s[j, t:t + 1, 0:wq] = zq_ref[t, j:j + 1, :]
            seq_s[j, t:t + 1, wq:] = zg_ref[t, j:j + 1, :]
    items = []
    for j in range(n_seq):
        zq = seq_s[j, :, 0:wq]
        zg = seq_s[j, :, wq:]
        zi = zg[:, 0:LANES] + bi_ref[...]
        lf = _log_sigmoid(zg[:, LANES:2 * LANES] + bf_ref[...])
        b_all = _cumsum_rows(lf)
        a_all = jnp.concatenate([zi - b_all, jnp.zeros((pad, LANES), F32)], axis=0)
        a_t = a_all.T
        for h in range(N_HEADS):
            q = zq[:, Q0 + h * DH:Q0 + (h + 1) * DH]
            k = zq[:, K0 + h * DH:K0 + (h + 1) * DH] * (DH ** -0.5)
            v = zq[:, V0 + h * DH:V0 + (h + 1) * DH]
            zeros = jnp.zeros((pad, DH), F32)
            k = jnp.concatenate([k, zeros], axis=0)
            v = jnp.concatenate([v, zeros], axis=0)
            out = {}
            steps = _mlstm_head_steps(
                q, k, v, b_all[:, h:h + 1], a_all[:, h:h + 1], a_t[h:h + 1, :],
                m_ref[j:j + 1, h:h + 1], functools.partial(c_ref.__getitem__, (j, h)),
                n_ref[j, h:h + 1, :], t_seq, out)
            items.append((j, h, out, steps))
    _round_robin(steps for _, _, _, steps in items)
    for j, h, out, _ in items:
        co_ref[j, h] = out["c"]
        no_ref[j, h:h + 1, :] = out["n"]
        mo_ref[j:j + 1, h:h + 1] = out["m"]
        for t in range(t_seq):
            h_ref[t, j:j + 1, h * DH:(h + 1) * DH] = out["h"][t:t + 1, :]


def _sample_mlstm(z, bi, bf, c0, n0, m0, t_seq):
    nb = c0.shape[1]
    sb = MLSTM_SEQS
    kern = functools.partial(_sample_mlstm_kernel, t_seq=t_seq)
    c_spec = pl.BlockSpec((None, sb, N_HEADS, DH, DH), lambda i: (0, i, 0, 0, 0))
    n_spec = pl.BlockSpec((None, sb, N_HEADS, DH), lambda i: (0, i, 0, 0))
    m_spec = pl.BlockSpec((None, sb, N_HEADS), lambda i: (0, i, 0))
    return pl.pallas_call(
        kern,
        grid=(nb // sb,),
        in_specs=[
            pl.BlockSpec((t_seq, sb, 3 * D_MODEL), lambda i: (0, i, 0)),
            pl.BlockSpec((t_seq, sb, 2 * LANES), lambda i: (0, i, IG0 // (2 * LANES))),
            _const_spec((1, LANES)), _const_spec((1, LANES)),
            c_spec, n_spec, m_spec,
        ],
        out_specs=[pl.BlockSpec((t_seq, sb, D_MODEL), lambda i: (0, i, 0)), c_spec, n_spec, m_spec],
        out_shape=[
            jax.ShapeDtypeStruct((t_seq, nb, D_MODEL), F32),
            jax.ShapeDtypeStruct(c0.shape, F32),
            jax.ShapeDtypeStruct(n0.shape, F32),
            jax.ShapeDtypeStruct(m0.shape, F32),
        ],
        scratch_shapes=[pltpu.VMEM((sb, SUBLANES, 3 * D_MODEL + 2 * LANES), F32)],
        compiler_params=pltpu.CompilerParams(
            dimension_semantics=("arbitrary",), vmem_limit_bytes=VMEM_LIMIT),
        name="sample_mlstm",
    )(z, z, bi, bf, c0, n0, m0)


def _rows(ref):
    return jnp.concatenate([ref[t] for t in range(ref.shape[0])], axis=0)


def _sample_mixer_tail_kernel(x_ref, z_ref, h_ref, cst_ref, hg_ref, wmo_ref, cw_ref, cb_ref,
                              lng_ref, lnb_ref, wco_ref, wout_ref,
                              x1_ref, cnew_ref, uc_s):
    t_seq, n_seq, _ = x_ref.shape
    hist = CONV_K - 1
    z = _rows(z_ref)
    hcat = _rows(h_ref)
    ha = jnp.concatenate(
        [_head_out(hcat[:, h * DH:(h + 1) * DH], hg_ref[:, h * DH:(h + 1) * DH],
                   z[:, O0 + h * DH:O0 + (h + 1) * DH]) for h in range(N_HEADS)], axis=1)
    y_a = _mm(ha.astype(BF16), wmo_ref[...])

    for t in range(t_seq):
        uc_s[t] = z_ref[t, :, GV0:GV0 + D_MODEL] * _sigmoid(z_ref[t, :, GG0:GG0 + D_MODEL])

    def plane(r):
        return cst_ref[r] if r < hist else uc_s[r - hist]

    outs = []
    for t in range(t_seq):
        acc = cb_ref[...] + cw_ref[0:1, :] * plane(t)
        for j in range(1, CONV_K):
            acc = acc + cw_ref[j:j + 1, :] * plane(t + j)
        outs.append(acc)
    for r in range(hist):
        cnew_ref[r] = plane(r + t_seq)
    uc = jnp.concatenate(outs, axis=0)

    ucn = _layernorm(uc, lng_ref[...], lnb_ref[...])
    y_b = _mm((ucn * _sigmoid(ucn)).astype(BF16), wco_ref[...])
    mix = _sigmoid(z[:, GA0:GA0 + D_MODEL]) * y_a + _sigmoid(z[:, GB0:GB0 + D_MODEL]) * y_b
    x1 = _rows(x_ref) + _mm(mix.astype(BF16), wout_ref[...])
    for t in range(t_seq):
        x1_ref[t] = x1[t * n_seq:(t + 1) * n_seq, :]


def _sample_mixer_tail(x, z, h, cst, hg, wmo, cw, cb, lng, lnb, wco, wout):
    t_seq, nb, D = x.shape
    sb = SAMPLE_ROWS // t_seq
    hist = CONV_K - 1
    st_spec = pl.BlockSpec((None, hist, sb, D), lambda i: (0, 0, i, 0))
    return pl.pallas_call(
        _sample_mixer_tail_kernel,
        grid=(nb // sb,),
        in_specs=[
            pl.BlockSpec((t_seq, sb, D), lambda i: (0, i, 0)),
            pl.BlockSpec((t_seq, sb, N_Z), lambda i: (0, i, 0)),
            pl.BlockSpec((t_seq, sb, D), lambda i: (0, i, 0)),
            st_spec,
            _const_spec((1, D)), _const_spec((D, D)), _const_spec((CONV_K, D)), _const_spec((1, D)),
            _const_spec((1, D)), _const_spec((1, D)), _const_spec((D, D)), _const_spec((D, D)),
        ],
        out_specs=[pl.BlockSpec((t_seq, sb, D), lambda i: (0, i, 0)), st_spec],
        out_shape=[
            jax.ShapeDtypeStruct((t_seq, nb, D), F32),
            jax.ShapeDtypeStruct(cst.shape, F32),
        ],
        scratch_shapes=[pltpu.VMEM((t_seq, sb, D), F32)],
        compiler_params=pltpu.CompilerParams(
            dimension_semantics=("arbitrary",), vmem_limit_bytes=VMEM_LIMIT),
        name="sample_mixer_tail",
    )(x, z, h, cst, hg, wmo, cw, cb, lng, lnb, wco, wout)


def _sample_ffn_kernel(x_ref, fst_ref, fg_ref, wup_ref, fw_ref, fb_ref, wdown_ref, fing_ref,
                       y_ref, fnew_ref, up_s, upc_s):
    t_seq, n_seq, _ = x_ref.shape
    hist = FFN_K - 1
    x = _rows(x_ref)
    hf = _rmsnorm(x, fg_ref[...]).astype(BF16)
    for blk in range(2 * D_FF // DH):
        cs = slice(blk * DH, (blk + 1) * DH)
        up = _mm(hf, wup_ref[:, cs])
        for t in range(t_seq):
            up_s[hist + t, :, cs] = up[t * n_seq:(t + 1) * n_seq, :]
    for j in range(n_seq):
        for r in range(hist):
            up_s[r, j:j + 1, :] = fst_ref[j, r:r + 1, :]
    for t in range(t_seq):
        acc = fb_ref[...] + fw_ref[0:1, :] * up_s[t]
        for kk in range(1, FFN_K):
            acc = acc + fw_ref[kk:kk + 1, :] * up_s[t + kk]
        upc_s[t] = acc
    for j in range(n_seq):
        for r in range(hist):
            fnew_ref[j, r:r + 1, :] = up_s[t_seq + r, j:j + 1, :]
    upc = _rows(upc_s)
    y = _ffn_tail(upc[:, 0:D_FF], upc[:, D_FF:2 * D_FF], wdown_ref, x, fing_ref)
    for t in range(t_seq):
        y_ref[t] = y[t * n_seq:(t + 1) * n_seq, :]


def _sample_ffn(x1, fst, fg, wup, fw, fb, wdown, fing):
    t_seq, nb, D = x1.shape
    sb = SAMPLE_ROWS // t_seq
    hist = FFN_K - 1
    st_spec = pl.BlockSpec((None, sb, hist, 2 * D_FF), lambda i: (0, i, 0, 0))
    return pl.pallas_call(
        _sample_ffn_kernel,
        grid=(nb // sb,),
        in_specs=[
            pl.BlockSpec((t_seq, sb, D), lambda i: (0, i, 0)),
            st_spec,
            _const_spec((1, D)), _const_spec((D, 2 * D_FF)), _const_spec((FFN_K, 2 * D_FF)),
            _const_spec((1, 2 * D_FF)), _const_spec((D_FF, D)), _const_spec((1, D)),
        ],
        out_specs=[pl.BlockSpec((t_seq, sb, D), lambda i: (0, i, 0)), st_spec],
        out_shape=[
            jax.ShapeDtypeStruct((t_seq, nb, D), F32),
            jax.ShapeDtypeStruct(fst.shape, F32),
        ],
        scratch_shapes=[pltpu.VMEM((t_seq + hist, sb, 2 * D_FF), F32), pltpu.VMEM((t_seq, sb, 2 * D_FF), F32)],
        compiler_params=pltpu.CompilerParams(
            dimension_semantics=("arbitrary",), vmem_limit_bytes=VMEM_LIMIT),
        name="sample_ffn",
    )(x1, fst, fg, wup, fw, fb, wdown, fing)


def _pad_gate_bias(b):
    return jnp.pad(b, (0, LANES - N_HEADS)).reshape(1, LANES)


def kernel(x_prompt, x_sample, state_C, state_n, state_m, state_conv, state_ffn, mix_norm_g, w_in, b_if,
           head_norm_g, w_mlstm_out, conv_w, conv_b, conv_ln_g, conv_ln_b, w_conv_out, w_out, ffn_norm_g,
           w_up, ffn_conv_w, ffn_conv_b, w_down, final_norm_g):
    depth = w_in.shape[0]
    assert depth == 1, "single-layer trunk"
    l = 0
    row = lambda a: a.reshape(1, -1)
    w_all = _pack_w_in(jnp.swapaxes(w_in, 1, 2))
    bi = _pad_gate_bias(b_if[l, :N_HEADS])
    bf = _pad_gate_bias(b_if[l, N_HEADS:])
    mixg, hg = row(mix_norm_g[l]), row(head_norm_g[l])
    wmo, wco, wout = (w.astype(BF16) for w in (w_mlstm_out[l], w_conv_out[l], w_out[l]))
    cw, cb = conv_w[l], row(conv_b[l])
    lng, lnb = row(conv_ln_g[l]), row(conv_ln_b[l])
    fg, fing = row(ffn_norm_g[l]), row(final_norm_g)
    wup, wdown = w_up[l].astype(BF16), w_down[l].astype(BF16)
    fw, fb = ffn_conv_w[l], row(ffn_conv_b[l])

    x1p, c_p, n_p, m_p, conv_p = _prompt_mixer(x_prompt, mixg, w_all, bi, bf, hg, wmo, cw, cb, lng, lnb, wco, wout)
    y_p, ffn_p = _prompt_ffn(x1p, fg, wup, fw, fb, wdown, fing)

    nb, t_seq, d = x_sample.shape
    xs = jnp.swapaxes(x_sample, 0, 1)
    z = _sample_proj(xs.reshape(t_seq * nb, d), mixg, w_all).reshape(t_seq, nb, N_Z)
    h_s, c_s, n_s, m_s = _sample_mlstm(z, bi, bf, state_C, state_n, state_m, t_seq)
    x1s, conv_t = _sample_mixer_tail(xs, z, h_s, jnp.swapaxes(state_conv, 1, 2), hg, wmo, cw, cb, lng, lnb,
                                     wco, wout)
    conv_s = jnp.swapaxes(conv_t, 1, 2)
    y_t, ffn_s = _sample_ffn(x1s, state_ffn, fg, wup, fw, fb, wdown, fing)
    y_s = jnp.swapaxes(y_t, 0, 1)

    return (y_p, y_s,
            c_p, n_p, m_p.reshape(1, -1, N_HEADS), conv_p, ffn_p,
            c_s, n_s, m_s, conv_s, ffn_s)
```

```python
import functools

import jax
import jax.numpy as jnp
from jax import lax
from jax.experimental import pallas as pl
from jax.experimental.pallas import tpu as pltpu

F32 = jnp.float32
BF16 = jnp.bfloat16

D_MODEL = 1024
N_HEADS = 4
DH = D_MODEL // N_HEADS
CONV_K = 31
D_FF = 2816
FFN_K = 3
EPS = 1e-6

LANES = 128
Q0, K0, V0, O0, GV0, GG0, GA0, GB0 = (i * D_MODEL for i in range(8))
IG0 = 8 * D_MODEL
N_Z = IG0 + LANES

SUBLANES = 8
CONV_LEAD = 32
FFN_LEAD = 8
PROMPT_CHUNK = 256
SAMPLE_ROWS = 128
MLSTM_SEQS = 8
PACK_ROWS = 128
VMEM_LIMIT = 58 * 1024 * 1024


def _mm(a, b):
    return jnp.dot(a, b, preferred_element_type=F32)


def _mm_nt(a, b):
    return lax.dot_general(a, b, (((1,), (1,)), ((), ())), preferred_element_type=F32)


def _mm_tn(a, b):
    return lax.dot_general(a, b, (((0,), (0,)), ((), ())), preferred_element_type=F32)


def _rmsnorm(x, g):
    return x * lax.rsqrt(jnp.mean(x * x, axis=-1, keepdims=True) + EPS) * g


def _layernorm(x, g, b):
    mu = jnp.mean(x, axis=-1, keepdims=True)
    xc = x - mu
    return xc * lax.rsqrt(jnp.mean(xc * xc, axis=-1, keepdims=True) + EPS) * g + b


def _sigmoid(x):
    return 0.5 * jnp.tanh(0.5 * x) + 0.5


def _log_sigmoid(x):
    return jnp.minimum(x, 0.0) - jnp.log1p(jnp.exp(-jnp.abs(x)))


def _cumsum_rows(x):
    n = x.shape[0]
    row = lax.broadcasted_iota(jnp.int32, x.shape, 0)
    s = 1
    while s < n:
        x = x + jnp.where(row >= s, pltpu.roll(x, s, axis=0), 0.0)
        s *= 2
    return x


def _gate_columns(zg, bias):
    zgb = zg + bias
    b_all = pltpu.roll(_cumsum_rows(_log_sigmoid(zgb)), LANES - N_HEADS, axis=1)
    return b_all, zgb - b_all


def _mlstm_head_steps(q, k, v, b_col, a_col, a_row, m_prev, c_prev, n_prev, n_valid, out):
    L, LK = q.shape[0], k.shape[0]
    row = lax.broadcasted_iota(jnp.int32, (L, LK), 0)
    col = lax.broadcasted_iota(jnp.int32, (L, LK), 1)
    d = jnp.where(col <= row, b_col + a_row, -jnp.inf)
    inter = b_col + m_prev
    d_max = jnp.max(d, axis=1, keepdims=True)
    qb, kb, vb = q.astype(BF16), k.astype(BF16), v.astype(BF16)
    qk = _mm_nt(qb, kb)
    yield
    m_t = jnp.maximum(inter, d_max)
    w_inter = jnp.exp(inter - m_t)
    p = jnp.exp(d - m_t)
    s = qk * p
    qc = _mm_nt(qb, c_prev().astype(BF16))
    yield
    num = w_inter * qc + _mm(s.astype(BF16), vb)
    den = w_inter * jnp.sum(q * n_prev, axis=1, keepdims=True) + jnp.sum(s, axis=1, keepdims=True)
    m_new = m_t[n_valid - 1:n_valid, :]
    b_last = b_col[n_valid - 1:n_valid, :]
    g_inter = jnp.exp(b_last + m_prev - m_new)
    g_col = jnp.exp(b_last + a_col - m_new)
    if n_valid < LK:
        g_col = jnp.where(lax.broadcasted_iota(jnp.int32, (LK, 1), 0) < n_valid, g_col, 0.0)
    yield
    out["h"] = num * (1.0 / jnp.maximum(jnp.abs(den), jnp.exp(-m_t)))
    out["c"] = g_inter * c_prev() + _mm_tn((g_col * v).astype(BF16), kb)
    out["n"] = g_inter * n_prev + jnp.sum(g_col * k, axis=0, keepdims=True)
    out["m"] = m_new
    yield


def _mlstm_head(q, k, v, b_col, a_col, a_row, m_prev, c_prev, n_prev, n_valid):
    out = {}
    for _ in _mlstm_head_steps(q, k, v, b_col, a_col, a_row, m_prev, c_prev, n_prev, n_valid, out):
        pass
    return out["h"], out["c"], out["n"], out["m"]


def _round_robin(streams):
    streams = list(streams)
    while streams:
        alive = []
        for gen in streams:
            try:
                next(gen)
                alive.append(gen)
            except StopIteration:
                pass
        streams = alive


def _head_out(h, head_g, o):
    mu = jnp.mean(h, axis=-1, keepdims=True)
    hc = h - mu
    hn = hc * lax.rsqrt(jnp.mean(hc * hc, axis=-1, keepdims=True) + EPS)
    return hn * head_g * _sigmoid(o)


def _causal_conv_cols(full_v, cwb_s, cb_ref, out_v, cs, T):
    off = CONV_LEAD - (CONV_K - 1)
    n_a = (off + CONV_K - 1) // SUBLANES + 1
    width = cs.stop - cs.start
    sub = lax.broadcasted_iota(jnp.int32, (SUBLANES, width), 0)
    bias = cb_ref[:, cs]
    tiles = {}

    def tile(i):
        if i not in tiles:
            tiles[i] = full_v[SUBLANES * i:SUBLANES * (i + 1), :]
        return tiles[i]

    prev = None
    for i in range(T // SUBLANES + 1):
        ys = []
        for r in range(SUBLANES):
            acc = None
            for a in range(n_a):
                j = SUBLANES * a + r - off
                if 0 <= j < CONV_K:
                    term = cwb_s[j, :, cs] * tile(i + a)
                    acc = term if acc is None else acc + term
            ys.append(acc)
        tiles.pop(i, None)
        cur = [ys[0]] + [pltpu.roll(ys[r], SUBLANES - r, axis=0) for r in range(1, SUBLANES)]
        if prev is not None:
            out = prev[0] + bias
            for r in range(1, SUBLANES):
                out = out + jnp.where(sub < SUBLANES - r, prev[r], cur[r])
            out_v[SUBLANES * (i - 1):SUBLANES * i, :] = out
        prev = cur


def _const_spec(shape):
    nd = len(shape)
    return pl.BlockSpec(shape, lambda *_: (0,) * nd, pipeline_mode=pl.Buffered(1))


def _pack_kernel(w_ref, o_ref):
    split = 4 * D_MODEL
    n_gate = 2 * N_HEADS
    rows = w_ref.shape[1]
    for src, dst in ((0, 0), (split + n_gate, split)):
        for t in range(split // LANES):
            blk = w_ref[src + t * LANES:src + (t + 1) * LANES, :]
            o_ref[:, dst + t * LANES:dst + (t + 1) * LANES] = blk.T.astype(BF16)
    g = jnp.concatenate([w_ref[split:split + n_gate, :], jnp.zeros((LANES - n_gate, rows), F32)], axis=0).T
    o_ref[:, IG0:IG0 + LANES] = g.astype(BF16)


def _pack_w_in(w_t):
    _, n_in, d = w_t.shape
    return pl.pallas_call(
        _pack_kernel,
        grid=(d // PACK_ROWS,),
        in_specs=[pl.BlockSpec((None, n_in, PACK_ROWS), lambda i: (0, 0, i))],
        out_specs=pl.BlockSpec((PACK_ROWS, N_Z), lambda i: (i, 0)),
        out_shape=jax.ShapeDtypeStruct((d, N_Z), BF16),
        compiler_params=pltpu.CompilerParams(
            dimension_semantics=("arbitrary",), vmem_limit_bytes=VMEM_LIMIT),
        name="pack_w_in",
    )(w_t)


def _prompt_mixer_kernel(x_ref, mixg_ref, w_ref, bg_ref, hg_ref, wmo_ref, cw_ref, cb_ref,
                         lng_ref, lnb_ref, wco_ref, wout_ref,
                         x1_ref, c_ref, n_ref, m_ref, conv_ref,
                         xn_s, ha_s, ga_s, gb_s, full_s, uc_s, m_s, cwb_s, *, n_chunks):
    T = x_ref.shape[1]
    c = pl.program_id(1)

    @pl.when(c == 0)
    def _():
        c_ref[...] = jnp.zeros_like(c_ref)
        n_ref[...] = jnp.zeros_like(n_ref)
        m_s[...] = jnp.zeros_like(m_s)
        full_s[0:CONV_LEAD, :] = jnp.zeros((CONV_LEAD, D_MODEL), F32)
        full_s[CONV_LEAD + T:CONV_LEAD + T + SUBLANES, :] = jnp.zeros((SUBLANES, D_MODEL), F32)
        for j in range(CONV_K):
            cwb_s[j] = jnp.broadcast_to(cw_ref[j:j + 1, :], (SUBLANES, D_MODEL))

    x = x_ref[0]
    xn_s[...] = _rmsnorm(x, mixg_ref[...]).astype(BF16)
    xn = xn_s[...]

    b_all, a_all = _gate_columns(_mm(xn, w_ref[:, IG0:IG0 + LANES]), bg_ref[...])
    a_t = a_all.T

    for h in range(N_HEADS):
        cs = slice(h * DH, (h + 1) * DH)
        gv = _mm(xn, w_ref[:, GV0 + h * DH:GV0 + (h + 1) * DH])
        gg = _mm(xn, w_ref[:, GG0 + h * DH:GG0 + (h + 1) * DH])
        full_s[CONV_LEAD:CONV_LEAD + T, cs] = gv * _sigmoid(gg)
        q = _mm(xn, w_ref[:, Q0 + h * DH:Q0 + (h + 1) * DH])
        k = _mm(xn, w_ref[:, K0 + h * DH:K0 + (h + 1) * DH]) * (DH ** -0.5)
        v = _mm(xn, w_ref[:, V0 + h * DH:V0 + (h + 1) * DH])
        o = _mm(xn, w_ref[:, O0 + h * DH:O0 + (h + 1) * DH])
        _causal_conv_cols(full_s.at[:, cs], cwb_s, cb_ref, uc_s.at[:, cs], cs, T)
        hh, c_new, n_new, m_new = _mlstm_head(
            q, k, v, b_all[:, h:h + 1], a_all[:, h:h + 1], a_t[h:h + 1, :],
            m_s[h:h + 1, 0:1], functools.partial(c_ref.__getitem__, (0, h)), n_ref[0, h:h + 1, :], T)
        c_ref[0, h] = c_new
        n_ref[0, h:h + 1, :] = n_new
        m_s[h:h + 1, :] = jnp.broadcast_to(m_new, (1, LANES))
        ha_s[:, cs] = _head_out(hh, hg_ref[:, cs], o).astype(BF16)
        ga_s[:, cs] = _mm(xn, w_ref[:, GA0 + h * DH:GA0 + (h + 1) * DH])
        gb_s[:, cs] = _mm(xn, w_ref[:, GB0 + h * DH:GB0 + (h + 1) * DH])

    y_a = _mm(ha_s[...], wmo_ref[...])
    tail = full_s[T + CONV_LEAD - (CONV_K - 1):T + CONV_LEAD, :]
    conv_ref[0] = tail
    full_s[CONV_LEAD - (CONV_K - 1):CONV_LEAD, :] = tail

    ucn = _layernorm(uc_s[...], lng_ref[...], lnb_ref[...])
    y_b = _mm((ucn * _sigmoid(ucn)).astype(BF16), wco_ref[...])

    mix = _sigmoid(ga_s[...]) * y_a + _sigmoid(gb_s[...]) * y_b
    x1_ref[0] = x + _mm(mix.astype(BF16), wout_ref[...])

    @pl.when(c == n_chunks - 1)
    def _():
        for h in range(N_HEADS):
            m_ref[0, :, h:h + 1] = m_s[h:h + 1, 0:1]


def _prompt_mixer(x, mixg, w_all, bg, hg, wmo, cw, cb, lng, lnb, wco, wout):
    B, S, D = x.shape
    T = PROMPT_CHUNK
    nc = S // T
    kern = functools.partial(_prompt_mixer_kernel, n_chunks=nc)
    return pl.pallas_call(
        kern,
        grid=(B, nc),
        in_specs=[
            pl.BlockSpec((1, T, D), lambda b, c: (b, c, 0)),
            _const_spec((1, D)), _const_spec((D, N_Z)), _const_spec((1, LANES)),
            _const_spec((1, D)), _const_spec((D, D)), _const_spec((CONV_K, D)), _const_spec((1, D)),
            _const_spec((1, D)), _const_spec((1, D)), _const_spec((D, D)), _const_spec((D, D)),
        ],
        out_specs=[
            pl.BlockSpec((1, T, D), lambda b, c: (b, c, 0)),
            pl.BlockSpec((None, 1, N_HEADS, DH, DH), lambda b, c: (0, b, 0, 0, 0)),
            pl.BlockSpec((None, 1, N_HEADS, DH), lambda b, c: (0, b, 0, 0)),
            pl.BlockSpec((1, 1, N_HEADS), lambda b, c: (b, 0, 0)),
            pl.BlockSpec((None, 1, CONV_K - 1, D), lambda b, c: (0, b, 0, 0)),
        ],
        out_shape=[
            jax.ShapeDtypeStruct((B, S, D), F32),
            jax.ShapeDtypeStruct((1, B, N_HEADS, DH, DH), F32),
            jax.ShapeDtypeStruct((1, B, N_HEADS, DH), F32),
            jax.ShapeDtypeStruct((B, 1, N_HEADS), F32),
            jax.ShapeDtypeStruct((1, B, CONV_K - 1, D), F32),
        ],
        scratch_shapes=[
            pltpu.VMEM((T, D), BF16), pltpu.VMEM((T, D), BF16),
            pltpu.VMEM((T, D), F32), pltpu.VMEM((T, D), F32),
            pltpu.VMEM((T + CONV_LEAD + SUBLANES, D), F32), pltpu.VMEM((T, D), F32),
            pltpu.VMEM((SUBLANES, LANES), F32), pltpu.VMEM((CONV_K, SUBLANES, D), F32),
        ],
        compiler_params=pltpu.CompilerParams(
            dimension_semantics=("arbitrary", "arbitrary"), vmem_limit_bytes=VMEM_LIMIT),
        name="prompt_mixer",
    )(x, mixg, w_all, bg, hg, wmo, cw, cb, lng, lnb, wco, wout)


def _ffn_tail(upc_a, upc_g, wdown_ref, x, fing_ref):
    act = (upc_a * _sigmoid(upc_a) * upc_g).astype(BF16)
    x2 = x + _mm(act, wdown_ref[...])
    return _rmsnorm(x2, fing_ref[...])


def _prompt_ffn_kernel(x_ref, fg_ref, wup_ref, fw_ref, fb_ref, wdown_ref, fing_ref,
                       y_ref, ffn_ref, up_s, act_s):
    T = x_ref.shape[1]
    lo = FFN_LEAD - (FFN_K - 1)
    c = pl.program_id(1)

    @pl.when(c == 0)
    def _():
        up_s[0:FFN_LEAD, :] = jnp.zeros((FFN_LEAD, 2 * D_FF), F32)

    x = x_ref[0]
    hf = _rmsnorm(x, fg_ref[...]).astype(BF16)
    nb = 2 * D_FF // DH
    for blk in range(nb):
        cs = slice(blk * DH, (blk + 1) * DH)
        up_s[FFN_LEAD:FFN_LEAD + T, cs] = _mm(hf, wup_ref[:, cs])

    def conv(cs):
        return (fb_ref[:, cs] + fw_ref[0:1, cs] * up_s[lo:lo + T, cs]
                + fw_ref[1:2, cs] * up_s[lo + 1:lo + 1 + T, cs]
                + fw_ref[2:3, cs] * up_s[lo + 2:lo + 2 + T, cs])

    for blk in range(D_FF // DH):
        ca = slice(blk * DH, (blk + 1) * DH)
        a = conv(ca)
        g = conv(slice(D_FF + blk * DH, D_FF + (blk + 1) * DH))
        act_s[:, ca] = (a * _sigmoid(a) * g).astype(BF16)
    tail = up_s[T + lo:T + FFN_LEAD, :]
    ffn_ref[0] = tail
    up_s[lo:FFN_LEAD, :] = tail
    x2 = x + _mm(act_s[...], wdown_ref[...])
    y_ref[0] = _rmsnorm(x2, fing_ref[...])


def _prompt_ffn(x1, fg, wup, fw, fb, wdown, fing):
    B, S, D = x1.shape
    T = PROMPT_CHUNK
    nc = S // T
    return pl.pallas_call(
        _prompt_ffn_kernel,
        grid=(B, nc),
        in_specs=[
            pl.BlockSpec((1, T, D), lambda b, c: (b, c, 0)),
            _const_spec((1, D)), _const_spec((D, 2 * D_FF)), _const_spec((FFN_K, 2 * D_FF)),
            _const_spec((1, 2 * D_FF)), _const_spec((D_FF, D)), _const_spec((1, D)),
        ],
        out_specs=[
            pl.BlockSpec((1, T, D), lambda b, c: (b, c, 0)),
            pl.BlockSpec((None, 1, FFN_K - 1, 2 * D_FF), lambda b, c: (0, b, 0, 0)),
        ],
        out_shape=[
            jax.ShapeDtypeStruct((B, S, D), F32),
            jax.ShapeDtypeStruct((1, B, FFN_K - 1, 2 * D_FF), F32),
        ],
        scratch_shapes=[pltpu.VMEM((T + FFN_LEAD, 2 * D_FF), F32), pltpu.VMEM((T, D_FF), BF16)],
        compiler_params=pltpu.CompilerParams(
            dimension_semantics=("arbitrary", "arbitrary"), vmem_limit_bytes=VMEM_LIMIT),
        name="prompt_ffn",
    )(x1, fg, wup, fw, fb, wdown, fing)


def _sample_proj_kernel(x_ref, mixg_ref, w_ref, z_ref):
    xn = _rmsnorm(x_ref[...], mixg_ref[...]).astype(BF16)
    for c0 in range(0, N_Z, DH):
        cs = slice(c0, min(c0 + DH, N_Z))
        z_ref[:, cs] = _mm(xn, w_ref[:, cs])


def _sample_proj(x, mixg, w_all):
    R, D = x.shape
    return pl.pallas_call(
        _sample_proj_kernel,
        grid=(R // SAMPLE_ROWS,),
        in_specs=[pl.BlockSpec((SAMPLE_ROWS, D), lambda i: (i, 0)),
                  _const_spec((1, D)), _const_spec((D, N_Z))],
        out_specs=pl.BlockSpec((SAMPLE_ROWS, N_Z), lambda i: (i, 0)),
        out_shape=jax.ShapeDtypeStruct((R, N_Z), F32),
        compiler_params=pltpu.CompilerParams(
            dimension_semantics=("arbitrary",), vmem_limit_bytes=VMEM_LIMIT),
        name="sample_proj",
    )(x, mixg, w_all)


def _sample_mlstm_kernel(zq_ref, zg_ref, bg_ref, c_ref, n_ref, m_ref,
                         h_ref, co_ref, no_ref, mo_ref, seq_s, *, t_seq):
    n_seq = zq_ref.shape[1]
    pad = LANES - SUBLANES
    wq = zq_ref.shape[2]
    seq_s[...] = jnp.zeros_like(seq_s)
    for j in range(n_seq):
        for t in range(t_seq):
            seq_s[j, t:t + 1, 0:wq] = zq_ref[t, j:j + 1, :]
            seq_s[j, t:t + 1, wq:] = zg_ref[t, j:j + 1, :]
    items = []
    for j in range(n_seq):
        zq = seq_s[j, :, 0:wq]
        b_all, a_seq = _gate_columns(seq_s[j, :, wq:], bg_ref[...])
        a_all = jnp.concatenate([a_seq, jnp.zeros((pad, LANES), F32)], axis=0)
        a_t = a_all.T
        for h in range(N_HEADS):
            q = zq[:, Q0 + h * DH:Q0 + (h + 1) * DH]
            k = zq[:, K0 + h * DH:K0 + (h + 1) * DH] * (DH ** -0.5)
            v = zq[:, V0 + h * DH:V0 + (h + 1) * DH]
            zeros = jnp.zeros((pad, DH), F32)
            k = jnp.concatenate([k, zeros], axis=0)
            v = jnp.concatenate([v, zeros], axis=0)
            out = {}
            steps = _mlstm_head_steps(
                q, k, v, b_all[:, h:h + 1], a_all[:, h:h + 1], a_t[h:h + 1, :],
                m_ref[j:j + 1, h:h + 1], functools.partial(c_ref.__getitem__, (j, h)),
                n_ref[j, h:h + 1, :], t_seq, out)
            items.append((j, h, out, steps))
    _round_robin(steps for _, _, _, steps in items)
    for j, h, out, _ in items:
        co_ref[j, h] = out["c"]
        no_ref[j, h:h + 1, :] = out["n"]
        mo_ref[j:j + 1, h:h + 1] = out["m"]
        for t in range(t_seq):
            h_ref[t, j:j + 1, h * DH:(h + 1) * DH] = out["h"][t:t + 1, :]


def _sample_mlstm(z, bg, c0, n0, m0, t_seq):
    nb = c0.shape[1]
    sb = MLSTM_SEQS
    kern = functools.partial(_sample_mlstm_kernel, t_seq=t_seq)
    c_spec = pl.BlockSpec((None, sb, N_HEADS, DH, DH), lambda i: (0, i, 0, 0, 0))
    n_spec = pl.BlockSpec((None, sb, N_HEADS, DH), lambda i: (0, i, 0, 0))
    m_spec = pl.BlockSpec((None, sb, N_HEADS), lambda i: (0, i, 0))
    return pl.pallas_call(
        kern,
        grid=(nb // sb,),
        in_specs=[
            pl.BlockSpec((t_seq, sb, 3 * D_MODEL), lambda i: (0, i, 0)),
            pl.BlockSpec((t_seq, sb, LANES), lambda i: (0, i, IG0 // LANES)),
            _const_spec((1, LANES)),
            c_spec, n_spec, m_spec,
        ],
        out_specs=[pl.BlockSpec((t_seq, sb, D_MODEL), lambda i: (0, i, 0)), c_spec, n_spec, m_spec],
        out_shape=[
            jax.ShapeDtypeStruct((t_seq, nb, D_MODEL), F32),
            jax.ShapeDtypeStruct(c0.shape, F32),
            jax.ShapeDtypeStruct(n0.shape, F32),
            jax.ShapeDtypeStruct(m0.shape, F32),
        ],
        scratch_shapes=[pltpu.VMEM((sb, SUBLANES, 3 * D_MODEL + LANES), F32)],
        compiler_params=pltpu.CompilerParams(
            dimension_semantics=("arbitrary",), vmem_limit_bytes=VMEM_LIMIT),
        name="sample_mlstm",
    )(z, z, bg, c0, n0, m0)


def _rows(ref):
    return jnp.concatenate([ref[t] for t in range(ref.shape[0])], axis=0)


def _sample_mixer_tail_kernel(x_ref, z_ref, h_ref, cst_ref, hg_ref, wmo_ref, cw_ref, cb_ref,
                              lng_ref, lnb_ref, wco_ref, wout_ref,
                              x1_ref, cnew_ref, uc_s):
    t_seq, n_seq, _ = x_ref.shape
    hist = CONV_K - 1
    z = _rows(z_ref)
    hcat = _rows(h_ref)
    ha = jnp.concatenate(
        [_head_out(hcat[:, h * DH:(h + 1) * DH], hg_ref[:, h * DH:(h + 1) * DH],
                   z[:, O0 + h * DH:O0 + (h + 1) * DH]) for h in range(N_HEADS)], axis=1)
    y_a = _mm(ha.astype(BF16), wmo_ref[...])

    for t in range(t_seq):
        uc_s[t] = z_ref[t, :, GV0:GV0 + D_MODEL] * _sigmoid(z_ref[t, :, GG0:GG0 + D_MODEL])

    def plane(r):
        return cst_ref[r] if r < hist else uc_s[r - hist]

    outs = []
    for t in range(t_seq):
        acc = cb_ref[...] + cw_ref[0:1, :] * plane(t)
        for j in range(1, CONV_K):
            acc = acc + cw_ref[j:j + 1, :] * plane(t + j)
        outs.append(acc)
    for r in range(hist):
        cnew_ref[r] = plane(r + t_seq)
    uc = jnp.concatenate(outs, axis=0)

    ucn = _layernorm(uc, lng_ref[...], lnb_ref[...])
    y_b = _mm((ucn * _sigmoid(ucn)).astype(BF16), wco_ref[...])
    mix = _sigmoid(z[:, GA0:GA0 + D_MODEL]) * y_a + _sigmoid(z[:, GB0:GB0 + D_MODEL]) * y_b
    x1 = _rows(x_ref) + _mm(mix.astype(BF16), wout_ref[...])
    for t in range(t_seq):
        x1_ref[t] = x1[t * n_seq:(t + 1) * n_seq, :]


def _sample_mixer_tail(x, z, h, cst, hg, wmo, cw, cb, lng, lnb, wco, wout):
    t_seq, nb, D = x.shape
    sb = SAMPLE_ROWS // t_seq
    hist = CONV_K - 1
    st_spec = pl.BlockSpec((None, hist, sb, D), lambda i: (0, 0, i, 0))
    return pl.pallas_call(
        _sample_mixer_tail_kernel,
        grid=(nb // sb,),
        in_specs=[
            pl.BlockSpec((t_seq, sb, D), lambda i: (0, i, 0)),
            pl.BlockSpec((t_seq, sb, N_Z), lambda i: (0, i, 0)),
            pl.BlockSpec((t_seq, sb, D), lambda i: (0, i, 0)),
            st_spec,
            _const_spec((1, D)), _const_spec((D, D)), _const_spec((CONV_K, D)), _const_spec((1, D)),
            _const_spec((1, D)), _const_spec((1, D)), _const_spec((D, D)), _const_spec((D, D)),
        ],
        out_specs=[pl.BlockSpec((t_seq, sb, D), lambda i: (0, i, 0)), st_spec],
        out_shape=[
            jax.ShapeDtypeStruct((t_seq, nb, D), F32),
            jax.ShapeDtypeStruct(cst.shape, F32),
        ],
        scratch_shapes=[pltpu.VMEM((t_seq, sb, D), F32)],
        compiler_params=pltpu.CompilerParams(
            dimension_semantics=("arbitrary",), vmem_limit_bytes=VMEM_LIMIT),
        name="sample_mixer_tail",
    )(x, z, h, cst, hg, wmo, cw, cb, lng, lnb, wco, wout)


def _sample_ffn_kernel(x_ref, fst_ref, fg_ref, wup_ref, fw_ref, fb_ref, wdown_ref, fing_ref,
                       y_ref, fnew_ref, up_s, upc_s):
    t_seq, n_seq, _ = x_ref.shape
    hist = FFN_K - 1
    x = _rows(x_ref)
    hf = _rmsnorm(x, fg_ref[...]).astype(BF16)
    for blk in range(2 * D_FF // DH):
        cs = slice(blk * DH, (blk + 1) * DH)
        up = _mm(hf, wup_ref[:, cs])
        for t in range(t_seq):
            up_s[hist + t, :, cs] = up[t * n_seq:(t + 1) * n_seq, :]
    for j in range(n_seq):
        for r in range(hist):
            up_s[r, j:j + 1, :] = fst_ref[j, r:r + 1, :]
    for t in range(t_seq):
        acc = fb_ref[...] + fw_ref[0:1, :] * up_s[t]
        for kk in range(1, FFN_K):
            acc = acc + fw_ref[kk:kk + 1, :] * up_s[t + kk]
        upc_s[t] = acc
    for j in range(n_seq):
        for r in range(hist):
            fnew_ref[j, r:r + 1, :] = up_s[t_seq + r, j:j + 1, :]
    upc = _rows(upc_s)
    y = _ffn_tail(upc[:, 0:D_FF], upc[:, D_FF:2 * D_FF], wdown_ref, x, fing_ref)
    for t in range(t_seq):
        y_ref[t] = y[t * n_seq:(t + 1) * n_seq, :]


def _sample_ffn(x1, fst, fg, wup, fw, fb, wdown, fing):
    t_seq, nb, D = x1.shape
    sb = SAMPLE_ROWS // t_seq
    hist = FFN_K - 1
    st_spec = pl.BlockSpec((None, sb, hist, 2 * D_FF), lambda i: (0, i, 0, 0))
    return pl.pallas_call(
        _sample_ffn_kernel,
        grid=(nb // sb,),
        in_specs=[
            pl.BlockSpec((t_seq, sb, D), lambda i: (0, i, 0)),
            st_spec,
            _const_spec((1, D)), _const_spec((D, 2 * D_FF)), _const_spec((FFN_K, 2 * D_FF)),
            _const_spec((1, 2 * D_FF)), _const_spec((D_FF, D)), _const_spec((1, D)),
        ],
        out_specs=[pl.BlockSpec((t_seq, sb, D), lambda i: (0, i, 0)), st_spec],
        out_shape=[
            jax.ShapeDtypeStruct((t_seq, nb, D), F32),
            jax.ShapeDtypeStruct(fst.shape, F32),
        ],
        scratch_shapes=[pltpu.VMEM((t_seq + hist, sb, 2 * D_FF), F32), pltpu.VMEM((t_seq, sb, 2 * D_FF), F32)],
        compiler_params=pltpu.CompilerParams(
            dimension_semantics=("arbitrary",), vmem_limit_bytes=VMEM_LIMIT),
        name="sample_ffn",
    )(x1, fst, fg, wup, fw, fb, wdown, fing)


def _pad_gate_bias(b):
    return jnp.pad(b, (0, LANES - b.shape[0])).reshape(1, LANES)


def kernel(x_prompt, x_sample, state_C, state_n, state_m, state_conv, state_ffn, mix_norm_g, w_in, b_if,
           head_norm_g, w_mlstm_out, conv_w, conv_b, conv_ln_g, conv_ln_b, w_conv_out, w_out, ffn_norm_g,
           w_up, ffn_conv_w, ffn_conv_b, w_down, final_norm_g):
    depth = w_in.shape[0]
    assert depth == 1, "single-layer trunk"
    l = 0
    row = lambda a: a.reshape(1, -1)
    w_all = _pack_w_in(jnp.swapaxes(w_in, 1, 2))
    bg = _pad_gate_bias(b_if[l])
    mixg, hg = row(mix_norm_g[l]), row(head_norm_g[l])
    wmo, wco, wout = (w.astype(BF16) for w in (w_mlstm_out[l], w_conv_out[l], w_out[l]))
    cw, cb = conv_w[l], row(conv_b[l])
    lng, lnb = row(conv_ln_g[l]), row(conv_ln_b[l])
    fg, fing = row(ffn_norm_g[l]), row(final_norm_g)
    wup, wdown = w_up[l].astype(BF16), w_down[l].astype(BF16)
    fw, fb = ffn_conv_w[l], row(ffn_conv_b[l])

    x1p, c_p, n_p, m_p, conv_p = _prompt_mixer(x_prompt, mixg, w_all, bg, hg, wmo, cw, cb, lng, lnb, wco, wout)
    y_p, ffn_p = _prompt_ffn(x1p, fg, wup, fw, fb, wdown, fing)

    nb, t_seq, d = x_sample.shape
    xs = jnp.swapaxes(x_sample, 0, 1)
    z = _sample_proj(xs.reshape(t_seq * nb, d), mixg, w_all).reshape(t_seq, nb, N_Z)
    h_s, c_s, n_s, m_s = _sample_mlstm(z, bg, state_C, state_n, state_m, t_seq)
    x1s, conv_t = _sample_mixer_tail(xs, z, h_s, jnp.swapaxes(state_conv, 1, 2), hg, wmo, cw, cb, lng, lnb,
                                     wco, wout)
    conv_s = jnp.swapaxes(conv_t, 1, 2)
    y_t, ffn_s = _sample_ffn(x1s, state_ffn, fg, wup, fw, fb, wdown, fing)
    y_s = jnp.swapaxes(y_t, 0, 1)

    return (y_p, y_s,
            c_p, n_p, m_p.reshape(1, -1, N_HEADS), conv_p, ffn_p,
            c_s, n_s, m_s, conv_s, ffn_s)
```

```python
import functools

import jax
import jax.numpy as jnp
from jax import lax
from jax.experimental import pallas as pl
from jax.experimental.pallas import tpu as pltpu

F32 = jnp.float32
BF16 = jnp.bfloat16

D_MODEL = 1024
N_HEADS = 4
DH = D_MODEL // N_HEADS
CONV_K = 31
D_FF = 2816
FFN_K = 3
EPS = 1e-6

LANES = 128
Q0, K0, V0, O0, GV0, GG0, GA0, GB0 = (i * D_MODEL for i in range(8))
IG0 = 8 * D_MODEL
N_Z = IG0 + LANES

SUBLANES = 8
CONV_LEAD = 32
FFN_LEAD = 8
PROMPT_CHUNK = 256
FFN_CHUNK = 512
SAMPLE_ROWS = 128
MLSTM_SEQS = 8
PACK_ROWS = 128
VMEM_LIMIT = 58 * 1024 * 1024


def _mm(a, b):
    return jnp.dot(a, b, preferred_element_type=F32)


def _mm_nt(a, b):
    return lax.dot_general(a, b, (((1,), (1,)), ((), ())), preferred_element_type=F32)


def _mm_tn(a, b):
    return lax.dot_general(a, b, (((0,), (0,)), ((), ())), preferred_element_type=F32)


def _rmsnorm(x, g):
    return x * lax.rsqrt(jnp.mean(x * x, axis=-1, keepdims=True) + EPS) * g


def _layernorm(x, g, b):
    mu = jnp.mean(x, axis=-1, keepdims=True)
    xc = x - mu
    return xc * lax.rsqrt(jnp.mean(xc * xc, axis=-1, keepdims=True) + EPS) * g + b


def _sigmoid(x):
    return 0.5 * jnp.tanh(0.5 * x) + 0.5


def _log_sigmoid(x):
    return jnp.minimum(x, 0.0) - jnp.log1p(jnp.exp(-jnp.abs(x)))


def _cumsum_rows(x):
    n = x.shape[0]
    row = lax.broadcasted_iota(jnp.int32, x.shape, 0)
    s = 1
    while s < n:
        x = x + jnp.where(row >= s, pltpu.roll(x, s, axis=0), 0.0)
        s *= 2
    return x


def _gate_columns(zg, bias):
    zgb = zg + bias
    b_all = pltpu.roll(_cumsum_rows(_log_sigmoid(zgb)), LANES - N_HEADS, axis=1)
    return b_all, zgb - b_all


def _mlstm_head_steps(q, k, v, b_col, a_col, a_row, m_prev, c_prev, n_prev, n_valid, out):
    L, LK = q.shape[0], k.shape[0]
    row = lax.broadcasted_iota(jnp.int32, (L, LK), 0)
    col = lax.broadcasted_iota(jnp.int32, (L, LK), 1)
    d = jnp.where(col <= row, b_col + a_row, -jnp.inf)
    inter = b_col + m_prev
    d_max = jnp.max(d, axis=1, keepdims=True)
    qb, kb, vb = q.astype(BF16), k.astype(BF16), v.astype(BF16)
    qk = _mm_nt(qb, kb)
    yield
    m_t = jnp.maximum(inter, d_max)
    w_inter = jnp.exp(inter - m_t)
    p = jnp.exp(d - m_t)
    s = qk * p
    qc = _mm_nt(qb, c_prev().astype(BF16))
    yield
    num = w_inter * qc + _mm(s.astype(BF16), vb)
    den = w_inter * jnp.sum(q * n_prev, axis=1, keepdims=True) + jnp.sum(s, axis=1, keepdims=True)
    m_new = m_t[n_valid - 1:n_valid, :]
    b_last = b_col[n_valid - 1:n_valid, :]
    g_inter = jnp.exp(b_last + m_prev - m_new)
    g_col = jnp.exp(b_last + a_col - m_new)
    if n_valid < LK:
        g_col = jnp.where(lax.broadcasted_iota(jnp.int32, (LK, 1), 0) < n_valid, g_col, 0.0)
    yield
    out["h"] = num * (1.0 / jnp.maximum(jnp.abs(den), jnp.exp(-m_t)))
    out["c"] = g_inter * c_prev() + _mm_tn((g_col * v).astype(BF16), kb)
    out["n"] = g_inter * n_prev + jnp.sum(g_col * k, axis=0, keepdims=True)
    out["m"] = m_new
    yield


def _mlstm_head(q, k, v, b_col, a_col, a_row, m_prev, c_prev, n_prev, n_valid):
    out = {}
    for _ in _mlstm_head_steps(q, k, v, b_col, a_col, a_row, m_prev, c_prev, n_prev, n_valid, out):
        pass
    return out["h"], out["c"], out["n"], out["m"]


def _round_robin(streams):
    streams = list(streams)
    while streams:
        alive = []
        for gen in streams:
            try:
                next(gen)
                alive.append(gen)
            except StopIteration:
                pass
        streams = alive


def _head_out(h, head_g, o):
    mu = jnp.mean(h, axis=-1, keepdims=True)
    hc = h - mu
    hn = hc * lax.rsqrt(jnp.mean(hc * hc, axis=-1, keepdims=True) + EPS)
    return hn * head_g * _sigmoid(o)


def _causal_conv_cols(full_v, cwb_s, cb_ref, out_v, cs, T):
    off = CONV_LEAD - (CONV_K - 1)
    n_a = (off + CONV_K - 1) // SUBLANES + 1
    width = cs.stop - cs.start
    sub = lax.broadcasted_iota(jnp.int32, (SUBLANES, width), 0)
    bias = cb_ref[:, cs]
    tiles = {}

    def tile(i):
        if i not in tiles:
            tiles[i] = full_v[SUBLANES * i:SUBLANES * (i + 1), :]
        return tiles[i]

    prev = None
    for i in range(T // SUBLANES + 1):
        ys = []
        for r in range(SUBLANES):
            acc = None
            for a in range(n_a):
                j = SUBLANES * a + r - off
                if 0 <= j < CONV_K:
                    term = cwb_s[j, :, cs] * tile(i + a)
                    acc = term if acc is None else acc + term
            ys.append(acc)
        tiles.pop(i, None)
        cur = [ys[0]] + [pltpu.roll(ys[r], SUBLANES - r, axis=0) for r in range(1, SUBLANES)]
        if prev is not None:
            out = prev[0] + bias
            for r in range(1, SUBLANES):
                out = out + jnp.where(sub < SUBLANES - r, prev[r], cur[r])
            out_v[SUBLANES * (i - 1):SUBLANES * i, :] = out
        prev = cur


def _const_spec(shape):
    nd = len(shape)
    return pl.BlockSpec(shape, lambda *_: (0,) * nd, pipeline_mode=pl.Buffered(1))


def _pack_kernel(w_ref, o_ref):
    split = 4 * D_MODEL
    n_gate = 2 * N_HEADS
    rows = w_ref.shape[1]
    for src, dst in ((0, 0), (split + n_gate, split)):
        for t in range(split // LANES):
            blk = w_ref[src + t * LANES:src + (t + 1) * LANES, :]
            o_ref[:, dst + t * LANES:dst + (t + 1) * LANES] = blk.T.astype(BF16)
    g = jnp.concatenate([w_ref[split:split + n_gate, :], jnp.zeros((LANES - n_gate, rows), F32)], axis=0).T
    o_ref[:, IG0:IG0 + LANES] = g.astype(BF16)


def _pack_w_in(w_t):
    _, n_in, d = w_t.shape
    return pl.pallas_call(
        _pack_kernel,
        grid=(d // PACK_ROWS,),
        in_specs=[pl.BlockSpec((None, n_in, PACK_ROWS), lambda i: (0, 0, i))],
        out_specs=pl.BlockSpec((PACK_ROWS, N_Z), lambda i: (i, 0)),
        out_shape=jax.ShapeDtypeStruct((d, N_Z), BF16),
        compiler_params=pltpu.CompilerParams(
            dimension_semantics=("arbitrary",), vmem_limit_bytes=VMEM_LIMIT),
        name="pack_w_in",
    )(w_t)


def _prompt_mixer_kernel(x_ref, mixg_ref, w_ref, bg_ref, hg_ref, wmo_ref, cw_ref, cb_ref,
                         lng_ref, lnb_ref, wco_ref, wout_ref,
                         x1_ref, c_ref, n_ref, m_ref, conv_ref,
                         xn_s, ha_s, ga_s, gb_s, full_s, uc_s, m_s, cwb_s, *, n_chunks):
    T = x_ref.shape[1]
    c = pl.program_id(1)

    @pl.when(c == 0)
    def _():
        c_ref[...] = jnp.zeros_like(c_ref)
        n_ref[...] = jnp.zeros_like(n_ref)
        m_s[...] = jnp.zeros_like(m_s)
        full_s[0:CONV_LEAD, :] = jnp.zeros((CONV_LEAD, D_MODEL), F32)
        full_s[CONV_LEAD + T:CONV_LEAD + T + SUBLANES, :] = jnp.zeros((SUBLANES, D_MODEL), F32)
        for j in range(CONV_K):
            cwb_s[j] = jnp.broadcast_to(cw_ref[j:j + 1, :], (SUBLANES, D_MODEL))

    x = x_ref[0]
    xn_s[...] = _rmsnorm(x, mixg_ref[...]).astype(BF16)
    xn = xn_s[...]

    b_all, a_all = _gate_columns(_mm(xn, w_ref[:, IG0:IG0 + LANES]), bg_ref[...])
    a_t = a_all.T

    for h in range(N_HEADS):
        cs = slice(h * DH, (h + 1) * DH)
        gv = _mm(xn, w_ref[:, GV0 + h * DH:GV0 + (h + 1) * DH])
        gg = _mm(xn, w_ref[:, GG0 + h * DH:GG0 + (h + 1) * DH])
        full_s[CONV_LEAD:CONV_LEAD + T, cs] = gv * _sigmoid(gg)
        q = _mm(xn, w_ref[:, Q0 + h * DH:Q0 + (h + 1) * DH])
        k = _mm(xn, w_ref[:, K0 + h * DH:K0 + (h + 1) * DH]) * (DH ** -0.5)
        v = _mm(xn, w_ref[:, V0 + h * DH:V0 + (h + 1) * DH])
        o = _mm(xn, w_ref[:, O0 + h * DH:O0 + (h + 1) * DH])
        _causal_conv_cols(full_s.at[:, cs], cwb_s, cb_ref, uc_s.at[:, cs], cs, T)
        hh, c_new, n_new, m_new = _mlstm_head(
            q, k, v, b_all[:, h:h + 1], a_all[:, h:h + 1], a_t[h:h + 1, :],
            m_s[h:h + 1, 0:1], functools.partial(c_ref.__getitem__, (0, h)), n_ref[0, h:h + 1, :], T)
        c_ref[0, h] = c_new
        n_ref[0, h:h + 1, :] = n_new
        m_s[h:h + 1, :] = jnp.broadcast_to(m_new, (1, LANES))
        ha_s[:, cs] = _head_out(hh, hg_ref[:, cs], o).astype(BF16)
        ga_s[:, cs] = _mm(xn, w_ref[:, GA0 + h * DH:GA0 + (h + 1) * DH])
        gb_s[:, cs] = _mm(xn, w_ref[:, GB0 + h * DH:GB0 + (h + 1) * DH])

    y_a = _mm(ha_s[...], wmo_ref[...])
    tail = full_s[T + CONV_LEAD - (CONV_K - 1):T + CONV_LEAD, :]
    conv_ref[0] = tail
    full_s[CONV_LEAD - (CONV_K - 1):CONV_LEAD, :] = tail

    ucn = _layernorm(uc_s[...], lng_ref[...], lnb_ref[...])
    y_b = _mm((ucn * _sigmoid(ucn)).astype(BF16), wco_ref[...])

    mix = _sigmoid(ga_s[...]) * y_a + _sigmoid(gb_s[...]) * y_b
    x1_ref[0] = x + _mm(mix.astype(BF16), wout_ref[...])

    @pl.when(c == n_chunks - 1)
    def _():
        for h in range(N_HEADS):
            m_ref[0, :, h:h + 1] = m_s[h:h + 1, 0:1]


def _prompt_mixer(x, mixg, w_all, bg, hg, wmo, cw, cb, lng, lnb, wco, wout):
    B, S, D = x.shape
    T = PROMPT_CHUNK
    nc = S // T
    kern = functools.partial(_prompt_mixer_kernel, n_chunks=nc)
    return pl.pallas_call(
        kern,
        grid=(B, nc),
        in_specs=[
            pl.BlockSpec((1, T, D), lambda b, c: (b, c, 0)),
            _const_spec((1, D)), _const_spec((D, N_Z)), _const_spec((1, LANES)),
            _const_spec((1, D)), _const_spec((D, D)), _const_spec((CONV_K, D)), _const_spec((1, D)),
            _const_spec((1, D)), _const_spec((1, D)), _const_spec((D, D)), _const_spec((D, D)),
        ],
        out_specs=[
            pl.BlockSpec((1, T, D), lambda b, c: (b, c, 0)),
            pl.BlockSpec((None, 1, N_HEADS, DH, DH), lambda b, c: (0, b, 0, 0, 0)),
            pl.BlockSpec((None, 1, N_HEADS, DH), lambda b, c: (0, b, 0, 0)),
            pl.BlockSpec((1, 1, N_HEADS), lambda b, c: (b, 0, 0)),
            pl.BlockSpec((None, 1, CONV_K - 1, D), lambda b, c: (0, b, 0, 0)),
        ],
        out_shape=[
            jax.ShapeDtypeStruct((B, S, D), F32),
            jax.ShapeDtypeStruct((1, B, N_HEADS, DH, DH), F32),
            jax.ShapeDtypeStruct((1, B, N_HEADS, DH), F32),
            jax.ShapeDtypeStruct((B, 1, N_HEADS), F32),
            jax.ShapeDtypeStruct((1, B, CONV_K - 1, D), F32),
        ],
        scratch_shapes=[
            pltpu.VMEM((T, D), BF16), pltpu.VMEM((T, D), BF16),
            pltpu.VMEM((T, D), F32), pltpu.VMEM((T, D), F32),
            pltpu.VMEM((T + CONV_LEAD + SUBLANES, D), F32), pltpu.VMEM((T, D), F32),
            pltpu.VMEM((SUBLANES, LANES), F32), pltpu.VMEM((CONV_K, SUBLANES, D), F32),
        ],
        compiler_params=pltpu.CompilerParams(
            dimension_semantics=("arbitrary", "arbitrary"), vmem_limit_bytes=VMEM_LIMIT),
        name="prompt_mixer",
    )(x, mixg, w_all, bg, hg, wmo, cw, cb, lng, lnb, wco, wout)


def _ffn_tail(upc_a, upc_g, wdown_ref, x, fing_ref):
    act = (upc_a * _sigmoid(upc_a) * upc_g).astype(BF16)
    x2 = x + _mm(act, wdown_ref[...])
    return _rmsnorm(x2, fing_ref[...])


def _prompt_ffn_kernel(x_ref, fg_ref, wup_ref, fw_ref, fb_ref, wdown_ref, fing_ref,
                       y_ref, ffn_ref, up_s, act_s):
    T = x_ref.shape[1]
    lo = FFN_LEAD - (FFN_K - 1)
    c = pl.program_id(1)

    @pl.when(c == 0)
    def _():
        up_s[0:FFN_LEAD, :] = jnp.zeros((FFN_LEAD, 2 * D_FF), F32)

    x = x_ref[0]
    hf = _rmsnorm(x, fg_ref[...]).astype(BF16)
    nb = 2 * D_FF // DH
    for blk in range(nb):
        cs = slice(blk * DH, (blk + 1) * DH)
        up_s[FFN_LEAD:FFN_LEAD + T, cs] = _mm(hf, wup_ref[:, cs])

    def conv(cs):
        return (fb_ref[:, cs] + fw_ref[0:1, cs] * up_s[lo:lo + T, cs]
                + fw_ref[1:2, cs] * up_s[lo + 1:lo + 1 + T, cs]
                + fw_ref[2:3, cs] * up_s[lo + 2:lo + 2 + T, cs])

    for blk in range(D_FF // DH):
        ca = slice(blk * DH, (blk + 1) * DH)
        a = conv(ca)
        g = conv(slice(D_FF + blk * DH, D_FF + (blk + 1) * DH))
        act_s[:, ca] = (a * _sigmoid(a) * g).astype(BF16)
    tail = up_s[T + lo:T + FFN_LEAD, :]
    ffn_ref[0] = tail
    up_s[lo:FFN_LEAD, :] = tail
    x2 = x + _mm(act_s[...], wdown_ref[...])
    y_ref[0] = _rmsnorm(x2, fing_ref[...])


def _prompt_ffn(x1, fg, wup, fw, fb, wdown, fing):
    B, S, D = x1.shape
    T = FFN_CHUNK
    nc = S // T
    return pl.pallas_call(
        _prompt_ffn_kernel,
        grid=(B, nc),
        in_specs=[
            pl.BlockSpec((1, T, D), lambda b, c: (b, c, 0)),
            _const_spec((1, D)), _const_spec((D, 2 * D_FF)), _const_spec((FFN_K, 2 * D_FF)),
            _const_spec((1, 2 * D_FF)), _const_spec((D_FF, D)), _const_spec((1, D)),
        ],
        out_specs=[
            pl.BlockSpec((1, T, D), lambda b, c: (b, c, 0)),
            pl.BlockSpec((None, 1, FFN_K - 1, 2 * D_FF), lambda b, c: (0, b, 0, 0)),
        ],
        out_shape=[
            jax.ShapeDtypeStruct((B, S, D), F32),
            jax.ShapeDtypeStruct((1, B, FFN_K - 1, 2 * D_FF), F32),
        ],
        scratch_shapes=[pltpu.VMEM((T + FFN_LEAD, 2 * D_FF), F32), pltpu.VMEM((T, D_FF), BF16)],
        compiler_params=pltpu.CompilerParams(
            dimension_semantics=("arbitrary", "arbitrary"), vmem_limit_bytes=VMEM_LIMIT),
        name="prompt_ffn",
    )(x1, fg, wup, fw, fb, wdown, fing)


def _sample_proj_kernel(x_ref, mixg_ref, w_ref, z_ref):
    xn = _rmsnorm(x_ref[...], mixg_ref[...]).astype(BF16)
    for c0 in range(0, N_Z, DH):
        cs = slice(c0, min(c0 + DH, N_Z))
        z_ref[:, cs] = _mm(xn, w_ref[:, cs])


def _sample_proj(x, mixg, w_all):
    R, D = x.shape
    return pl.pallas_call(
        _sample_proj_kernel,
        grid=(R // SAMPLE_ROWS,),
        in_specs=[pl.BlockSpec((SAMPLE_ROWS, D), lambda i: (i, 0)),
                  _const_spec((1, D)), _const_spec((D, N_Z))],
        out_specs=pl.BlockSpec((SAMPLE_ROWS, N_Z), lambda i: (i, 0)),
        out_shape=jax.ShapeDtypeStruct((R, N_Z), F32),
        compiler_params=pltpu.CompilerParams(
            dimension_semantics=("arbitrary",), vmem_limit_bytes=VMEM_LIMIT),
        name="sample_proj",
    )(x, mixg, w_all)


def _sample_mlstm_kernel(zq_ref, zg_ref, bg_ref, c_ref, n_ref, m_ref,
                         h_ref, co_ref, no_ref, mo_ref, seq_s, *, t_seq):
    n_seq = zq_ref.shape[1]
    pad = LANES - SUBLANES
    wq = zq_ref.shape[2]
    seq_s[...] = jnp.zeros_like(seq_s)
    for j in range(n_seq):
        for t in range(t_seq):
            seq_s[j, t:t + 1, 0:wq] = zq_ref[t, j:j + 1, :]
            seq_s[j, t:t + 1, wq:] = zg_ref[t, j:j + 1, :]
    items = []
    for j in range(n_seq):
        zq = seq_s[j, :, 0:wq]
        b_all, a_seq = _gate_columns(seq_s[j, :, wq:], bg_ref[...])
        a_all = jnp.concatenate([a_seq, jnp.zeros((pad, LANES), F32)], axis=0)
        a_t = a_all.T
        for h in range(N_HEADS):
            q = zq[:, Q0 + h * DH:Q0 + (h + 1) * DH]
            k = zq[:, K0 + h * DH:K0 + (h + 1) * DH] * (DH ** -0.5)
            v = zq[:, V0 + h * DH:V0 + (h + 1) * DH]
            zeros = jnp.zeros((pad, DH), F32)
            k = jnp.concatenate([k, zeros], axis=0)
            v = jnp.concatenate([v, zeros], axis=0)
            out = {}
            steps = _mlstm_head_steps(
                q, k, v, b_all[:, h:h + 1], a_all[:, h:h + 1], a_t[h:h + 1, :],
                m_ref[j:j + 1, h:h + 1], functools.partial(c_ref.__getitem__, (j, h)),
                n_ref[j, h:h + 1, :], t_seq, out)
            items.append((j, h, out, steps))
    _round_robin(steps for _, _, _, steps in items)
    for j, h, out, _ in items:
        co_ref[j, h] = out["c"]
        no_ref[j, h:h + 1, :] = out["n"]
        mo_ref[j:j + 1, h:h + 1] = out["m"]
        for t in range(t_seq):
            h_ref[t, j:j + 1, h * DH:(h + 1) * DH] = out["h"][t:t + 1, :]


def _sample_mlstm(z, bg, c0, n0, m0, t_seq):
    nb = c0.shape[1]
    sb = MLSTM_SEQS
    kern = functools.partial(_sample_mlstm_kernel, t_seq=t_seq)
    c_spec = pl.BlockSpec((None, sb, N_HEADS, DH, DH), lambda i: (0, i, 0, 0, 0))
    n_spec = pl.BlockSpec((None, sb, N_HEADS, DH), lambda i: (0, i, 0, 0))
    m_spec = pl.BlockSpec((None, sb, N_HEADS), lambda i: (0, i, 0))
    return pl.pallas_call(
        kern,
        grid=(nb // sb,),
        in_specs=[
            pl.BlockSpec((t_seq, sb, 3 * D_MODEL), lambda i: (0, i, 0)),
            pl.BlockSpec((t_seq, sb, LANES), lambda i: (0, i, IG0 // LANES)),
            _const_spec((1, LANES)),
            c_spec, n_spec, m_spec,
        ],
        out_specs=[pl.BlockSpec((t_seq, sb, D_MODEL), lambda i: (0, i, 0)), c_spec, n_spec, m_spec],
        out_shape=[
            jax.ShapeDtypeStruct((t_seq, nb, D_MODEL), F32),
            jax.ShapeDtypeStruct(c0.shape, F32),
            jax.ShapeDtypeStruct(n0.shape, F32),
            jax.ShapeDtypeStruct(m0.shape, F32),
        ],
        scratch_shapes=[pltpu.VMEM((sb, SUBLANES, 3 * D_MODEL + LANES), F32)],
        compiler_params=pltpu.CompilerParams(
            dimension_semantics=("arbitrary",), vmem_limit_bytes=VMEM_LIMIT),
        name="sample_mlstm",
    )(z, z, bg, c0, n0, m0)


def _rows(ref):
    return jnp.concatenate([ref[t] for t in range(ref.shape[0])], axis=0)


def _sample_mixer_tail_kernel(x_ref, z_ref, h_ref, cst_ref, hg_ref, wmo_ref, cw_ref, cb_ref,
                              lng_ref, lnb_ref, wco_ref, wout_ref,
                              x1_ref, cnew_ref, uc_s):
    t_seq, n_seq, _ = x_ref.shape
    hist = CONV_K - 1
    z = _rows(z_ref)
    hcat = _rows(h_ref)
    ha = jnp.concatenate(
        [_head_out(hcat[:, h * DH:(h + 1) * DH], hg_ref[:, h * DH:(h + 1) * DH],
                   z[:, O0 + h * DH:O0 + (h + 1) * DH]) for h in range(N_HEADS)], axis=1)
    y_a = _mm(ha.astype(BF16), wmo_ref[...])

    for t in range(t_seq):
        uc_s[t] = z_ref[t, :, GV0:GV0 + D_MODEL] * _sigmoid(z_ref[t, :, GG0:GG0 + D_MODEL])

    def plane(r):
        return cst_ref[r] if r < hist else uc_s[r - hist]

    outs = []
    for t in range(t_seq):
        acc = cb_ref[...] + cw_ref[0:1, :] * plane(t)
        for j in range(1, CONV_K):
            acc = acc + cw_ref[j:j + 1, :] * plane(t + j)
        outs.append(acc)
    for r in range(hist):
        cnew_ref[r] = plane(r + t_seq)
    uc = jnp.concatenate(outs, axis=0)

    ucn = _layernorm(uc, lng_ref[...], lnb_ref[...])
    y_b = _mm((ucn * _sigmoid(ucn)).astype(BF16), wco_ref[...])
    mix = _sigmoid(z[:, GA0:GA0 + D_MODEL]) * y_a + _sigmoid(z[:, GB0:GB0 + D_MODEL]) * y_b
    x1 = _rows(x_ref) + _mm(mix.astype(BF16), wout_ref[...])
    for t in range(t_seq):
        x1_ref[t] = x1[t * n_seq:(t + 1) * n_seq, :]


def _sample_mixer_tail(x, z, h, cst, hg, wmo, cw, cb, lng, lnb, wco, wout):
    t_seq, nb, D = x.shape
    sb = SAMPLE_ROWS // t_seq
    hist = CONV_K - 1
    st_spec = pl.BlockSpec((None, hist, sb, D), lambda i: (0, 0, i, 0))
    return pl.pallas_call(
        _sample_mixer_tail_kernel,
        grid=(nb // sb,),
        in_specs=[
            pl.BlockSpec((t_seq, sb, D), lambda i: (0, i, 0)),
            pl.BlockSpec((t_seq, sb, N_Z), lambda i: (0, i, 0)),
            pl.BlockSpec((t_seq, sb, D), lambda i: (0, i, 0)),
            st_spec,
            _const_spec((1, D)), _const_spec((D, D)), _const_spec((CONV_K, D)), _const_spec((1, D)),
            _const_spec((1, D)), _const_spec((1, D)), _const_spec((D, D)), _const_spec((D, D)),
        ],
        out_specs=[pl.BlockSpec((t_seq, sb, D), lambda i: (0, i, 0)), st_spec],
        out_shape=[
            jax.ShapeDtypeStruct((t_seq, nb, D), F32),
            jax.ShapeDtypeStruct(cst.shape, F32),
        ],
        scratch_shapes=[pltpu.VMEM((t_seq, sb, D), F32)],
        compiler_params=pltpu.CompilerParams(
            dimension_semantics=("arbitrary",), vmem_limit_bytes=VMEM_LIMIT),
        name="sample_mixer_tail",
    )(x, z, h, cst, hg, wmo, cw, cb, lng, lnb, wco, wout)


def _sample_ffn_kernel(x_ref, fst_ref, fg_ref, wup_ref, fw_ref, fb_ref, wdown_ref, fing_ref,
                       y_ref, fnew_ref, up_s, upc_s):
    t_seq, n_seq, _ = x_ref.shape
    hist = FFN_K - 1
    x = _rows(x_ref)
    hf = _rmsnorm(x, fg_ref[...]).astype(BF16)
    for blk in range(2 * D_FF // DH):
        cs = slice(blk * DH, (blk + 1) * DH)
        up = _mm(hf, wup_ref[:, cs])
        for t in range(t_seq):
            up_s[hist + t, :, cs] = up[t * n_seq:(t + 1) * n_seq, :]
    for j in range(n_seq):
        for r in range(hist):
            up_s[r, j:j + 1, :] = fst_ref[j, r:r + 1, :]
    for t in range(t_seq):
        acc = fb_ref[...] + fw_ref[0:1, :] * up_s[t]
        for kk in range(1, FFN_K):
            acc = acc + fw_ref[kk:kk + 1, :] * up_s[t + kk]
        upc_s[t] = acc
    for j in range(n_seq):
        for r in range(hist):
            fnew_ref[j, r:r + 1, :] = up_s[t_seq + r, j:j + 1, :]
    upc = _rows(upc_s)
    y = _ffn_tail(upc[:, 0:D_FF], upc[:, D_FF:2 * D_FF], wdown_ref, x, fing_ref)
    for t in range(t_seq):
        y_ref[t] = y[t * n_seq:(t + 1) * n_seq, :]


def _sample_ffn(x1, fst, fg, wup, fw, fb, wdown, fing):
    t_seq, nb, D = x1.shape
    sb = SAMPLE_ROWS // t_seq
    hist = FFN_K - 1
    st_spec = pl.BlockSpec((None, sb, hist, 2 * D_FF), lambda i: (0, i, 0, 0))
    return pl.pallas_call(
        _sample_ffn_kernel,
        grid=(nb // sb,),
        in_specs=[
            pl.BlockSpec((t_seq, sb, D), lambda i: (0, i, 0)),
            st_spec,
            _const_spec((1, D)), _const_spec((D, 2 * D_FF)), _const_spec((FFN_K, 2 * D_FF)),
            _const_spec((1, 2 * D_FF)), _const_spec((D_FF, D)), _const_spec((1, D)),
        ],
        out_specs=[pl.BlockSpec((t_seq, sb, D), lambda i: (0, i, 0)), st_spec],
        out_shape=[
            jax.ShapeDtypeStruct((t_seq, nb, D), F32),
            jax.ShapeDtypeStruct(fst.shape, F32),
        ],
        scratch_shapes=[pltpu.VMEM((t_seq + hist, sb, 2 * D_FF), F32), pltpu.VMEM((t_seq, sb, 2 * D_FF), F32)],
        compiler_params=pltpu.CompilerParams(
            dimension_semantics=("arbitrary",), vmem_limit_bytes=VMEM_LIMIT),
        name="sample_ffn",
    )(x1, fst, fg, wup, fw, fb, wdown, fing)


def _pad_gate_bias(b):
    return jnp.pad(b, (0, LANES - b.shape[0])).reshape(1, LANES)


def kernel(x_prompt, x_sample, state_C, state_n, state_m, state_conv, state_ffn, mix_norm_g, w_in, b_if,
           head_norm_g, w_mlstm_out, conv_w, conv_b, conv_ln_g, conv_ln_b, w_conv_out, w_out, ffn_norm_g,
           w_up, ffn_conv_w, ffn_conv_b, w_down, final_norm_g):
    depth = w_in.shape[0]
    assert depth == 1, "single-layer trunk"
    l = 0
    row = lambda a: a.reshape(1, -1)
    w_all = _pack_w_in(jnp.swapaxes(w_in, 1, 2))
    bg = _pad_gate_bias(b_if[l])
    mixg, hg = row(mix_norm_g[l]), row(head_norm_g[l])
    wmo, wco, wout = (w.astype(BF16) for w in (w_mlstm_out[l], w_conv_out[l], w_out[l]))
    cw, cb = conv_w[l], row(conv_b[l])
    lng, lnb = row(conv_ln_g[l]), row(conv_ln_b[l])
    fg, fing = row(ffn_norm_g[l]), row(final_norm_g)
    wup, wdown = w_up[l].astype(BF16), w_down[l].astype(BF16)
    fw, fb = ffn_conv_w[l], row(ffn_conv_b[l])

    x1p, c_p, n_p, m_p, conv_p = _prompt_mixer(x_prompt, mixg, w_all, bg, hg, wmo, cw, cb, lng, lnb, wco, wout)
    y_p, ffn_p = _prompt_ffn(x1p, fg, wup, fw, fb, wdown, fing)

    nb, t_seq, d = x_sample.shape
    xs = jnp.swapaxes(x_sample, 0, 1)
    z = _sample_proj(xs.reshape(t_seq * nb, d), mixg, w_all).reshape(t_seq, nb, N_Z)
    h_s, c_s, n_s, m_s = _sample_mlstm(z, bg, state_C, state_n, state_m, t_seq)
    x1s, conv_t = _sample_mixer_tail(xs, z, h_s, jnp.swapaxes(state_conv, 1, 2), hg, wmo, cw, cb, lng, lnb,
                                     wco, wout)
    conv_s = jnp.swapaxes(conv_t, 1, 2)
    y_t, ffn_s = _sample_ffn(x1s, state_ffn, fg, wup, fw, fb, wdown, fing)
    y_s = jnp.swapaxes(y_t, 0, 1)

    return (y_p, y_s,
            c_p, n_p, m_p.reshape(1, -1, N_HEADS), conv_p, ffn_p,
            c_s, n_s, m_s, conv_s, ffn_s)
```

```python
import functools

import jax
import jax.numpy as jnp
from jax import lax
from jax.experimental import pallas as pl
from jax.experimental.pallas import tpu as pltpu

F32 = jnp.float32
BF16 = jnp.bfloat16

D_MODEL = 1024
N_HEADS = 4
DH = D_MODEL // N_HEADS
CONV_K = 31
D_FF = 2816
FFN_K = 3
EPS = 1e-6

LANES = 128
Q0, K0, V0, O0, GV0, GG0, GA0, GB0 = (i * D_MODEL for i in range(8))
IG0 = 8 * D_MODEL
N_Z = IG0 + LANES

SUBLANES = 8
CONV_LEAD = 32
FFN_LEAD = 8
PROMPT_CHUNK = 512
FFN_CHUNK = 512
SAMPLE_ROWS = 128
MLSTM_SEQS = 8
PACK_ROWS = 128
VMEM_LIMIT = 58 * 1024 * 1024


def _mm(a, b):
    return jnp.dot(a, b, preferred_element_type=F32)


def _mm_nt(a, b):
    return lax.dot_general(a, b, (((1,), (1,)), ((), ())), preferred_element_type=F32)


def _mm_tn(a, b):
    return lax.dot_general(a, b, (((0,), (0,)), ((), ())), preferred_element_type=F32)


def _rmsnorm(x, g):
    return x * lax.rsqrt(jnp.mean(x * x, axis=-1, keepdims=True) + EPS) * g


def _layernorm(x, g, b):
    mu = jnp.mean(x, axis=-1, keepdims=True)
    xc = x - mu
    return xc * lax.rsqrt(jnp.mean(xc * xc, axis=-1, keepdims=True) + EPS) * g + b


def _sigmoid(x):
    return 0.5 * jnp.tanh(0.5 * x) + 0.5


def _log_sigmoid(x):
    return jnp.minimum(x, 0.0) - jnp.log1p(jnp.exp(-jnp.abs(x)))


def _cumsum_rows(x):
    n = x.shape[0]
    row = lax.broadcasted_iota(jnp.int32, x.shape, 0)
    s = 1
    while s < n:
        x = x + jnp.where(row >= s, pltpu.roll(x, s, axis=0), 0.0)
        s *= 2
    return x


def _gate_columns(zg, bias):
    zgb = zg + bias
    b_all = pltpu.roll(_cumsum_rows(_log_sigmoid(zgb)), LANES - N_HEADS, axis=1)
    return b_all, zgb - b_all


def _mlstm_head_steps(q, k, v, b_col, a_col, a_row, m_prev, c_prev, n_prev, n_valid, out):
    L, LK = q.shape[0], k.shape[0]
    row = lax.broadcasted_iota(jnp.int32, (L, LK), 0)
    col = lax.broadcasted_iota(jnp.int32, (L, LK), 1)
    d = jnp.where(col <= row, b_col + a_row, -jnp.inf)
    inter = b_col + m_prev
    d_max = jnp.max(d, axis=1, keepdims=True)
    qb, kb, vb = q.astype(BF16), k.astype(BF16), v.astype(BF16)
    qk = _mm_nt(qb, kb)
    yield
    m_t = jnp.maximum(inter, d_max)
    w_inter = jnp.exp(inter - m_t)
    p = jnp.exp(d - m_t)
    s = qk * p
    qc = _mm_nt(qb, c_prev().astype(BF16))
    yield
    num = w_inter * qc + _mm(s.astype(BF16), vb)
    den = w_inter * jnp.sum(q * n_prev, axis=1, keepdims=True) + jnp.sum(s, axis=1, keepdims=True)
    m_new = m_t[n_valid - 1:n_valid, :]
    b_last = b_col[n_valid - 1:n_valid, :]
    g_inter = jnp.exp(b_last + m_prev - m_new)
    g_col = jnp.exp(b_last + a_col - m_new)
    if n_valid < LK:
        g_col = jnp.where(lax.broadcasted_iota(jnp.int32, (LK, 1), 0) < n_valid, g_col, 0.0)
    yield
    out["h"] = num * (1.0 / jnp.maximum(jnp.abs(den), jnp.exp(-m_t)))
    out["c"] = g_inter * c_prev() + _mm_tn((g_col * v).astype(BF16), kb)
    out["n"] = g_inter * n_prev + jnp.sum(g_col * k, axis=0, keepdims=True)
    out["m"] = m_new
    yield


def _mlstm_head(q, k, v, b_col, a_col, a_row, m_prev, c_prev, n_prev, n_valid):
    out = {}
    for _ in _mlstm_head_steps(q, k, v, b_col, a_col, a_row, m_prev, c_prev, n_prev, n_valid, out):
        pass
    return out["h"], out["c"], out["n"], out["m"]


def _round_robin(streams):
    streams = list(streams)
    while streams:
        alive = []
        for gen in streams:
            try:
                next(gen)
                alive.append(gen)
            except StopIteration:
                pass
        streams = alive


def _head_out(h, head_g, o):
    mu = jnp.mean(h, axis=-1, keepdims=True)
    hc = h - mu
    hn = hc * lax.rsqrt(jnp.mean(hc * hc, axis=-1, keepdims=True) + EPS)
    return hn * head_g * _sigmoid(o)


def _causal_conv_cols(full_v, cwb_s, cb_ref, out_v, cs, T):
    off = CONV_LEAD - (CONV_K - 1)
    n_a = (off + CONV_K - 1) // SUBLANES + 1
    width = cs.stop - cs.start
    sub = lax.broadcasted_iota(jnp.int32, (SUBLANES, width), 0)
    bias = cb_ref[:, cs]
    tiles = {}

    def tile(i):
        if i not in tiles:
            tiles[i] = full_v[SUBLANES * i:SUBLANES * (i + 1), :]
        return tiles[i]

    prev = None
    for i in range(T // SUBLANES + 1):
        ys = []
        for r in range(SUBLANES):
            acc = None
            for a in range(n_a):
                j = SUBLANES * a + r - off
                if 0 <= j < CONV_K:
                    term = cwb_s[j, :, cs] * tile(i + a)
                    acc = term if acc is None else acc + term
            ys.append(acc)
        tiles.pop(i, None)
        cur = [ys[0]] + [pltpu.roll(ys[r], SUBLANES - r, axis=0) for r in range(1, SUBLANES)]
        if prev is not None:
            out = prev[0] + bias
            for r in range(1, SUBLANES):
                out = out + jnp.where(sub < SUBLANES - r, prev[r], cur[r])
            out_v[SUBLANES * (i - 1):SUBLANES * i, :] = out
        prev = cur


def _const_spec(shape):
    nd = len(shape)
    return pl.BlockSpec(shape, lambda *_: (0,) * nd, pipeline_mode=pl.Buffered(1))


def _pack_kernel(w_ref, o_ref):
    split = 4 * D_MODEL
    n_gate = 2 * N_HEADS
    rows = w_ref.shape[1]
    for src, dst in ((0, 0), (split + n_gate, split)):
        for t in range(split // LANES):
            blk = w_ref[src + t * LANES:src + (t + 1) * LANES, :]
            o_ref[:, dst + t * LANES:dst + (t + 1) * LANES] = blk.T.astype(BF16)
    g = jnp.concatenate([w_ref[split:split + n_gate, :], jnp.zeros((LANES - n_gate, rows), F32)], axis=0).T
    o_ref[:, IG0:IG0 + LANES] = g.astype(BF16)


def _pack_w_in(w_t):
    _, n_in, d = w_t.shape
    return pl.pallas_call(
        _pack_kernel,
        grid=(d // PACK_ROWS,),
        in_specs=[pl.BlockSpec((None, n_in, PACK_ROWS), lambda i: (0, 0, i))],
        out_specs=pl.BlockSpec((PACK_ROWS, N_Z), lambda i: (i, 0)),
        out_shape=jax.ShapeDtypeStruct((d, N_Z), BF16),
        compiler_params=pltpu.CompilerParams(
            dimension_semantics=("arbitrary",), vmem_limit_bytes=VMEM_LIMIT),
        name="pack_w_in",
    )(w_t)


def _prompt_mixer_kernel(x_ref, mixg_ref, w_ref, bg_ref, hg_ref, wmo_ref, cw_ref, cb_ref,
                         lng_ref, lnb_ref, wco_ref, wout_ref,
                         x1_ref, c_ref, n_ref, m_ref, conv_ref,
                         xn_s, ha_s, ga_s, gb_s, full_s, uc_s, m_s, cwb_s, *, n_chunks):
    T = x_ref.shape[1]
    c = pl.program_id(1)

    @pl.when(c == 0)
    def _():
        c_ref[...] = jnp.zeros_like(c_ref)
        n_ref[...] = jnp.zeros_like(n_ref)
        m_s[...] = jnp.zeros_like(m_s)
        full_s[0:CONV_LEAD, :] = jnp.zeros((CONV_LEAD, D_MODEL), F32)
        full_s[CONV_LEAD + T:CONV_LEAD + T + SUBLANES, :] = jnp.zeros((SUBLANES, D_MODEL), F32)
        for j in range(CONV_K):
            cwb_s[j] = jnp.broadcast_to(cw_ref[j:j + 1, :], (SUBLANES, D_MODEL))

    x = x_ref[0]
    xn_s[...] = _rmsnorm(x, mixg_ref[...]).astype(BF16)
    xn = xn_s[...]

    b_all, a_all = _gate_columns(_mm(xn, w_ref[:, IG0:IG0 + LANES]), bg_ref[...])
    a_t = a_all.T

    for h in range(N_HEADS):
        cs = slice(h * DH, (h + 1) * DH)
        gv = _mm(xn, w_ref[:, GV0 + h * DH:GV0 + (h + 1) * DH])
        gg = _mm(xn, w_ref[:, GG0 + h * DH:GG0 + (h + 1) * DH])
        full_s[CONV_LEAD:CONV_LEAD + T, cs] = gv * _sigmoid(gg)
        q = _mm(xn, w_ref[:, Q0 + h * DH:Q0 + (h + 1) * DH])
        k = _mm(xn, w_ref[:, K0 + h * DH:K0 + (h + 1) * DH]) * (DH ** -0.5)
        v = _mm(xn, w_ref[:, V0 + h * DH:V0 + (h + 1) * DH])
        o = _mm(xn, w_ref[:, O0 + h * DH:O0 + (h + 1) * DH])
        _causal_conv_cols(full_s.at[:, cs], cwb_s, cb_ref, uc_s.at[:, cs], cs, T)
        hh, c_new, n_new, m_new = _mlstm_head(
            q, k, v, b_all[:, h:h + 1], a_all[:, h:h + 1], a_t[h:h + 1, :],
            m_s[h:h + 1, 0:1], functools.partial(c_ref.__getitem__, (0, h)), n_ref[0, h:h + 1, :], T)
        c_ref[0, h] = c_new
        n_ref[0, h:h + 1, :] = n_new
        m_s[h:h + 1, :] = jnp.broadcast_to(m_new, (1, LANES))
        ha_s[:, cs] = _head_out(hh, hg_ref[:, cs], o).astype(BF16)
        ga_s[:, cs] = _mm(xn, w_ref[:, GA0 + h * DH:GA0 + (h + 1) * DH])
        gb_s[:, cs] = _mm(xn, w_ref[:, GB0 + h * DH:GB0 + (h + 1) * DH])

    y_a = _mm(ha_s[...], wmo_ref[...])
    tail = full_s[T + CONV_LEAD - (CONV_K - 1):T + CONV_LEAD, :]
    conv_ref[0] = tail
    full_s[CONV_LEAD - (CONV_K - 1):CONV_LEAD, :] = tail

    ucn = _layernorm(uc_s[...], lng_ref[...], lnb_ref[...])
    y_b = _mm((ucn * _sigmoid(ucn)).astype(BF16), wco_ref[...])

    mix = _sigmoid(ga_s[...]) * y_a + _sigmoid(gb_s[...]) * y_b
    x1_ref[0] = x + _mm(mix.astype(BF16), wout_ref[...])

    @pl.when(c == n_chunks - 1)
    def _():
        for h in range(N_HEADS):
            m_ref[0, :, h:h + 1] = m_s[h:h + 1, 0:1]


def _prompt_mixer(x, mixg, w_all, bg, hg, wmo, cw, cb, lng, lnb, wco, wout):
    B, S, D = x.shape
    T = PROMPT_CHUNK
    nc = S // T
    kern = functools.partial(_prompt_mixer_kernel, n_chunks=nc)
    return pl.pallas_call(
        kern,
        grid=(B, nc),
        in_specs=[
            pl.BlockSpec((1, T, D), lambda b, c: (b, c, 0)),
            _const_spec((1, D)), _const_spec((D, N_Z)), _const_spec((1, LANES)),
            _const_spec((1, D)), _const_spec((D, D)), _const_spec((CONV_K, D)), _const_spec((1, D)),
            _const_spec((1, D)), _const_spec((1, D)), _const_spec((D, D)), _const_spec((D, D)),
        ],
        out_specs=[
            pl.BlockSpec((1, T, D), lambda b, c: (b, c, 0)),
            pl.BlockSpec((None, 1, N_HEADS, DH, DH), lambda b, c: (0, b, 0, 0, 0)),
            pl.BlockSpec((None, 1, N_HEADS, DH), lambda b, c: (0, b, 0, 0)),
            pl.BlockSpec((1, 1, N_HEADS), lambda b, c: (b, 0, 0)),
            pl.BlockSpec((None, 1, CONV_K - 1, D), lambda b, c: (0, b, 0, 0)),
        ],
        out_shape=[
            jax.ShapeDtypeStruct((B, S, D), F32),
            jax.ShapeDtypeStruct((1, B, N_HEADS, DH, DH), F32),
            jax.ShapeDtypeStruct((1, B, N_HEADS, DH), F32),
            jax.ShapeDtypeStruct((B, 1, N_HEADS), F32),
            jax.ShapeDtypeStruct((1, B, CONV_K - 1, D), F32),
        ],
        scratch_shapes=[
            pltpu.VMEM((T, D), BF16), pltpu.VMEM((T, D), BF16),
            pltpu.VMEM((T, D), F32), pltpu.VMEM((T, D), F32),
            pltpu.VMEM((T + CONV_LEAD + SUBLANES, D), F32), pltpu.VMEM((T, D), F32),
            pltpu.VMEM((SUBLANES, LANES), F32), pltpu.VMEM((CONV_K, SUBLANES, D), F32),
        ],
        compiler_params=pltpu.CompilerParams(
            dimension_semantics=("arbitrary", "arbitrary"), vmem_limit_bytes=VMEM_LIMIT),
        name="prompt_mixer",
    )(x, mixg, w_all, bg, hg, wmo, cw, cb, lng, lnb, wco, wout)


def _ffn_tail(upc_a, upc_g, wdown_ref, x, fing_ref):
    act = (upc_a * _sigmoid(upc_a) * upc_g).astype(BF16)
    x2 = x + _mm(act, wdown_ref[...])
    return _rmsnorm(x2, fing_ref[...])


def _prompt_ffn_kernel(x_ref, fg_ref, wup_ref, fw_ref, fb_ref, wdown_ref, fing_ref,
                       y_ref, ffn_ref, up_s, act_s):
    T = x_ref.shape[1]
    lo = FFN_LEAD - (FFN_K - 1)
    c = pl.program_id(1)

    @pl.when(c == 0)
    def _():
        up_s[0:FFN_LEAD, :] = jnp.zeros((FFN_LEAD, 2 * D_FF), F32)

    x = x_ref[0]
    hf = _rmsnorm(x, fg_ref[...]).astype(BF16)
    nb = 2 * D_FF // DH
    for blk in range(nb):
        cs = slice(blk * DH, (blk + 1) * DH)
        up_s[FFN_LEAD:FFN_LEAD + T, cs] = _mm(hf, wup_ref[:, cs])

    def conv(cs):
        return (fb_ref[:, cs] + fw_ref[0:1, cs] * up_s[lo:lo + T, cs]
                + fw_ref[1:2, cs] * up_s[lo + 1:lo + 1 + T, cs]
                + fw_ref[2:3, cs] * up_s[lo + 2:lo + 2 + T, cs])

    for blk in range(D_FF // DH):
        ca = slice(blk * DH, (blk + 1) * DH)
        a = conv(ca)
        g = conv(slice(D_FF + blk * DH, D_FF + (blk + 1) * DH))
        act_s[:, ca] = (a * _sigmoid(a) * g).astype(BF16)
    tail = up_s[T + lo:T + FFN_LEAD, :]
    ffn_ref[0] = tail
    up_s[lo:FFN_LEAD, :] = tail
    x2 = x + _mm(act_s[...], wdown_ref[...])
    y_ref[0] = _rmsnorm(x2, fing_ref[...])


def _prompt_ffn(x1, fg, wup, fw, fb, wdown, fing):
    B, S, D = x1.shape
    T = FFN_CHUNK
    nc = S // T
    return pl.pallas_call(
        _prompt_ffn_kernel,
        grid=(B, nc),
        in_specs=[
            pl.BlockSpec((1, T, D), lambda b, c: (b, c, 0)),
            _const_spec((1, D)), _const_spec((D, 2 * D_FF)), _const_spec((FFN_K, 2 * D_FF)),
            _const_spec((1, 2 * D_FF)), _const_spec((D_FF, D)), _const_spec((1, D)),
        ],
        out_specs=[
            pl.BlockSpec((1, T, D), lambda b, c: (b, c, 0)),
            pl.BlockSpec((None, 1, FFN_K - 1, 2 * D_FF), lambda b, c: (0, b, 0, 0)),
        ],
        out_shape=[
            jax.ShapeDtypeStruct((B, S, D), F32),
            jax.ShapeDtypeStruct((1, B, FFN_K - 1, 2 * D_FF), F32),
        ],
        scratch_shapes=[pltpu.VMEM((T + FFN_LEAD, 2 * D_FF), F32), pltpu.VMEM((T, D_FF), BF16)],
        compiler_params=pltpu.CompilerParams(
            dimension_semantics=("arbitrary", "arbitrary"), vmem_limit_bytes=VMEM_LIMIT),
        name="prompt_ffn",
    )(x1, fg, wup, fw, fb, wdown, fing)


def _sample_proj_kernel(x_ref, mixg_ref, w_ref, z_ref):
    xn = _rmsnorm(x_ref[...], mixg_ref[...]).astype(BF16)
    for c0 in range(0, N_Z, DH):
        cs = slice(c0, min(c0 + DH, N_Z))
        z_ref[:, cs] = _mm(xn, w_ref[:, cs])


def _sample_proj(x, mixg, w_all):
    R, D = x.shape
    return pl.pallas_call(
        _sample_proj_kernel,
        grid=(R // SAMPLE_ROWS,),
        in_specs=[pl.BlockSpec((SAMPLE_ROWS, D), lambda i: (i, 0)),
                  _const_spec((1, D)), _const_spec((D, N_Z))],
        out_specs=pl.BlockSpec((SAMPLE_ROWS, N_Z), lambda i: (i, 0)),
        out_shape=jax.ShapeDtypeStruct((R, N_Z), F32),
        compiler_params=pltpu.CompilerParams(
            dimension_semantics=("arbitrary",), vmem_limit_bytes=VMEM_LIMIT),
        name="sample_proj",
    )(x, mixg, w_all)


def _sample_mlstm_kernel(zq_ref, zg_ref, bg_ref, c_ref, n_ref, m_ref,
                         h_ref, co_ref, no_ref, mo_ref, seq_s, *, t_seq):
    n_seq = zq_ref.shape[1]
    pad = LANES - SUBLANES
    wq = zq_ref.shape[2]
    seq_s[...] = jnp.zeros_like(seq_s)
    for j in range(n_seq):
        for t in range(t_seq):
            seq_s[j, t:t + 1, 0:wq] = zq_ref[t, j:j + 1, :]
            seq_s[j, t:t + 1, wq:] = zg_ref[t, j:j + 1, :]
    items = []
    for j in range(n_seq):
        zq = seq_s[j, :, 0:wq]
        b_all, a_seq = _gate_columns(seq_s[j, :, wq:], bg_ref[...])
        a_all = jnp.concatenate([a_seq, jnp.zeros((pad, LANES), F32)], axis=0)
        a_t = a_all.T
        for h in range(N_HEADS):
            q = zq[:, Q0 + h * DH:Q0 + (h + 1) * DH]
            k = zq[:, K0 + h * DH:K0 + (h + 1) * DH] * (DH ** -0.5)
            v = zq[:, V0 + h * DH:V0 + (h + 1) * DH]
            zeros = jnp.zeros((pad, DH), F32)
            k = jnp.concatenate([k, zeros], axis=0)
            v = jnp.concatenate([v, zeros], axis=0)
            out = {}
            steps = _mlstm_head_steps(
                q, k, v, b_all[:, h:h + 1], a_all[:, h:h + 1], a_t[h:h + 1, :],
                m_ref[j:j + 1, h:h + 1], functools.partial(c_ref.__getitem__, (j, h)),
                n_ref[j, h:h + 1, :], t_seq, out)
            items.append((j, h, out, steps))
    _round_robin(steps for _, _, _, steps in items)
    for j, h, out, _ in items:
        co_ref[j, h] = out["c"]
        no_ref[j, h:h + 1, :] = out["n"]
        mo_ref[j:j + 1, h:h + 1] = out["m"]
        for t in range(t_seq):
            h_ref[t, j:j + 1, h * DH:(h + 1) * DH] = out["h"][t:t + 1, :]


def _sample_mlstm(z, bg, c0, n0, m0, t_seq):
    nb = c0.shape[1]
    sb = MLSTM_SEQS
    kern = functools.partial(_sample_mlstm_kernel, t_seq=t_seq)
    c_spec = pl.BlockSpec((None, sb, N_HEADS, DH, DH), lambda i: (0, i, 0, 0, 0))
    n_spec = pl.BlockSpec((None, sb, N_HEADS, DH), lambda i: (0, i, 0, 0))
    m_spec = pl.BlockSpec((None, sb, N_HEADS), lambda i: (0, i, 0))
    return pl.pallas_call(
        kern,
        grid=(nb // sb,),
        in_specs=[
            pl.BlockSpec((t_seq, sb, 3 * D_MODEL), lambda i: (0, i, 0)),
            pl.BlockSpec((t_seq, sb, LANES), lambda i: (0, i, IG0 // LANES)),
            _const_spec((1, LANES)),
            c_spec, n_spec, m_spec,
        ],
        out_specs=[pl.BlockSpec((t_seq, sb, D_MODEL), lambda i: (0, i, 0)), c_spec, n_spec, m_spec],
        out_shape=[
            jax.ShapeDtypeStruct((t_seq, nb, D_MODEL), F32),
            jax.ShapeDtypeStruct(c0.shape, F32),
            jax.ShapeDtypeStruct(n0.shape, F32),
            jax.ShapeDtypeStruct(m0.shape, F32),
        ],
        scratch_shapes=[pltpu.VMEM((sb, SUBLANES, 3 * D_MODEL + LANES), F32)],
        compiler_params=pltpu.CompilerParams(
            dimension_semantics=("arbitrary",), vmem_limit_bytes=VMEM_LIMIT),
        name="sample_mlstm",
    )(z, z, bg, c0, n0, m0)


def _rows(ref):
    return jnp.concatenate([ref[t] for t in range(ref.shape[0])], axis=0)


def _sample_mixer_tail_kernel(x_ref, z_ref, h_ref, cst_ref, hg_ref, wmo_ref, cw_ref, cb_ref,
                              lng_ref, lnb_ref, wco_ref, wout_ref,
                              x1_ref, cnew_ref, uc_s):
    t_seq, n_seq, _ = x_ref.shape
    hist = CONV_K - 1
    z = _rows(z_ref)
    hcat = _rows(h_ref)
    ha = jnp.concatenate(
        [_head_out(hcat[:, h * DH:(h + 1) * DH], hg_ref[:, h * DH:(h + 1) * DH],
                   z[:, O0 + h * DH:O0 + (h + 1) * DH]) for h in range(N_HEADS)], axis=1)
    y_a = _mm(ha.astype(BF16), wmo_ref[...])

    for t in range(t_seq):
        uc_s[t] = z_ref[t, :, GV0:GV0 + D_MODEL] * _sigmoid(z_ref[t, :, GG0:GG0 + D_MODEL])

    def plane(r):
        return cst_ref[r] if r < hist else uc_s[r - hist]

    outs = []
    for t in range(t_seq):
        acc = cb_ref[...] + cw_ref[0:1, :] * plane(t)
        for j in range(1, CONV_K):
            acc = acc + cw_ref[j:j + 1, :] * plane(t + j)
        outs.append(acc)
    for r in range(hist):
        cnew_ref[r] = plane(r + t_seq)
    uc = jnp.concatenate(outs, axis=0)

    ucn = _layernorm(uc, lng_ref[...], lnb_ref[...])
    y_b = _mm((ucn * _sigmoid(ucn)).astype(BF16), wco_ref[...])
    mix = _sigmoid(z[:, GA0:GA0 + D_MODEL]) * y_a + _sigmoid(z[:, GB0:GB0 + D_MODEL]) * y_b
    x1 = _rows(x_ref) + _mm(mix.astype(BF16), wout_ref[...])
    for t in range(t_seq):
        x1_ref[t] = x1[t * n_seq:(t + 1) * n_seq, :]


def _sample_mixer_tail(x, z, h, cst, hg, wmo, cw, cb, lng, lnb, wco, wout):
    t_seq, nb, D = x.shape
    sb = SAMPLE_ROWS // t_seq
    hist = CONV_K - 1
    st_spec = pl.BlockSpec((None, hist, sb, D), lambda i: (0, 0, i, 0))
    return pl.pallas_call(
        _sample_mixer_tail_kernel,
        grid=(nb // sb,),
        in_specs=[
            pl.BlockSpec((t_seq, sb, D), lambda i: (0, i, 0)),
            pl.BlockSpec((t_seq, sb, N_Z), lambda i: (0, i, 0)),
            pl.BlockSpec((t_seq, sb, D), lambda i: (0, i, 0)),
            st_spec,
            _const_spec((1, D)), _const_spec((D, D)), _const_spec((CONV_K, D)), _const_spec((1, D)),
            _const_spec((1, D)), _const_spec((1, D)), _const_spec((D, D)), _const_spec((D, D)),
        ],
        out_specs=[pl.BlockSpec((t_seq, sb, D), lambda i: (0, i, 0)), st_spec],
        out_shape=[
            jax.ShapeDtypeStruct((t_seq, nb, D), F32),
            jax.ShapeDtypeStruct(cst.shape, F32),
        ],
        scratch_shapes=[pltpu.VMEM((t_seq, sb, D), F32)],
        compiler_params=pltpu.CompilerParams(
            dimension_semantics=("arbitrary",), vmem_limit_bytes=VMEM_LIMIT),
        name="sample_mixer_tail",
    )(x, z, h, cst, hg, wmo, cw, cb, lng, lnb, wco, wout)


def _sample_ffn_kernel(x_ref, fst_ref, fg_ref, wup_ref, fw_ref, fb_ref, wdown_ref, fing_ref,
                       y_ref, fnew_ref, up_s, upc_s):
    t_seq, n_seq, _ = x_ref.shape
    hist = FFN_K - 1
    x = _rows(x_ref)
    hf = _rmsnorm(x, fg_ref[...]).astype(BF16)
    for blk in range(2 * D_FF // DH):
        cs = slice(blk * DH, (blk + 1) * DH)
        up = _mm(hf, wup_ref[:, cs])
        for t in range(t_seq):
            up_s[hist + t, :, cs] = up[t * n_seq:(t + 1) * n_seq, :]
    for j in range(n_seq):
        for r in range(hist):
            up_s[r, j:j + 1, :] = fst_ref[j, r:r + 1, :]
    for t in range(t_seq):
        acc = fb_ref[...] + fw_ref[0:1, :] * up_s[t]
        for kk in range(1, FFN_K):
            acc = acc + fw_ref[kk:kk + 1, :] * up_s[t + kk]
        upc_s[t] = acc
    for j in range(n_seq):
        for r in range(hist):
            fnew_ref[j, r:r + 1, :] = up_s[t_seq + r, j:j + 1, :]
    upc = _rows(upc_s)
    y = _ffn_tail(upc[:, 0:D_FF], upc[:, D_FF:2 * D_FF], wdown_ref, x, fing_ref)
    for t in range(t_seq):
        y_ref[t] = y[t * n_seq:(t + 1) * n_seq, :]


def _sample_ffn(x1, fst, fg, wup, fw, fb, wdown, fing):
    t_seq, nb, D = x1.shape
    sb = SAMPLE_ROWS // t_seq
    hist = FFN_K - 1
    st_spec = pl.BlockSpec((None, sb, hist, 2 * D_FF), lambda i: (0, i, 0, 0))
    return pl.pallas_call(
        _sample_ffn_kernel,
        grid=(nb // sb,),
        in_specs=[
            pl.BlockSpec((t_seq, sb, D), lambda i: (0, i, 0)),
            st_spec,
            _const_spec((1, D)), _const_spec((D, 2 * D_FF)), _const_spec((FFN_K, 2 * D_FF)),
            _const_spec((1, 2 * D_FF)), _const_spec((D_FF, D)), _const_spec((1, D)),
        ],
        out_specs=[pl.BlockSpec((t_seq, sb, D), lambda i: (0, i, 0)), st_spec],
        out_shape=[
            jax.ShapeDtypeStruct((t_seq, nb, D), F32),
            jax.ShapeDtypeStruct(fst.shape, F32),
        ],
        scratch_shapes=[pltpu.VMEM((t_seq + hist, sb, 2 * D_FF), F32), pltpu.VMEM((t_seq, sb, 2 * D_FF), F32)],
        compiler_params=pltpu.CompilerParams(
            dimension_semantics=("arbitrary",), vmem_limit_bytes=VMEM_LIMIT),
        name="sample_ffn",
    )(x1, fst, fg, wup, fw, fb, wdown, fing)


def _pad_gate_bias(b):
    return jnp.pad(b, (0, LANES - b.shape[0])).reshape(1, LANES)


def kernel(x_prompt, x_sample, state_C, state_n, state_m, state_conv, state_ffn, mix_norm_g, w_in, b_if,
           head_norm_g, w_mlstm_out, conv_w, conv_b, conv_ln_g, conv_ln_b, w_conv_out, w_out, ffn_norm_g,
           w_up, ffn_conv_w, ffn_conv_b, w_down, final_norm_g):
    depth = w_in.shape[0]
    assert depth == 1, "single-layer trunk"
    l = 0
    row = lambda a: a.reshape(1, -1)
    w_all = _pack_w_in(jnp.swapaxes(w_in, 1, 2))
    bg = _pad_gate_bias(b_if[l])
    mixg, hg = row(mix_norm_g[l]), row(head_norm_g[l])
    wmo, wco, wout = (w.astype(BF16) for w in (w_mlstm_out[l], w_conv_out[l], w_out[l]))
    cw, cb = conv_w[l], row(conv_b[l])
    lng, lnb = row(conv_ln_g[l]), row(conv_ln_b[l])
    fg, fing = row(ffn_norm_g[l]), row(final_norm_g)
    wup, wdown = w_up[l].astype(BF16), w_down[l].astype(BF16)
    fw, fb = ffn_conv_w[l], row(ffn_conv_b[l])

    x1p, c_p, n_p, m_p, conv_p = _prompt_mixer(x_prompt, mixg, w_all, bg, hg, wmo, cw, cb, lng, lnb, wco, wout)
    y_p, ffn_p = _prompt_ffn(x1p, fg, wup, fw, fb, wdown, fing)

    nb, t_seq, d = x_sample.shape
    xs = jnp.swapaxes(x_sample, 0, 1)
    z = _sample_proj(xs.reshape(t_seq * nb, d), mixg, w_all).reshape(t_seq, nb, N_Z)
    h_s, c_s, n_s, m_s = _sample_mlstm(z, bg, state_C, state_n, state_m, t_seq)
    x1s, conv_t = _sample_mixer_tail(xs, z, h_s, jnp.swapaxes(state_conv, 1, 2), hg, wmo, cw, cb, lng, lnb,
                                     wco, wout)
    conv_s = jnp.swapaxes(conv_t, 1, 2)
    y_t, ffn_s = _sample_ffn(x1s, state_ffn, fg, wup, fw, fb, wdown, fing)
    y_s = jnp.swapaxes(y_t, 0, 1)

    return (y_p, y_s,
            c_p, n_p, m_p.reshape(1, -1, N_HEADS), conv_p, ffn_p,
            c_s, n_s, m_s, conv_s, ffn_s)
```

```python
import functools

import jax
import jax.numpy as jnp
from jax import lax
from jax.experimental import pallas as pl
from jax.experimental.pallas import tpu as pltpu

F32 = jnp.float32
BF16 = jnp.bfloat16

D_MODEL = 1024
N_HEADS = 4
DH = D_MODEL // N_HEADS
CONV_K = 31
D_FF = 2816
FFN_K = 3
EPS = 1e-6

LANES = 128
Q0, K0, V0, O0, GV0, GG0, GA0, GB0 = (i * D_MODEL for i in range(8))
IG0 = 8 * D_MODEL
N_Z = IG0 + LANES

SUBLANES = 8
CONV_LEAD = 32
FFN_LEAD = 8
PROMPT_CHUNK = 512
FFN_CHUNK = 512
SAMPLE_ROWS = 128
MLSTM_SEQS = 8
MLSTM_RING = 3
PACK_ROWS = 128
VMEM_LIMIT = 58 * 1024 * 1024


def _mm(a, b):
    return jnp.dot(a, b, preferred_element_type=F32)


def _mm_nt(a, b):
    return lax.dot_general(a, b, (((1,), (1,)), ((), ())), preferred_element_type=F32)


def _mm_tn(a, b):
    return lax.dot_general(a, b, (((0,), (0,)), ((), ())), preferred_element_type=F32)


def _rmsnorm(x, g):
    return x * lax.rsqrt(jnp.mean(x * x, axis=-1, keepdims=True) + EPS) * g


def _layernorm(x, g, b):
    mu = jnp.mean(x, axis=-1, keepdims=True)
    xc = x - mu
    return xc * lax.rsqrt(jnp.mean(xc * xc, axis=-1, keepdims=True) + EPS) * g + b


def _sigmoid(x):
    return 0.5 * jnp.tanh(0.5 * x) + 0.5


def _log_sigmoid(x):
    return jnp.minimum(x, 0.0) - jnp.log1p(jnp.exp(-jnp.abs(x)))


def _cumsum_rows(x):
    n = x.shape[0]
    row = lax.broadcasted_iota(jnp.int32, x.shape, 0)
    s = 1
    while s < n:
        x = x + jnp.where(row >= s, pltpu.roll(x, s, axis=0), 0.0)
        s *= 2
    return x


def _gate_columns(zg, bias):
    zgb = zg + bias
    b_all = pltpu.roll(_cumsum_rows(_log_sigmoid(zgb)), LANES - N_HEADS, axis=1)
    return b_all, zgb - b_all


def _mlstm_head_steps(q, k, v, b_col, a_col, a_row, m_prev, c_prev, n_prev, n_valid, out):
    L, LK = q.shape[0], k.shape[0]
    row = lax.broadcasted_iota(jnp.int32, (L, LK), 0)
    col = lax.broadcasted_iota(jnp.int32, (L, LK), 1)
    d = jnp.where(col <= row, b_col + a_row, -jnp.inf)
    inter = b_col + m_prev
    d_max = jnp.max(d, axis=1, keepdims=True)
    qb, kb, vb = q.astype(BF16), k.astype(BF16), v.astype(BF16)
    qk = _mm_nt(qb, kb)
    yield
    m_t = jnp.maximum(inter, d_max)
    w_inter = jnp.exp(inter - m_t)
    p = jnp.exp(d - m_t)
    s = qk * p
    qc = _mm_nt(qb, c_prev().astype(BF16))
    yield
    num = w_inter * qc + _mm(s.astype(BF16), vb)
    den = w_inter * jnp.sum(q * n_prev, axis=1, keepdims=True) + jnp.sum(s, axis=1, keepdims=True)
    m_new = m_t[n_valid - 1:n_valid, :]
    b_last = b_col[n_valid - 1:n_valid, :]
    g_inter = jnp.exp(b_last + m_prev - m_new)
    g_col = jnp.exp(b_last + a_col - m_new)
    if n_valid < LK:
        g_col = jnp.where(lax.broadcasted_iota(jnp.int32, (LK, 1), 0) < n_valid, g_col, 0.0)
    yield
    out["h"] = num * (1.0 / jnp.maximum(jnp.abs(den), jnp.exp(-m_t)))
    out["c"] = g_inter * c_prev() + _mm_tn((g_col * v).astype(BF16), kb)
    out["n"] = g_inter * n_prev + jnp.sum(g_col * k, axis=0, keepdims=True)
    out["m"] = m_new
    yield


def _mlstm_head(q, k, v, b_col, a_col, a_row, m_prev, c_prev, n_prev, n_valid):
    out = {}
    for _ in _mlstm_head_steps(q, k, v, b_col, a_col, a_row, m_prev, c_prev, n_prev, n_valid, out):
        pass
    return out["h"], out["c"], out["n"], out["m"]


def _round_robin(streams):
    streams = list(streams)
    while streams:
        alive = []
        for gen in streams:
            try:
                next(gen)
                alive.append(gen)
            except StopIteration:
                pass
        streams = alive


def _head_out(h, head_g, o):
    mu = jnp.mean(h, axis=-1, keepdims=True)
    hc = h - mu
    hn = hc * lax.rsqrt(jnp.mean(hc * hc, axis=-1, keepdims=True) + EPS)
    return hn * head_g * _sigmoid(o)


def _causal_conv_cols(full_v, cwb_s, cb_ref, out_v, cs, T):
    off = CONV_LEAD - (CONV_K - 1)
    n_a = (off + CONV_K - 1) // SUBLANES + 1
    width = cs.stop - cs.start
    sub = lax.broadcasted_iota(jnp.int32, (SUBLANES, width), 0)
    bias = cb_ref[:, cs]
    tiles = {}

    def tile(i):
        if i not in tiles:
            tiles[i] = full_v[SUBLANES * i:SUBLANES * (i + 1), :]
        return tiles[i]

    prev = None
    for i in range(T // SUBLANES + 1):
        ys = []
        for r in range(SUBLANES):
            acc = None
            for a in range(n_a):
                j = SUBLANES * a + r - off
                if 0 <= j < CONV_K:
                    term = cwb_s[j, :, cs] * tile(i + a)
                    acc = term if acc is None else acc + term
            ys.append(acc)
        tiles.pop(i, None)
        cur = [ys[0]] + [pltpu.roll(ys[r], SUBLANES - r, axis=0) for r in range(1, SUBLANES)]
        if prev is not None:
            out = prev[0] + bias
            for r in range(1, SUBLANES):
                out = out + jnp.where(sub < SUBLANES - r, prev[r], cur[r])
            out_v[SUBLANES * (i - 1):SUBLANES * i, :] = out
        prev = cur


def _const_spec(shape):
    nd = len(shape)
    return pl.BlockSpec(shape, lambda *_: (0,) * nd, pipeline_mode=pl.Buffered(1))


def _pack_kernel(w_ref, o_ref):
    split = 4 * D_MODEL
    n_gate = 2 * N_HEADS
    rows = w_ref.shape[1]
    for src, dst in ((0, 0), (split + n_gate, split)):
        for t in range(split // LANES):
            blk = w_ref[src + t * LANES:src + (t + 1) * LANES, :]
            o_ref[:, dst + t * LANES:dst + (t + 1) * LANES] = blk.T.astype(BF16)
    g = jnp.concatenate([w_ref[split:split + n_gate, :], jnp.zeros((LANES - n_gate, rows), F32)], axis=0).T
    o_ref[:, IG0:IG0 + LANES] = g.astype(BF16)


def _pack_w_in(w_t):
    _, n_in, d = w_t.shape
    return pl.pallas_call(
        _pack_kernel,
        grid=(d // PACK_ROWS,),
        in_specs=[pl.BlockSpec((None, n_in, PACK_ROWS), lambda i: (0, 0, i))],
        out_specs=pl.BlockSpec((PACK_ROWS, N_Z), lambda i: (i, 0)),
        out_shape=jax.ShapeDtypeStruct((d, N_Z), BF16),
        compiler_params=pltpu.CompilerParams(
            dimension_semantics=("arbitrary",), vmem_limit_bytes=VMEM_LIMIT),
        name="pack_w_in",
    )(w_t)


def _prompt_mixer_kernel(x_ref, mixg_ref, w_ref, bg_ref, hg_ref, wmo_ref, cw_ref, cb_ref,
                         lng_ref, lnb_ref, wco_ref, wout_ref,
                         x1_ref, c_ref, n_ref, m_ref, conv_ref,
                         xn_s, ha_s, ga_s, gb_s, full_s, uc_s, m_s, cwb_s, *, n_chunks):
    T = x_ref.shape[1]
    c = pl.program_id(1)

    @pl.when(c == 0)
    def _():
        c_ref[...] = jnp.zeros_like(c_ref)
        n_ref[...] = jnp.zeros_like(n_ref)
        m_s[...] = jnp.zeros_like(m_s)
        full_s[0:CONV_LEAD, :] = jnp.zeros((CONV_LEAD, D_MODEL), F32)
        full_s[CONV_LEAD + T:CONV_LEAD + T + SUBLANES, :] = jnp.zeros((SUBLANES, D_MODEL), F32)
        for j in range(CONV_K):
            cwb_s[j] = jnp.broadcast_to(cw_ref[j:j + 1, :], (SUBLANES, D_MODEL))

    x = x_ref[0]
    xn_s[...] = _rmsnorm(x, mixg_ref[...]).astype(BF16)
    xn = xn_s[...]

    b_all, a_all = _gate_columns(_mm(xn, w_ref[:, IG0:IG0 + LANES]), bg_ref[...])
    a_t = a_all.T

    for h in range(N_HEADS):
        cs = slice(h * DH, (h + 1) * DH)
        gv = _mm(xn, w_ref[:, GV0 + h * DH:GV0 + (h + 1) * DH])
        gg = _mm(xn, w_ref[:, GG0 + h * DH:GG0 + (h + 1) * DH])
        full_s[CONV_LEAD:CONV_LEAD + T, cs] = gv * _sigmoid(gg)
        q = _mm(xn, w_ref[:, Q0 + h * DH:Q0 + (h + 1) * DH])
        k = _mm(xn, w_ref[:, K0 + h * DH:K0 + (h + 1) * DH]) * (DH ** -0.5)
        v = _mm(xn, w_ref[:, V0 + h * DH:V0 + (h + 1) * DH])
        o = _mm(xn, w_ref[:, O0 + h * DH:O0 + (h + 1) * DH])
        _causal_conv_cols(full_s.at[:, cs], cwb_s, cb_ref, uc_s.at[:, cs], cs, T)
        hh, c_new, n_new, m_new = _mlstm_head(
            q, k, v, b_all[:, h:h + 1], a_all[:, h:h + 1], a_t[h:h + 1, :],
            m_s[h:h + 1, 0:1], functools.partial(c_ref.__getitem__, (0, h)), n_ref[0, h:h + 1, :], T)
        c_ref[0, h] = c_new
        n_ref[0, h:h + 1, :] = n_new
        m_s[h:h + 1, :] = jnp.broadcast_to(m_new, (1, LANES))
        ha_s[:, cs] = _head_out(hh, hg_ref[:, cs], o).astype(BF16)
        ga_s[:, cs] = _mm(xn, w_ref[:, GA0 + h * DH:GA0 + (h + 1) * DH])
        gb_s[:, cs] = _mm(xn, w_ref[:, GB0 + h * DH:GB0 + (h + 1) * DH])

    y_a = _mm(ha_s[...], wmo_ref[...])
    tail = full_s[T + CONV_LEAD - (CONV_K - 1):T + CONV_LEAD, :]
    conv_ref[0] = tail
    full_s[CONV_LEAD - (CONV_K - 1):CONV_LEAD, :] = tail

    ucn = _layernorm(uc_s[...], lng_ref[...], lnb_ref[...])
    y_b = _mm((ucn * _sigmoid(ucn)).astype(BF16), wco_ref[...])

    mix = _sigmoid(ga_s[...]) * y_a + _sigmoid(gb_s[...]) * y_b
    x1_ref[0] = x + _mm(mix.astype(BF16), wout_ref[...])

    @pl.when(c == n_chunks - 1)
    def _():
        for h in range(N_HEADS):
            m_ref[0, :, h:h + 1] = m_s[h:h + 1, 0:1]


def _prompt_mixer(x, mixg, w_all, bg, hg, wmo, cw, cb, lng, lnb, wco, wout):
    B, S, D = x.shape
    T = PROMPT_CHUNK
    nc = S // T
    kern = functools.partial(_prompt_mixer_kernel, n_chunks=nc)
    return pl.pallas_call(
        kern,
        grid=(B, nc),
        in_specs=[
            pl.BlockSpec((1, T, D), lambda b, c: (b, c, 0)),
            _const_spec((1, D)), _const_spec((D, N_Z)), _const_spec((1, LANES)),
            _const_spec((1, D)), _const_spec((D, D)), _const_spec((CONV_K, D)), _const_spec((1, D)),
            _const_spec((1, D)), _const_spec((1, D)), _const_spec((D, D)), _const_spec((D, D)),
        ],
        out_specs=[
            pl.BlockSpec((1, T, D), lambda b, c: (b, c, 0)),
            pl.BlockSpec((None, 1, N_HEADS, DH, DH), lambda b, c: (0, b, 0, 0, 0)),
            pl.BlockSpec((None, 1, N_HEADS, DH), lambda b, c: (0, b, 0, 0)),
            pl.BlockSpec((1, 1, N_HEADS), lambda b, c: (b, 0, 0)),
            pl.BlockSpec((None, 1, CONV_K - 1, D), lambda b, c: (0, b, 0, 0)),
        ],
        out_shape=[
            jax.ShapeDtypeStruct((B, S, D), F32),
            jax.ShapeDtypeStruct((1, B, N_HEADS, DH, DH), F32),
            jax.ShapeDtypeStruct((1, B, N_HEADS, DH), F32),
            jax.ShapeDtypeStruct((B, 1, N_HEADS), F32),
            jax.ShapeDtypeStruct((1, B, CONV_K - 1, D), F32),
        ],
        scratch_shapes=[
            pltpu.VMEM((T, D), BF16), pltpu.VMEM((T, D), BF16),
            pltpu.VMEM((T, D), F32), pltpu.VMEM((T, D), F32),
            pltpu.VMEM((T + CONV_LEAD + SUBLANES, D), F32), pltpu.VMEM((T, D), F32),
            pltpu.VMEM((SUBLANES, LANES), F32), pltpu.VMEM((CONV_K, SUBLANES, D), F32),
        ],
        compiler_params=pltpu.CompilerParams(
            dimension_semantics=("arbitrary", "arbitrary"), vmem_limit_bytes=VMEM_LIMIT),
        name="prompt_mixer",
    )(x, mixg, w_all, bg, hg, wmo, cw, cb, lng, lnb, wco, wout)


def _ffn_tail(upc_a, upc_g, wdown_ref, x, fing_ref):
    act = (upc_a * _sigmoid(upc_a) * upc_g).astype(BF16)
    x2 = x + _mm(act, wdown_ref[...])
    return _rmsnorm(x2, fing_ref[...])


def _prompt_ffn_kernel(x_ref, fg_ref, wup_ref, fw_ref, fb_ref, wdown_ref, fing_ref,
                       y_ref, ffn_ref, up_s, act_s):
    T = x_ref.shape[1]
    lo = FFN_LEAD - (FFN_K - 1)
    c = pl.program_id(1)

    @pl.when(c == 0)
    def _():
        up_s[0:FFN_LEAD, :] = jnp.zeros((FFN_LEAD, 2 * D_FF), F32)

    x = x_ref[0]
    hf = _rmsnorm(x, fg_ref[...]).astype(BF16)
    nb = 2 * D_FF // DH
    for blk in range(nb):
        cs = slice(blk * DH, (blk + 1) * DH)
        up_s[FFN_LEAD:FFN_LEAD + T, cs] = _mm(hf, wup_ref[:, cs])

    def conv(cs):
        return (fb_ref[:, cs] + fw_ref[0:1, cs] * up_s[lo:lo + T, cs]
                + fw_ref[1:2, cs] * up_s[lo + 1:lo + 1 + T, cs]
                + fw_ref[2:3, cs] * up_s[lo + 2:lo + 2 + T, cs])

    for blk in range(D_FF // DH):
        ca = slice(blk * DH, (blk + 1) * DH)
        a = conv(ca)
        g = conv(slice(D_FF + blk * DH, D_FF + (blk + 1) * DH))
        act_s[:, ca] = (a * _sigmoid(a) * g).astype(BF16)
    tail = up_s[T + lo:T + FFN_LEAD, :]
    ffn_ref[0] = tail
    up_s[lo:FFN_LEAD, :] = tail
    x2 = x + _mm(act_s[...], wdown_ref[...])
    y_ref[0] = _rmsnorm(x2, fing_ref[...])


def _prompt_ffn(x1, fg, wup, fw, fb, wdown, fing):
    B, S, D = x1.shape
    T = FFN_CHUNK
    nc = S // T
    return pl.pallas_call(
        _prompt_ffn_kernel,
        grid=(B, nc),
        in_specs=[
            pl.BlockSpec((1, T, D), lambda b, c: (b, c, 0)),
            _const_spec((1, D)), _const_spec((D, 2 * D_FF)), _const_spec((FFN_K, 2 * D_FF)),
            _const_spec((1, 2 * D_FF)), _const_spec((D_FF, D)), _const_spec((1, D)),
        ],
        out_specs=[
            pl.BlockSpec((1, T, D), lambda b, c: (b, c, 0)),
            pl.BlockSpec((None, 1, FFN_K - 1, 2 * D_FF), lambda b, c: (0, b, 0, 0)),
        ],
        out_shape=[
            jax.ShapeDtypeStruct((B, S, D), F32),
            jax.ShapeDtypeStruct((1, B, FFN_K - 1, 2 * D_FF), F32),
        ],
        scratch_shapes=[pltpu.VMEM((T + FFN_LEAD, 2 * D_FF), F32), pltpu.VMEM((T, D_FF), BF16)],
        compiler_params=pltpu.CompilerParams(
            dimension_semantics=("arbitrary", "arbitrary"), vmem_limit_bytes=VMEM_LIMIT),
        name="prompt_ffn",
    )(x1, fg, wup, fw, fb, wdown, fing)


def _sample_proj_kernel(x_ref, mixg_ref, w_ref, z_ref):
    xn = _rmsnorm(x_ref[...], mixg_ref[...]).astype(BF16)
    for c0 in range(0, N_Z, DH):
        cs = slice(c0, min(c0 + DH, N_Z))
        z_ref[:, cs] = _mm(xn, w_ref[:, cs])


def _sample_proj(x, mixg, w_all):
    R, D = x.shape
    return pl.pallas_call(
        _sample_proj_kernel,
        grid=(R // SAMPLE_ROWS,),
        in_specs=[pl.BlockSpec((SAMPLE_ROWS, D), lambda i: (i, 0)),
                  _const_spec((1, D)), _const_spec((D, N_Z))],
        out_specs=pl.BlockSpec((SAMPLE_ROWS, N_Z), lambda i: (i, 0)),
        out_shape=jax.ShapeDtypeStruct((R, N_Z), F32),
        compiler_params=pltpu.CompilerParams(
            dimension_semantics=("arbitrary",), vmem_limit_bytes=VMEM_LIMIT),
        name="sample_proj",
    )(x, mixg, w_all)


def _sample_mlstm_kernel(zq_ref, zg_ref, bg_ref, c_hbm, n_ref, m_ref,
                         h_ref, co_ref, no_ref, mo_ref, seq_s, c_buf, c_sem, *, t_seq, n_steps):
    n_seq = zq_ref.shape[1]
    pad = LANES - SUBLANES
    wq = zq_ref.shape[2]
    step = pl.program_id(0)
    slot = step % MLSTM_RING

    def c_copy(s):
        k = s % MLSTM_RING
        return pltpu.make_async_copy(c_hbm.at[0, pl.ds(s * n_seq, n_seq)], c_buf.at[k], c_sem.at[k])

    @pl.when(step == 0)
    def _():
        for s in range(min(MLSTM_RING - 1, n_steps)):
            c_copy(s).start()

    @pl.when(step + (MLSTM_RING - 1) < n_steps)
    def _():
        c_copy(step + (MLSTM_RING - 1)).start()

    c_copy(step).wait()
    seq_s[...] = jnp.zeros_like(seq_s)
    for j in range(n_seq):
        for t in range(t_seq):
            seq_s[j, t:t + 1, 0:wq] = zq_ref[t, j:j + 1, :]
            seq_s[j, t:t + 1, wq:] = zg_ref[t, j:j + 1, :]
    items = []
    for j in range(n_seq):
        zq = seq_s[j, :, 0:wq]
        b_all, a_seq = _gate_columns(seq_s[j, :, wq:], bg_ref[...])
        a_all = jnp.concatenate([a_seq, jnp.zeros((pad, LANES), F32)], axis=0)
        a_t = a_all.T
        for h in range(N_HEADS):
            q = zq[:, Q0 + h * DH:Q0 + (h + 1) * DH]
            k = zq[:, K0 + h * DH:K0 + (h + 1) * DH] * (DH ** -0.5)
            v = zq[:, V0 + h * DH:V0 + (h + 1) * DH]
            zeros = jnp.zeros((pad, DH), F32)
            k = jnp.concatenate([k, zeros], axis=0)
            v = jnp.concatenate([v, zeros], axis=0)
            out = {}
            steps = _mlstm_head_steps(
                q, k, v, b_all[:, h:h + 1], a_all[:, h:h + 1], a_t[h:h + 1, :],
                m_ref[j:j + 1, h:h + 1], functools.partial(c_buf.__getitem__, (slot, j, h)),
                n_ref[j, h:h + 1, :], t_seq, out)
            items.append((j, h, out, steps))
    _round_robin(steps for _, _, _, steps in items)
    for j, h, out, _ in items:
        co_ref[j, h] = out["c"]
        no_ref[j, h:h + 1, :] = out["n"]
        mo_ref[j:j + 1, h:h + 1] = out["m"]
        for t in range(t_seq):
            h_ref[t, j:j + 1, h * DH:(h + 1) * DH] = out["h"][t:t + 1, :]


def _sample_mlstm(z, bg, c0, n0, m0, t_seq):
    nb = c0.shape[1]
    sb = MLSTM_SEQS
    kern = functools.partial(_sample_mlstm_kernel, t_seq=t_seq, n_steps=nb // sb)
    c_spec = pl.BlockSpec((None, sb, N_HEADS, DH, DH), lambda i: (0, i, 0, 0, 0))
    n_spec = pl.BlockSpec((None, sb, N_HEADS, DH), lambda i: (0, i, 0, 0))
    m_spec = pl.BlockSpec((None, sb, N_HEADS), lambda i: (0, i, 0))
    return pl.pallas_call(
        kern,
        grid=(nb // sb,),
        in_specs=[
            pl.BlockSpec((t_seq, sb, 3 * D_MODEL), lambda i: (0, i, 0)),
            pl.BlockSpec((t_seq, sb, LANES), lambda i: (0, i, IG0 // LANES)),
            _const_spec((1, LANES)),
            pl.BlockSpec(memory_space=pl.ANY), n_spec, m_spec,
        ],
        out_specs=[pl.BlockSpec((t_seq, sb, D_MODEL), lambda i: (0, i, 0)), c_spec, n_spec, m_spec],
        out_shape=[
            jax.ShapeDtypeStruct((t_seq, nb, D_MODEL), F32),
            jax.ShapeDtypeStruct(c0.shape, F32),
            jax.ShapeDtypeStruct(n0.shape, F32),
            jax.ShapeDtypeStruct(m0.shape, F32),
        ],
        scratch_shapes=[pltpu.VMEM((sb, SUBLANES, 3 * D_MODEL + LANES), F32),
                        pltpu.VMEM((MLSTM_RING, sb, N_HEADS, DH, DH), F32),
                        pltpu.SemaphoreType.DMA((MLSTM_RING,))],
        compiler_params=pltpu.CompilerParams(
            dimension_semantics=("arbitrary",), vmem_limit_bytes=VMEM_LIMIT),
        name="sample_mlstm",
    )(z, z, bg, c0, n0, m0)


def _rows(ref):
    return jnp.concatenate([ref[t] for t in range(ref.shape[0])], axis=0)


def _sample_mixer_tail_kernel(x_ref, z_ref, h_ref, cst_ref, hg_ref, wmo_ref, cw_ref, cb_ref,
                              lng_ref, lnb_ref, wco_ref, wout_ref,
                              x1_ref, cnew_ref, uc_s):
    t_seq, n_seq, _ = x_ref.shape
    hist = CONV_K - 1
    z = _rows(z_ref)
    hcat = _rows(h_ref)
    ha = jnp.concatenate(
        [_head_out(hcat[:, h * DH:(h + 1) * DH], hg_ref[:, h * DH:(h + 1) * DH],
                   z[:, O0 + h * DH:O0 + (h + 1) * DH]) for h in range(N_HEADS)], axis=1)
    y_a = _mm(ha.astype(BF16), wmo_ref[...])

    for t in range(t_seq):
        uc_s[t] = z_ref[t, :, GV0:GV0 + D_MODEL] * _sigmoid(z_ref[t, :, GG0:GG0 + D_MODEL])

    def plane(r):
        return cst_ref[r] if r < hist else uc_s[r - hist]

    outs = []
    for t in range(t_seq):
        acc = cb_ref[...] + cw_ref[0:1, :] * plane(t)
        for j in range(1, CONV_K):
            acc = acc + cw_ref[j:j + 1, :] * plane(t + j)
        outs.append(acc)
    for r in range(hist):
        cnew_ref[r] = plane(r + t_seq)
    uc = jnp.concatenate(outs, axis=0)

    ucn = _layernorm(uc, lng_ref[...], lnb_ref[...])
    y_b = _mm((ucn * _sigmoid(ucn)).astype(BF16), wco_ref[...])
    mix = _sigmoid(z[:, GA0:GA0 + D_MODEL]) * y_a + _sigmoid(z[:, GB0:GB0 + D_MODEL]) * y_b
    x1 = _rows(x_ref) + _mm(mix.astype(BF16), wout_ref[...])
    for t in range(t_seq):
        x1_ref[t] = x1[t * n_seq:(t + 1) * n_seq, :]


def _sample_mixer_tail(x, z, h, cst, hg, wmo, cw, cb, lng, lnb, wco, wout):
    t_seq, nb, D = x.shape
    sb = SAMPLE_ROWS // t_seq
    hist = CONV_K - 1
    st_spec = pl.BlockSpec((None, hist, sb, D), lambda i: (0, 0, i, 0))
    return pl.pallas_call(
        _sample_mixer_tail_kernel,
        grid=(nb // sb,),
        in_specs=[
            pl.BlockSpec((t_seq, sb, D), lambda i: (0, i, 0)),
            pl.BlockSpec((t_seq, sb, N_Z), lambda i: (0, i, 0)),
            pl.BlockSpec((t_seq, sb, D), lambda i: (0, i, 0)),
            st_spec,
            _const_spec((1, D)), _const_spec((D, D)), _const_spec((CONV_K, D)), _const_spec((1, D)),
            _const_spec((1, D)), _const_spec((1, D)), _const_spec((D, D)), _const_spec((D, D)),
        ],
        out_specs=[pl.BlockSpec((t_seq, sb, D), lambda i: (0, i, 0)), st_spec],
        out_shape=[
            jax.ShapeDtypeStruct((t_seq, nb, D), F32),
            jax.ShapeDtypeStruct(cst.shape, F32),
        ],
        scratch_shapes=[pltpu.VMEM((t_seq, sb, D), F32)],
        compiler_params=pltpu.CompilerParams(
            dimension_semantics=("arbitrary",), vmem_limit_bytes=VMEM_LIMIT),
        name="sample_mixer_tail",
    )(x, z, h, cst, hg, wmo, cw, cb, lng, lnb, wco, wout)


def _sample_ffn_kernel(x_ref, fst_ref, fg_ref, wup_ref, fw_ref, fb_ref, wdown_ref, fing_ref,
                       y_ref, fnew_ref, up_s, upc_s):
    t_seq, n_seq, _ = x_ref.shape
    hist = FFN_K - 1
    x = _rows(x_ref)
    hf = _rmsnorm(x, fg_ref[...]).astype(BF16)
    for blk in range(2 * D_FF // DH):
        cs = slice(blk * DH, (blk + 1) * DH)
        up = _mm(hf, wup_ref[:, cs])
        for t in range(t_seq):
            up_s[hist + t, :, cs] = up[t * n_seq:(t + 1) * n_seq, :]
    for j in range(n_seq):
        for r in range(hist):
            up_s[r, j:j + 1, :] = fst_ref[j, r:r + 1, :]
    for t in range(t_seq):
        acc = fb_ref[...] + fw_ref[0:1, :] * up_s[t]
        for kk in range(1, FFN_K):
            acc = acc + fw_ref[kk:kk + 1, :] * up_s[t + kk]
        upc_s[t] = acc
    for j in range(n_seq):
        for r in range(hist):
            fnew_ref[j, r:r + 1, :] = up_s[t_seq + r, j:j + 1, :]
    upc = _rows(upc_s)
    y = _ffn_tail(upc[:, 0:D_FF], upc[:, D_FF:2 * D_FF], wdown_ref, x, fing_ref)
    for t in range(t_seq):
        y_ref[t] = y[t * n_seq:(t + 1) * n_seq, :]


def _sample_ffn(x1, fst, fg, wup, fw, fb, wdown, fing):
    t_seq, nb, D = x1.shape
    sb = SAMPLE_ROWS // t_seq
    hist = FFN_K - 1
    st_spec = pl.BlockSpec((None, sb, hist, 2 * D_FF), lambda i: (0, i, 0, 0))
    return pl.pallas_call(
        _sample_ffn_kernel,
        grid=(nb // sb,),
        in_specs=[
            pl.BlockSpec((t_seq, sb, D), lambda i: (0, i, 0)),
            st_spec,
            _const_spec((1, D)), _const_spec((D, 2 * D_FF)), _const_spec((FFN_K, 2 * D_FF)),
            _const_spec((1, 2 * D_FF)), _const_spec((D_FF, D)), _const_spec((1, D)),
        ],
        out_specs=[pl.BlockSpec((t_seq, sb, D), lambda i: (0, i, 0)), st_spec],
        out_shape=[
            jax.ShapeDtypeStruct((t_seq, nb, D), F32),
            jax.ShapeDtypeStruct(fst.shape, F32),
        ],
        scratch_shapes=[pltpu.VMEM((t_seq + hist, sb, 2 * D_FF), F32), pltpu.VMEM((t_seq, sb, 2 * D_FF), F32)],
        compiler_params=pltpu.CompilerParams(
            dimension_semantics=("arbitrary",), vmem_limit_bytes=VMEM_LIMIT),
        name="sample_ffn",
    )(x1, fst, fg, wup, fw, fb, wdown, fing)


def _pad_gate_bias(b):
    return jnp.pad(b, (0, LANES - b.shape[0])).reshape(1, LANES)


def kernel(x_prompt, x_sample, state_C, state_n, state_m, state_conv, state_ffn, mix_norm_g, w_in, b_if,
           head_norm_g, w_mlstm_out, conv_w, conv_b, conv_ln_g, conv_ln_b, w_conv_out, w_out, ffn_norm_g,
           w_up, ffn_conv_w, ffn_conv_b, w_down, final_norm_g):
    depth = w_in.shape[0]
    assert depth == 1, "single-layer trunk"
    l = 0
    row = lambda a: a.reshape(1, -1)
    w_all = _pack_w_in(jnp.swapaxes(w_in, 1, 2))
    bg = _pad_gate_bias(b_if[l])
    mixg, hg = row(mix_norm_g[l]), row(head_norm_g[l])
    wmo, wco, wout = (w.astype(BF16) for w in (w_mlstm_out[l], w_conv_out[l], w_out[l]))
    cw, cb = conv_w[l], row(conv_b[l])
    lng, lnb = row(conv_ln_g[l]), row(conv_ln_b[l])
    fg, fing = row(ffn_norm_g[l]), row(final_norm_g)
    wup, wdown = w_up[l].astype(BF16), w_down[l].astype(BF16)
    fw, fb = ffn_conv_w[l], row(ffn_conv_b[l])

    x1p, c_p, n_p, m_p, conv_p = _prompt_mixer(x_prompt, mixg, w_all, bg, hg, wmo, cw, cb, lng, lnb, wco, wout)
    y_p, ffn_p = _prompt_ffn(x1p, fg, wup, fw, fb, wdown, fing)

    nb, t_seq, d = x_sample.shape
    xs = jnp.swapaxes(x_sample, 0, 1)
    z = _sample_proj(xs.reshape(t_seq * nb, d), mixg, w_all).reshape(t_seq, nb, N_Z)
    h_s, c_s, n_s, m_s = _sample_mlstm(z, bg, state_C, state_n, state_m, t_seq)
    x1s, conv_t = _sample_mixer_tail(xs, z, h_s, jnp.swapaxes(state_conv, 1, 2), hg, wmo, cw, cb, lng, lnb,
                                     wco, wout)
    conv_s = jnp.swapaxes(conv_t, 1, 2)
    y_t, ffn_s = _sample_ffn(x1s, state_ffn, fg, wup, fw, fb, wdown, fing)
    y_s = jnp.swapaxes(y_t, 0, 1)

    return (y_p, y_s,
            c_p, n_p, m_p.reshape(1, -1, N_HEADS), conv_p, ffn_p,
            c_s, n_s, m_s, conv_s, ffn_s)
```

```python
import functools

import jax
import jax.numpy as jnp
from jax import lax
from jax.experimental import pallas as pl
from jax.experimental.pallas import tpu as pltpu

F32 = jnp.float32
BF16 = jnp.bfloat16

D_MODEL = 1024
N_HEADS = 4
DH = D_MODEL // N_HEADS
CONV_K = 31
D_FF = 2816
FFN_K = 3
EPS = 1e-6

LANES = 128
Q0, K0, V0, O0, GV0, GG0, GA0, GB0 = (i * D_MODEL for i in range(8))
IG0 = 8 * D_MODEL
N_Z = IG0 + LANES

SUBLANES = 8
CONV_LEAD = 32
FFN_LEAD = 8
PROMPT_CHUNK = 512
FFN_CHUNK = 512
SAMPLE_ROWS = 128
MLSTM_SEQS = 8
PACK_ROWS = 128
VMEM_LIMIT = 58 * 1024 * 1024


def _mm(a, b):
    return jnp.dot(a, b, preferred_element_type=F32)


def _mm_nt(a, b):
    return lax.dot_general(a, b, (((1,), (1,)), ((), ())), preferred_element_type=F32)


def _mm_tn(a, b):
    return lax.dot_general(a, b, (((0,), (0,)), ((), ())), preferred_element_type=F32)


def _rmsnorm(x, g):
    return x * lax.rsqrt(jnp.mean(x * x, axis=-1, keepdims=True) + EPS) * g


def _layernorm(x, g, b):
    mu = jnp.mean(x, axis=-1, keepdims=True)
    xc = x - mu
    return xc * lax.rsqrt(jnp.mean(xc * xc, axis=-1, keepdims=True) + EPS) * g + b


def _sigmoid(x):
    return 0.5 * jnp.tanh(0.5 * x) + 0.5


def _sigmoid_of_twice(xh):
    return 0.5 * jnp.tanh(xh) + 0.5


def _log_sigmoid(x):
    return jnp.minimum(x, 0.0) - jnp.log1p(jnp.exp(-jnp.abs(x)))


def _cumsum_rows(x):
    n = x.shape[0]
    row = lax.broadcasted_iota(jnp.int32, x.shape, 0)
    s = 1
    while s < n:
        x = x + jnp.where(row >= s, pltpu.roll(x, s, axis=0), 0.0)
        s *= 2
    return x


def _gate_columns(zg, bias):
    zgb = zg + bias
    b_all = pltpu.roll(_cumsum_rows(_log_sigmoid(zgb)), LANES - N_HEADS, axis=1)
    return b_all, zgb - b_all


def _mlstm_head_steps(q, k, v, b_col, a_col, a_row, m_prev, c_prev, n_prev, n_valid, out):
    L, LK = q.shape[0], k.shape[0]
    row = lax.broadcasted_iota(jnp.int32, (L, LK), 0)
    col = lax.broadcasted_iota(jnp.int32, (L, LK), 1)
    d = jnp.where(col <= row, b_col + a_row, -jnp.inf)
    inter = b_col + m_prev
    d_max = jnp.max(d, axis=1, keepdims=True)
    qb, kb, vb = q.astype(BF16), k.astype(BF16), v.astype(BF16)
    qk = _mm_nt(qb, kb)
    yield
    m_t = jnp.maximum(inter, d_max)
    w_inter = jnp.exp(inter - m_t)
    p = jnp.exp(d - m_t)
    s = qk * p
    qc = _mm_nt(qb, c_prev().astype(BF16))
    yield
    num = w_inter * qc + _mm(s.astype(BF16), vb)
    den = w_inter * jnp.sum(q * n_prev, axis=1, keepdims=True) + jnp.sum(s, axis=1, keepdims=True)
    m_new = m_t[n_valid - 1:n_valid, :]
    b_last = b_col[n_valid - 1:n_valid, :]
    g_inter = jnp.exp(b_last + m_prev - m_new)
    g_col = jnp.exp(b_last + a_col - m_new)
    if n_valid < LK:
        g_col = jnp.where(lax.broadcasted_iota(jnp.int32, (LK, 1), 0) < n_valid, g_col, 0.0)
    yield
    out["h"] = num * (1.0 / jnp.maximum(jnp.abs(den), jnp.exp(-m_t)))
    out["c"] = g_inter * c_prev() + _mm_tn((g_col * v).astype(BF16), kb)
    out["n"] = g_inter * n_prev + jnp.sum(g_col * k, axis=0, keepdims=True)
    out["m"] = m_new
    yield


def _mlstm_head(q, k, v, b_col, a_col, a_row, m_prev, c_prev, n_prev, n_valid):
    out = {}
    for _ in _mlstm_head_steps(q, k, v, b_col, a_col, a_row, m_prev, c_prev, n_prev, n_valid, out):
        pass
    return out["h"], out["c"], out["n"], out["m"]


def _round_robin(streams):
    streams = list(streams)
    while streams:
        alive = []
        for gen in streams:
            try:
                next(gen)
                alive.append(gen)
            except StopIteration:
                pass
        streams = alive


def _head_out(h, head_g, o):
    mu = jnp.mean(h, axis=-1, keepdims=True)
    hc = h - mu
    hn = hc * lax.rsqrt(jnp.mean(hc * hc, axis=-1, keepdims=True) + EPS)
    return hn * head_g * _sigmoid_of_twice(o)


def _causal_conv_cols(full_v, cwb_s, cb_ref, out_v, cs, T):
    off = CONV_LEAD - (CONV_K - 1)
    n_a = (off + CONV_K - 1) // SUBLANES + 1
    width = cs.stop - cs.start
    sub = lax.broadcasted_iota(jnp.int32, (SUBLANES, width), 0)
    bias = cb_ref[:, cs]
    tiles = {}

    def tile(i):
        if i not in tiles:
            tiles[i] = full_v[SUBLANES * i:SUBLANES * (i + 1), :]
        return tiles[i]

    prev = None
    for i in range(T // SUBLANES + 1):
        ys = []
        for r in range(SUBLANES):
            acc = None
            for a in range(n_a):
                j = SUBLANES * a + r - off
                if 0 <= j < CONV_K:
                    term = cwb_s[j, :, cs] * tile(i + a)
                    acc = term if acc is None else acc + term
            ys.append(acc)
        tiles.pop(i, None)
        cur = [ys[0]] + [pltpu.roll(ys[r], SUBLANES - r, axis=0) for r in range(1, SUBLANES)]
        if prev is not None:
            out = prev[0] + bias
            for r in range(1, SUBLANES):
                out = out + jnp.where(sub < SUBLANES - r, prev[r], cur[r])
            out_v[SUBLANES * (i - 1):SUBLANES * i, :] = out
        prev = cur


def _const_spec(shape):
    nd = len(shape)
    return pl.BlockSpec(shape, lambda *_: (0,) * nd, pipeline_mode=pl.Buffered(1))


def _pack_kernel(w_ref, o_ref):
    split = 4 * D_MODEL
    n_gate = 2 * N_HEADS
    rows = w_ref.shape[1]
    for src, dst in ((0, 0), (split + n_gate, split)):
        for t in range(split // LANES):
            blk = w_ref[src + t * LANES:src + (t + 1) * LANES, :]
            scale = 0.5 if dst + t * LANES >= O0 else 1.0
            o_ref[:, dst + t * LANES:dst + (t + 1) * LANES] = (blk.T * scale).astype(BF16)
    g = jnp.concatenate([w_ref[split:split + n_gate, :], jnp.zeros((LANES - n_gate, rows), F32)], axis=0).T
    o_ref[:, IG0:IG0 + LANES] = g.astype(BF16)


def _pack_w_in(w_t):
    _, n_in, d = w_t.shape
    return pl.pallas_call(
        _pack_kernel,
        grid=(d // PACK_ROWS,),
        in_specs=[pl.BlockSpec((None, n_in, PACK_ROWS), lambda i: (0, 0, i))],
        out_specs=pl.BlockSpec((PACK_ROWS, N_Z), lambda i: (i, 0)),
        out_shape=jax.ShapeDtypeStruct((d, N_Z), BF16),
        compiler_params=pltpu.CompilerParams(
            dimension_semantics=("arbitrary",), vmem_limit_bytes=VMEM_LIMIT),
        name="pack_w_in",
    )(w_t)


def _prompt_mixer_kernel(x_ref, mixg_ref, w_ref, bg_ref, hg_ref, wmo_ref, cw_ref, cb_ref,
                         lng_ref, lnb_ref, wco_ref, wout_ref,
                         x1_ref, c_ref, n_ref, m_ref, conv_ref,
                         xn_s, ha_s, ga_s, gb_s, full_s, uc_s, m_s, cwb_s, *, n_chunks):
    T = x_ref.shape[1]
    c = pl.program_id(1)

    @pl.when(c == 0)
    def _():
        c_ref[...] = jnp.zeros_like(c_ref)
        n_ref[...] = jnp.zeros_like(n_ref)
        m_s[...] = jnp.zeros_like(m_s)
        full_s[0:CONV_LEAD, :] = jnp.zeros((CONV_LEAD, D_MODEL), F32)
        full_s[CONV_LEAD + T:CONV_LEAD + T + SUBLANES, :] = jnp.zeros((SUBLANES, D_MODEL), F32)
        for j in range(CONV_K):
            cwb_s[j] = jnp.broadcast_to(cw_ref[j:j + 1, :], (SUBLANES, D_MODEL))

    x = x_ref[0]
    xn_s[...] = _rmsnorm(x, mixg_ref[...]).astype(BF16)
    xn = xn_s[...]

    b_all, a_all = _gate_columns(_mm(xn, w_ref[:, IG0:IG0 + LANES]), bg_ref[...])
    a_t = a_all.T

    for h in range(N_HEADS):
        cs = slice(h * DH, (h + 1) * DH)
        gv = _mm(xn, w_ref[:, GV0 + h * DH:GV0 + (h + 1) * DH])
        gg = _mm(xn, w_ref[:, GG0 + h * DH:GG0 + (h + 1) * DH])
        full_s[CONV_LEAD:CONV_LEAD + T, cs] = gv * jnp.tanh(gg) + gv
        q = _mm(xn, w_ref[:, Q0 + h * DH:Q0 + (h + 1) * DH])
        k = _mm(xn, w_ref[:, K0 + h * DH:K0 + (h + 1) * DH]) * (DH ** -0.5)
        v = _mm(xn, w_ref[:, V0 + h * DH:V0 + (h + 1) * DH])
        o = _mm(xn, w_ref[:, O0 + h * DH:O0 + (h + 1) * DH])
        _causal_conv_cols(full_s.at[:, cs], cwb_s, cb_ref, uc_s.at[:, cs], cs, T)
        hh, c_new, n_new, m_new = _mlstm_head(
            q, k, v, b_all[:, h:h + 1], a_all[:, h:h + 1], a_t[h:h + 1, :],
            m_s[h:h + 1, 0:1], functools.partial(c_ref.__getitem__, (0, h)), n_ref[0, h:h + 1, :], T)
        c_ref[0, h] = c_new
        n_ref[0, h:h + 1, :] = n_new
        m_s[h:h + 1, :] = jnp.broadcast_to(m_new, (1, LANES))
        ha_s[:, cs] = _head_out(hh, hg_ref[:, cs], o).astype(BF16)
        ga_s[:, cs] = _mm(xn, w_ref[:, GA0 + h * DH:GA0 + (h + 1) * DH])
        gb_s[:, cs] = _mm(xn, w_ref[:, GB0 + h * DH:GB0 + (h + 1) * DH])

    y_a = _mm(ha_s[...], wmo_ref[...])
    tail = full_s[T + CONV_LEAD - (CONV_K - 1):T + CONV_LEAD, :]
    conv_ref[0] = tail
    full_s[CONV_LEAD - (CONV_K - 1):CONV_LEAD, :] = tail

    ucn = _layernorm(uc_s[...], lng_ref[...], lnb_ref[...])
    y_b = _mm((ucn * _sigmoid(ucn)).astype(BF16), wco_ref[...])

    mix = _sigmoid_of_twice(ga_s[...]) * y_a + _sigmoid_of_twice(gb_s[...]) * y_b
    x1_ref[0] = x + _mm(mix.astype(BF16), wout_ref[...])

    @pl.when(c == n_chunks - 1)
    def _():
        for h in range(N_HEADS):
            m_ref[0, :, h:h + 1] = m_s[h:h + 1, 0:1]


def _prompt_mixer(x, mixg, w_all, bg, hg, wmo, cw, cb, lng, lnb, wco, wout):
    B, S, D = x.shape
    T = PROMPT_CHUNK
    nc = S // T
    kern = functools.partial(_prompt_mixer_kernel, n_chunks=nc)
    return pl.pallas_call(
        kern,
        grid=(B, nc),
        in_specs=[
            pl.BlockSpec((1, T, D), lambda b, c: (b, c, 0)),
            _const_spec((1, D)), _const_spec((D, N_Z)), _const_spec((1, LANES)),
            _const_spec((1, D)), _const_spec((D, D)), _const_spec((CONV_K, D)), _const_spec((1, D)),
            _const_spec((1, D)), _const_spec((1, D)), _const_spec((D, D)), _const_spec((D, D)),
        ],
        out_specs=[
            pl.BlockSpec((1, T, D), lambda b, c: (b, c, 0)),
            pl.BlockSpec((None, 1, N_HEADS, DH, DH), lambda b, c: (0, b, 0, 0, 0)),
            pl.BlockSpec((None, 1, N_HEADS, DH), lambda b, c: (0, b, 0, 0)),
            pl.BlockSpec((1, 1, N_HEADS), lambda b, c: (b, 0, 0)),
            pl.BlockSpec((None, 1, CONV_K - 1, D), lambda b, c: (0, b, 0, 0)),
        ],
        out_shape=[
            jax.ShapeDtypeStruct((B, S, D), F32),
            jax.ShapeDtypeStruct((1, B, N_HEADS, DH, DH), F32),
            jax.ShapeDtypeStruct((1, B, N_HEADS, DH), F32),
            jax.ShapeDtypeStruct((B, 1, N_HEADS), F32),
            jax.ShapeDtypeStruct((1, B, CONV_K - 1, D), F32),
        ],
        scratch_shapes=[
            pltpu.VMEM((T, D), BF16), pltpu.VMEM((T, D), BF16),
            pltpu.VMEM((T, D), F32), pltpu.VMEM((T, D), F32),
            pltpu.VMEM((T + CONV_LEAD + SUBLANES, D), F32), pltpu.VMEM((T, D), F32),
            pltpu.VMEM((SUBLANES, LANES), F32), pltpu.VMEM((CONV_K, SUBLANES, D), F32),
        ],
        compiler_params=pltpu.CompilerParams(
            dimension_semantics=("arbitrary", "arbitrary"), vmem_limit_bytes=VMEM_LIMIT),
        name="prompt_mixer",
    )(x, mixg, w_all, bg, hg, wmo, cw, cb, lng, lnb, wco, wout)


def _ffn_tail(upc_a, upc_g, wdown_ref, x, fing_ref):
    act = (upc_a * _sigmoid(upc_a) * upc_g).astype(BF16)
    x2 = x + _mm(act, wdown_ref[...])
    return _rmsnorm(x2, fing_ref[...])


def _prompt_ffn_kernel(x_ref, fg_ref, wup_ref, fw_ref, fb_ref, wdown_ref, fing_ref,
                       y_ref, ffn_ref, up_s, act_s):
    T = x_ref.shape[1]
    lo = FFN_LEAD - (FFN_K - 1)
    c = pl.program_id(1)

    @pl.when(c == 0)
    def _():
        up_s[0:FFN_LEAD, :] = jnp.zeros((FFN_LEAD, 2 * D_FF), F32)

    x = x_ref[0]
    hf = _rmsnorm(x, fg_ref[...]).astype(BF16)
    nb = 2 * D_FF // DH
    for blk in range(nb):
        cs = slice(blk * DH, (blk + 1) * DH)
        up_s[FFN_LEAD:FFN_LEAD + T, cs] = _mm(hf, wup_ref[:, cs])

    def conv(cs):
        return (fb_ref[:, cs] + fw_ref[0:1, cs] * up_s[lo:lo + T, cs]
                + fw_ref[1:2, cs] * up_s[lo + 1:lo + 1 + T, cs]
                + fw_ref[2:3, cs] * up_s[lo + 2:lo + 2 + T, cs])

    for blk in range(D_FF // DH):
        ca = slice(blk * DH, (blk + 1) * DH)
        a = conv(ca)
        g = conv(slice(D_FF + blk * DH, D_FF + (blk + 1) * DH))
        act_s[:, ca] = (a * _sigmoid(a) * g).astype(BF16)
    tail = up_s[T + lo:T + FFN_LEAD, :]
    ffn_ref[0] = tail
    up_s[lo:FFN_LEAD, :] = tail
    x2 = x + _mm(act_s[...], wdown_ref[...])
    y_ref[0] = _rmsnorm(x2, fing_ref[...])


def _prompt_ffn(x1, fg, wup, fw, fb, wdown, fing):
    B, S, D = x1.shape
    T = FFN_CHUNK
    nc = S // T
    return pl.pallas_call(
        _prompt_ffn_kernel,
        grid=(B, nc),
        in_specs=[
            pl.BlockSpec((1, T, D), lambda b, c: (b, c, 0)),
            _const_spec((1, D)), _const_spec((D, 2 * D_FF)), _const_spec((FFN_K, 2 * D_FF)),
            _const_spec((1, 2 * D_FF)), _const_spec((D_FF, D)), _const_spec((1, D)),
        ],
        out_specs=[
            pl.BlockSpec((1, T, D), lambda b, c: (b, c, 0)),
            pl.BlockSpec((None, 1, FFN_K - 1, 2 * D_FF), lambda b, c: (0, b, 0, 0)),
        ],
        out_shape=[
            jax.ShapeDtypeStruct((B, S, D), F32),
            jax.ShapeDtypeStruct((1, B, FFN_K - 1, 2 * D_FF), F32),
        ],
        scratch_shapes=[pltpu.VMEM((T + FFN_LEAD, 2 * D_FF), F32), pltpu.VMEM((T, D_FF), BF16)],
        compiler_params=pltpu.CompilerParams(
            dimension_semantics=("arbitrary", "arbitrary"), vmem_limit_bytes=VMEM_LIMIT),
        name="prompt_ffn",
    )(x1, fg, wup, fw, fb, wdown, fing)


def _sample_proj_kernel(x_ref, mixg_ref, w_ref, z_ref):
    xn = _rmsnorm(x_ref[...], mixg_ref[...]).astype(BF16)
    for c0 in range(0, N_Z, DH):
        cs = slice(c0, min(c0 + DH, N_Z))
        z_ref[:, cs] = _mm(xn, w_ref[:, cs])


def _sample_proj(x, mixg, w_all):
    R, D = x.shape
    return pl.pallas_call(
        _sample_proj_kernel,
        grid=(R // SAMPLE_ROWS,),
        in_specs=[pl.BlockSpec((SAMPLE_ROWS, D), lambda i: (i, 0)),
                  _const_spec((1, D)), _const_spec((D, N_Z))],
        out_specs=pl.BlockSpec((SAMPLE_ROWS, N_Z), lambda i: (i, 0)),
        out_shape=jax.ShapeDtypeStruct((R, N_Z), F32),
        compiler_params=pltpu.CompilerParams(
            dimension_semantics=("arbitrary",), vmem_limit_bytes=VMEM_LIMIT),
        name="sample_proj",
    )(x, mixg, w_all)


def _sample_mlstm_kernel(zq_ref, zg_ref, bg_ref, c_ref, n_ref, m_ref,
                         h_ref, co_ref, no_ref, mo_ref, seq_s, *, t_seq):
    n_seq = zq_ref.shape[1]
    pad = LANES - SUBLANES
    wq = zq_ref.shape[2]
    seq_s[...] = jnp.zeros_like(seq_s)
    for j in range(n_seq):
        for t in range(t_seq):
            seq_s[j, t:t + 1, 0:wq] = zq_ref[t, j:j + 1, :]
            seq_s[j, t:t + 1, wq:] = zg_ref[t, j:j + 1, :]
    items = []
    for j in range(n_seq):
        zq = seq_s[j, :, 0:wq]
        b_all, a_seq = _gate_columns(seq_s[j, :, wq:], bg_ref[...])
        a_all = jnp.concatenate([a_seq, jnp.zeros((pad, LANES), F32)], axis=0)
        a_t = a_all.T
        for h in range(N_HEADS):
            q = zq[:, Q0 + h * DH:Q0 + (h + 1) * DH]
            k = zq[:, K0 + h * DH:K0 + (h + 1) * DH] * (DH ** -0.5)
            v = zq[:, V0 + h * DH:V0 + (h + 1) * DH]
            zeros = jnp.zeros((pad, DH), F32)
            k = jnp.concatenate([k, zeros], axis=0)
            v = jnp.concatenate([v, zeros], axis=0)
            out = {}
            steps = _mlstm_head_steps(
                q, k, v, b_all[:, h:h + 1], a_all[:, h:h + 1], a_t[h:h + 1, :],
                m_ref[j:j + 1, h:h + 1], functools.partial(c_ref.__getitem__, (j, h)),
                n_ref[j, h:h + 1, :], t_seq, out)
            items.append((j, h, out, steps))
    _round_robin(steps for _, _, _, steps in items)
    for j, h, out, _ in items:
        co_ref[j, h] = out["c"]
        no_ref[j, h:h + 1, :] = out["n"]
        mo_ref[j:j + 1, h:h + 1] = out["m"]
        for t in range(t_seq):
            h_ref[t, j:j + 1, h * DH:(h + 1) * DH] = out["h"][t:t + 1, :]


def _sample_mlstm(z, bg, c0, n0, m0, t_seq):
    nb = c0.shape[1]
    sb = MLSTM_SEQS
    kern = functools.partial(_sample_mlstm_kernel, t_seq=t_seq)
    c_spec = pl.BlockSpec((None, sb, N_HEADS, DH, DH), lambda i: (0, i, 0, 0, 0))
    n_spec = pl.BlockSpec((None, sb, N_HEADS, DH), lambda i: (0, i, 0, 0))
    m_spec = pl.BlockSpec((None, sb, N_HEADS), lambda i: (0, i, 0))
    return pl.pallas_call(
        kern,
        grid=(nb // sb,),
        in_specs=[
            pl.BlockSpec((t_seq, sb, 3 * D_MODEL), lambda i: (0, i, 0)),
            pl.BlockSpec((t_seq, sb, LANES), lambda i: (0, i, IG0 // LANES)),
            _const_spec((1, LANES)),
            c_spec, n_spec, m_spec,
        ],
        out_specs=[pl.BlockSpec((t_seq, sb, D_MODEL), lambda i: (0, i, 0)), c_spec, n_spec, m_spec],
        out_shape=[
            jax.ShapeDtypeStruct((t_seq, nb, D_MODEL), F32),
            jax.ShapeDtypeStruct(c0.shape, F32),
            jax.ShapeDtypeStruct(n0.shape, F32),
            jax.ShapeDtypeStruct(m0.shape, F32),
        ],
        scratch_shapes=[pltpu.VMEM((sb, SUBLANES, 3 * D_MODEL + LANES), F32)],
        compiler_params=pltpu.CompilerParams(
            dimension_semantics=("arbitrary",), vmem_limit_bytes=VMEM_LIMIT),
        name="sample_mlstm",
    )(z, z, bg, c0, n0, m0)


def _rows(ref):
    return jnp.concatenate([ref[t] for t in range(ref.shape[0])], axis=0)


def _sample_mixer_tail_kernel(x_ref, z_ref, h_ref, cst_ref, hg_ref, wmo_ref, cw_ref, cb_ref,
                              lng_ref, lnb_ref, wco_ref, wout_ref,
                              x1_ref, cnew_ref, uc_s):
    t_seq, n_seq, _ = x_ref.shape
    hist = CONV_K - 1
    z = _rows(z_ref)
    hcat = _rows(h_ref)
    ha = jnp.concatenate(
        [_head_out(hcat[:, h * DH:(h + 1) * DH], hg_ref[:, h * DH:(h + 1) * DH],
                   z[:, O0 + h * DH:O0 + (h + 1) * DH]) for h in range(N_HEADS)], axis=1)
    y_a = _mm(ha.astype(BF16), wmo_ref[...])

    for t in range(t_seq):
        gv = z_ref[t, :, GV0:GV0 + D_MODEL]
        uc_s[t] = gv * jnp.tanh(z_ref[t, :, GG0:GG0 + D_MODEL]) + gv

    def plane(r):
        return cst_ref[r] if r < hist else uc_s[r - hist]

    outs = []
    for t in range(t_seq):
        acc = cb_ref[...] + cw_ref[0:1, :] * plane(t)
        for j in range(1, CONV_K):
            acc = acc + cw_ref[j:j + 1, :] * plane(t + j)
        outs.append(acc)
    for r in range(hist):
        cnew_ref[r] = plane(r + t_seq)
    uc = jnp.concatenate(outs, axis=0)

    ucn = _layernorm(uc, lng_ref[...], lnb_ref[...])
    y_b = _mm((ucn * _sigmoid(ucn)).astype(BF16), wco_ref[...])
    mix = _sigmoid_of_twice(z[:, GA0:GA0 + D_MODEL]) * y_a + _sigmoid_of_twice(z[:, GB0:GB0 + D_MODEL]) * y_b
    x1 = _rows(x_ref) + _mm(mix.astype(BF16), wout_ref[...])
    for t in range(t_seq):
        x1_ref[t] = x1[t * n_seq:(t + 1) * n_seq, :]


def _sample_mixer_tail(x, z, h, cst, hg, wmo, cw, cb, lng, lnb, wco, wout):
    t_seq, nb, D = x.shape
    sb = SAMPLE_ROWS // t_seq
    hist = CONV_K - 1
    st_spec = pl.BlockSpec((None, hist, sb, D), lambda i: (0, 0, i, 0))
    return pl.pallas_call(
        _sample_mixer_tail_kernel,
        grid=(nb // sb,),
        in_specs=[
            pl.BlockSpec((t_seq, sb, D), lambda i: (0, i, 0)),
            pl.BlockSpec((t_seq, sb, N_Z), lambda i: (0, i, 0)),
            pl.BlockSpec((t_seq, sb, D), lambda i: (0, i, 0)),
            st_spec,
            _const_spec((1, D)), _const_spec((D, D)), _const_spec((CONV_K, D)), _const_spec((1, D)),
            _const_spec((1, D)), _const_spec((1, D)), _const_spec((D, D)), _const_spec((D, D)),
        ],
        out_specs=[pl.BlockSpec((t_seq, sb, D), lambda i: (0, i, 0)), st_spec],
        out_shape=[
            jax.ShapeDtypeStruct((t_seq, nb, D), F32),
            jax.ShapeDtypeStruct(cst.shape, F32),
        ],
        scratch_shapes=[pltpu.VMEM((t_seq, sb, D), F32)],
        compiler_params=pltpu.CompilerParams(
            dimension_semantics=("arbitrary",), vmem_limit_bytes=VMEM_LIMIT),
        name="sample_mixer_tail",
    )(x, z, h, cst, hg, wmo, cw, cb, lng, lnb, wco, wout)


def _sample_ffn_kernel(x_ref, fst_ref, fg_ref, wup_ref, fw_ref, fb_ref, wdown_ref, fing_ref,
                       y_ref, fnew_ref, up_s, upc_s):
    t_seq, n_seq, _ = x_ref.shape
    hist = FFN_K - 1
    x = _rows(x_ref)
    hf = _rmsnorm(x, fg_ref[...]).astype(BF16)
    for blk in range(2 * D_FF // DH):
        cs = slice(blk * DH, (blk + 1) * DH)
        up = _mm(hf, wup_ref[:, cs])
        for t in range(t_seq):
            up_s[hist + t, :, cs] = up[t * n_seq:(t + 1) * n_seq, :]
    for j in range(n_seq):
        for r in range(hist):
            up_s[r, j:j + 1, :] = fst_ref[j, r:r + 1, :]
    for t in range(t_seq):
        acc = fb_ref[...] + fw_ref[0:1, :] * up_s[t]
        for kk in range(1, FFN_K):
            acc = acc + fw_ref[kk:kk + 1, :] * up_s[t + kk]
        upc_s[t] = acc
    for j in range(n_seq):
        for r in range(hist):
            fnew_ref[j, r:r + 1, :] = up_s[t_seq + r, j:j + 1, :]
    upc = _rows(upc_s)
    y = _ffn_tail(upc[:, 0:D_FF], upc[:, D_FF:2 * D_FF], wdown_ref, x, fing_ref)
    for t in range(t_seq):
        y_ref[t] = y[t * n_seq:(t + 1) * n_seq, :]


def _sample_ffn(x1, fst, fg, wup, fw, fb, wdown, fing):
    t_seq, nb, D = x1.shape
    sb = SAMPLE_ROWS // t_seq
    hist = FFN_K - 1
    st_spec = pl.BlockSpec((None, sb, hist, 2 * D_FF), lambda i: (0, i, 0, 0))
    return pl.pallas_call(
        _sample_ffn_kernel,
        grid=(nb // sb,),
        in_specs=[
            pl.BlockSpec((t_seq, sb, D), lambda i: (0, i, 0)),
            st_spec,
            _const_spec((1, D)), _const_spec((D, 2 * D_FF)), _const_spec((FFN_K, 2 * D_FF)),
            _const_spec((1, 2 * D_FF)), _const_spec((D_FF, D)), _const_spec((1, D)),
        ],
        out_specs=[pl.BlockSpec((t_seq, sb, D), lambda i: (0, i, 0)), st_spec],
        out_shape=[
            jax.ShapeDtypeStruct((t_seq, nb, D), F32),
            jax.ShapeDtypeStruct(fst.shape, F32),
        ],
        scratch_shapes=[pltpu.VMEM((t_seq + hist, sb, 2 * D_FF), F32), pltpu.VMEM((t_seq, sb, 2 * D_FF), F32)],
        compiler_params=pltpu.CompilerParams(
            dimension_semantics=("arbitrary",), vmem_limit_bytes=VMEM_LIMIT),
        name="sample_ffn",
    )(x1, fst, fg, wup, fw, fb, wdown, fing)


def _pad_gate_bias(b):
    return jnp.pad(b, (0, LANES - b.shape[0])).reshape(1, LANES)


def kernel(x_prompt, x_sample, state_C, state_n, state_m, state_conv, state_ffn, mix_norm_g, w_in, b_if,
           head_norm_g, w_mlstm_out, conv_w, conv_b, conv_ln_g, conv_ln_b, w_conv_out, w_out, ffn_norm_g,
           w_up, ffn_conv_w, ffn_conv_b, w_down, final_norm_g):
    depth = w_in.shape[0]
    assert depth == 1, "single-layer trunk"
    l = 0
    row = lambda a: a.reshape(1, -1)
    w_all = _pack_w_in(jnp.swapaxes(w_in, 1, 2))
    bg = _pad_gate_bias(b_if[l])
    mixg, hg = row(mix_norm_g[l]), row(head_norm_g[l])
    wmo, wco, wout = (w.astype(BF16) for w in (w_mlstm_out[l], w_conv_out[l], w_out[l]))
    cw, cb = conv_w[l], row(conv_b[l])
    lng, lnb = row(conv_ln_g[l]), row(conv_ln_b[l])
    fg, fing = row(ffn_norm_g[l]), row(final_norm_g)
    wup, wdown = w_up[l].astype(BF16), w_down[l].astype(BF16)
    fw, fb = ffn_conv_w[l], row(ffn_conv_b[l])

    x1p, c_p, n_p, m_p, conv_p = _prompt_mixer(x_prompt, mixg, w_all, bg, hg, wmo, cw, cb, lng, lnb, wco, wout)
    y_p, ffn_p = _prompt_ffn(x1p, fg, wup, fw, fb, wdown, fing)

    nb, t_seq, d = x_sample.shape
    xs = jnp.swapaxes(x_sample, 0, 1)
    z = _sample_proj(xs.reshape(t_seq * nb, d), mixg, w_all).reshape(t_seq, nb, N_Z)
    h_s, c_s, n_s, m_s = _sample_mlstm(z, bg, state_C, state_n, state_m, t_seq)
    x1s, conv_t = _sample_mixer_tail(xs, z, h_s, jnp.swapaxes(state_conv, 1, 2), hg, wmo, cw, cb, lng, lnb,
                                     wco, wout)
    conv_s = jnp.swapaxes(conv_t, 1, 2)
    y_t, ffn_s = _sample_ffn(x1s, state_ffn, fg, wup, fw, fb, wdown, fing)
    y_s = jnp.swapaxes(y_t, 0, 1)

    return (y_p, y_s,
            c_p, n_p, m_p.reshape(1, -1, N_HEADS), conv_p, ffn_p,
            c_s, n_s, m_s, conv_s, ffn_s)
```

```python
import functools

import jax
import jax.numpy as jnp
from jax import lax
from jax.experimental import pallas as pl
from jax.experimental.pallas import tpu as pltpu

F32 = jnp.float32
BF16 = jnp.bfloat16

D_MODEL = 1024
N_HEADS = 4
DH = D_MODEL // N_HEADS
CONV_K = 31
D_FF = 2816
FFN_K = 3
EPS = 1e-6

LANES = 128
Q0, K0, V0, O0, GV0, GG0, GA0, GB0 = (i * D_MODEL for i in range(8))
IG0 = 8 * D_MODEL
N_Z = IG0 + LANES

SUBLANES = 8
CONV_LEAD = 32
FFN_LEAD = 8
PROMPT_CHUNK = 512
FFN_CHUNK = 512
SAMPLE_ROWS = 128
MLSTM_SEQS = 8
MLSTM_RING = 3
PACK_ROWS = 128
VMEM_LIMIT = 58 * 1024 * 1024


def _mm(a, b):
    return jnp.dot(a, b, preferred_element_type=F32)


def _mm_nt(a, b):
    return lax.dot_general(a, b, (((1,), (1,)), ((), ())), preferred_element_type=F32)


def _mm_tn(a, b):
    return lax.dot_general(a, b, (((0,), (0,)), ((), ())), preferred_element_type=F32)


def _rmsnorm(x, g):
    return x * lax.rsqrt(jnp.mean(x * x, axis=-1, keepdims=True) + EPS) * g


def _layernorm(x, g, b):
    mu = jnp.mean(x, axis=-1, keepdims=True)
    xc = x - mu
    return xc * lax.rsqrt(jnp.mean(xc * xc, axis=-1, keepdims=True) + EPS) * g + b


def _sigmoid(x):
    return 0.5 * jnp.tanh(0.5 * x) + 0.5


def _sigmoid_of_twice(xh):
    return 0.5 * jnp.tanh(xh) + 0.5


def _log_sigmoid(x):
    return jnp.minimum(x, 0.0) - jnp.log1p(jnp.exp(-jnp.abs(x)))


def _cumsum_rows(x):
    n = x.shape[0]
    row = lax.broadcasted_iota(jnp.int32, x.shape, 0)
    s = 1
    while s < n:
        x = x + jnp.where(row >= s, pltpu.roll(x, s, axis=0), 0.0)
        s *= 2
    return x


def _gate_columns(zg, bias):
    zgb = zg + bias
    b_all = pltpu.roll(_cumsum_rows(_log_sigmoid(zgb)), LANES - N_HEADS, axis=1)
    return b_all, zgb - b_all


def _mlstm_head_steps(q, k, v, b_col, a_col, a_row, m_prev, c_prev, n_prev, n_valid, out):
    L, LK = q.shape[0], k.shape[0]
    row = lax.broadcasted_iota(jnp.int32, (L, LK), 0)
    col = lax.broadcasted_iota(jnp.int32, (L, LK), 1)
    d = jnp.where(col <= row, b_col + a_row, -jnp.inf)
    inter = b_col + m_prev
    d_max = jnp.max(d, axis=1, keepdims=True)
    qb, kb, vb = q.astype(BF16), k.astype(BF16), v.astype(BF16)
    qk = _mm_nt(qb, kb)
    yield
    m_t = jnp.maximum(inter, d_max)
    w_inter = jnp.exp(inter - m_t)
    p = jnp.exp(d - m_t)
    s = qk * p
    qc = _mm_nt(qb, c_prev().astype(BF16))
    yield
    num = w_inter * qc + _mm(s.astype(BF16), vb)
    den = w_inter * jnp.sum(q * n_prev, axis=1, keepdims=True) + jnp.sum(s, axis=1, keepdims=True)
    m_new = m_t[n_valid - 1:n_valid, :]
    b_last = b_col[n_valid - 1:n_valid, :]
    g_inter = jnp.exp(b_last + m_prev - m_new)
    g_col = jnp.exp(b_last + a_col - m_new)
    if n_valid < LK:
        g_col = jnp.where(lax.broadcasted_iota(jnp.int32, (LK, 1), 0) < n_valid, g_col, 0.0)
    yield
    out["h"] = num * (1.0 / jnp.maximum(jnp.abs(den), jnp.exp(-m_t)))
    out["c"] = g_inter * c_prev() + _mm_tn((g_col * v).astype(BF16), kb)
    out["n"] = g_inter * n_prev + jnp.sum(g_col * k, axis=0, keepdims=True)
    out["m"] = m_new
    yield


def _mlstm_head(q, k, v, b_col, a_col, a_row, m_prev, c_prev, n_prev, n_valid):
    out = {}
    for _ in _mlstm_head_steps(q, k, v, b_col, a_col, a_row, m_prev, c_prev, n_prev, n_valid, out):
        pass
    return out["h"], out["c"], out["n"], out["m"]


def _round_robin(streams):
    streams = list(streams)
    while streams:
        alive = []
        for gen in streams:
            try:
                next(gen)
                alive.append(gen)
            except StopIteration:
                pass
        streams = alive


def _head_out(h, head_g, o):
    mu = jnp.mean(h, axis=-1, keepdims=True)
    hc = h - mu
    hn = hc * lax.rsqrt(jnp.mean(hc * hc, axis=-1, keepdims=True) + EPS)
    return hn * head_g * _sigmoid_of_twice(o)


def _causal_conv_cols(full_v, cwb_s, cb_ref, out_v, cs, T):
    off = CONV_LEAD - (CONV_K - 1)
    n_a = (off + CONV_K - 1) // SUBLANES + 1
    width = cs.stop - cs.start
    sub = lax.broadcasted_iota(jnp.int32, (SUBLANES, width), 0)
    bias = cb_ref[:, cs]
    tiles = {}

    def tile(i):
        if i not in tiles:
            tiles[i] = full_v[SUBLANES * i:SUBLANES * (i + 1), :]
        return tiles[i]

    prev = None
    for i in range(T // SUBLANES + 1):
        ys = []
        for r in range(SUBLANES):
            acc = None
            for a in range(n_a):
                j = SUBLANES * a + r - off
                if 0 <= j < CONV_K:
                    term = cwb_s[j, :, cs] * tile(i + a)
                    acc = term if acc is None else acc + term
            ys.append(acc)
        tiles.pop(i, None)
        cur = [ys[0]] + [pltpu.roll(ys[r], SUBLANES - r, axis=0) for r in range(1, SUBLANES)]
        if prev is not None:
            out = prev[0] + bias
            for r in range(1, SUBLANES):
                out = out + jnp.where(sub < SUBLANES - r, prev[r], cur[r])
            out_v[SUBLANES * (i - 1):SUBLANES * i, :] = out
        prev = cur


def _const_spec(shape):
    nd = len(shape)
    return pl.BlockSpec(shape, lambda *_: (0,) * nd, pipeline_mode=pl.Buffered(1))


def _pack_kernel(w_ref, o_ref):
    split = 4 * D_MODEL
    n_gate = 2 * N_HEADS
    rows = w_ref.shape[1]
    for src, dst in ((0, 0), (split + n_gate, split)):
        for t in range(split // LANES):
            blk = w_ref[src + t * LANES:src + (t + 1) * LANES, :]
            scale = 0.5 if dst + t * LANES >= O0 else 1.0
            o_ref[:, dst + t * LANES:dst + (t + 1) * LANES] = (blk.T * scale).astype(BF16)
    g = jnp.concatenate([w_ref[split:split + n_gate, :], jnp.zeros((LANES - n_gate, rows), F32)], axis=0).T
    o_ref[:, IG0:IG0 + LANES] = g.astype(BF16)


def _pack_w_in(w_t):
    _, n_in, d = w_t.shape
    return pl.pallas_call(
        _pack_kernel,
        grid=(d // PACK_ROWS,),
        in_specs=[pl.BlockSpec((None, n_in, PACK_ROWS), lambda i: (0, 0, i))],
        out_specs=pl.BlockSpec((PACK_ROWS, N_Z), lambda i: (i, 0)),
        out_shape=jax.ShapeDtypeStruct((d, N_Z), BF16),
        compiler_params=pltpu.CompilerParams(
            dimension_semantics=("arbitrary",), vmem_limit_bytes=VMEM_LIMIT),
        name="pack_w_in",
    )(w_t)


def _prompt_mixer_kernel(x_ref, mixg_ref, w_ref, bg_ref, hg_ref, wmo_ref, cw_ref, cb_ref,
                         lng_ref, lnb_ref, wco_ref, wout_ref,
                         x1_ref, c_ref, n_ref, m_ref, conv_ref,
                         xn_s, ha_s, ga_s, gb_s, full_s, uc_s, m_s, cwb_s, *, n_chunks):
    T = x_ref.shape[1]
    c = pl.program_id(1)

    @pl.when(c == 0)
    def _():
        c_ref[...] = jnp.zeros_like(c_ref)
        n_ref[...] = jnp.zeros_like(n_ref)
        m_s[...] = jnp.zeros_like(m_s)
        full_s[0:CONV_LEAD, :] = jnp.zeros((CONV_LEAD, D_MODEL), F32)
        full_s[CONV_LEAD + T:CONV_LEAD + T + SUBLANES, :] = jnp.zeros((SUBLANES, D_MODEL), F32)
        for j in range(CONV_K):
            cwb_s[j] = jnp.broadcast_to(cw_ref[j:j + 1, :], (SUBLANES, D_MODEL))

    x = x_ref[0]
    xn_s[...] = _rmsnorm(x, mixg_ref[...]).astype(BF16)
    xn = xn_s[...]

    b_all, a_all = _gate_columns(_mm(xn, w_ref[:, IG0:IG0 + LANES]), bg_ref[...])
    a_t = a_all.T

    for h in range(N_HEADS):
        cs = slice(h * DH, (h + 1) * DH)
        gv = _mm(xn, w_ref[:, GV0 + h * DH:GV0 + (h + 1) * DH])
        gg = _mm(xn, w_ref[:, GG0 + h * DH:GG0 + (h + 1) * DH])
        full_s[CONV_LEAD:CONV_LEAD + T, cs] = gv * jnp.tanh(gg) + gv
        q = _mm(xn, w_ref[:, Q0 + h * DH:Q0 + (h + 1) * DH])
        k = _mm(xn, w_ref[:, K0 + h * DH:K0 + (h + 1) * DH]) * (DH ** -0.5)
        v = _mm(xn, w_ref[:, V0 + h * DH:V0 + (h + 1) * DH])
        o = _mm(xn, w_ref[:, O0 + h * DH:O0 + (h + 1) * DH])
        _causal_conv_cols(full_s.at[:, cs], cwb_s, cb_ref, uc_s.at[:, cs], cs, T)
        hh, c_new, n_new, m_new = _mlstm_head(
            q, k, v, b_all[:, h:h + 1], a_all[:, h:h + 1], a_t[h:h + 1, :],
            m_s[h:h + 1, 0:1], functools.partial(c_ref.__getitem__, (0, h)), n_ref[0, h:h + 1, :], T)
        c_ref[0, h] = c_new
        n_ref[0, h:h + 1, :] = n_new
        m_s[h:h + 1, :] = jnp.broadcast_to(m_new, (1, LANES))
        ha_s[:, cs] = _head_out(hh, hg_ref[:, cs], o).astype(BF16)
        ga_s[:, cs] = _mm(xn, w_ref[:, GA0 + h * DH:GA0 + (h + 1) * DH])
        gb_s[:, cs] = _mm(xn, w_ref[:, GB0 + h * DH:GB0 + (h + 1) * DH])

    y_a = _mm(ha_s[...], wmo_ref[...])
    tail = full_s[T + CONV_LEAD - (CONV_K - 1):T + CONV_LEAD, :]
    conv_ref[0] = tail
    full_s[CONV_LEAD - (CONV_K - 1):CONV_LEAD, :] = tail

    ucn = _layernorm(uc_s[...], lng_ref[...], lnb_ref[...])
    y_b = _mm((ucn * _sigmoid(ucn)).astype(BF16), wco_ref[...])

    mix = _sigmoid_of_twice(ga_s[...]) * y_a + _sigmoid_of_twice(gb_s[...]) * y_b
    x1_ref[0] = x + _mm(mix.astype(BF16), wout_ref[...])

    @pl.when(c == n_chunks - 1)
    def _():
        for h in range(N_HEADS):
            m_ref[0, :, h:h + 1] = m_s[h:h + 1, 0:1]


def _prompt_mixer(x, mixg, w_all, bg, hg, wmo, cw, cb, lng, lnb, wco, wout):
    B, S, D = x.shape
    T = PROMPT_CHUNK
    nc = S // T
    kern = functools.partial(_prompt_mixer_kernel, n_chunks=nc)
    return pl.pallas_call(
        kern,
        grid=(B, nc),
        in_specs=[
            pl.BlockSpec((1, T, D), lambda b, c: (b, c, 0)),
            _const_spec((1, D)), _const_spec((D, N_Z)), _const_spec((1, LANES)),
            _const_spec((1, D)), _const_spec((D, D)), _const_spec((CONV_K, D)), _const_spec((1, D)),
            _const_spec((1, D)), _const_spec((1, D)), _const_spec((D, D)), _const_spec((D, D)),
        ],
        out_specs=[
            pl.BlockSpec((1, T, D), lambda b, c: (b, c, 0)),
            pl.BlockSpec((None, 1, N_HEADS, DH, DH), lambda b, c: (0, b, 0, 0, 0)),
            pl.BlockSpec((None, 1, N_HEADS, DH), lambda b, c: (0, b, 0, 0)),
            pl.BlockSpec((1, 1, N_HEADS), lambda b, c: (b, 0, 0)),
            pl.BlockSpec((None, 1, CONV_K - 1, D), lambda b, c: (0, b, 0, 0)),
        ],
        out_shape=[
            jax.ShapeDtypeStruct((B, S, D), F32),
            jax.ShapeDtypeStruct((1, B, N_HEADS, DH, DH), F32),
            jax.ShapeDtypeStruct((1, B, N_HEADS, DH), F32),
            jax.ShapeDtypeStruct((B, 1, N_HEADS), F32),
            jax.ShapeDtypeStruct((1, B, CONV_K - 1, D), F32),
        ],
        scratch_shapes=[
            pltpu.VMEM((T, D), BF16), pltpu.VMEM((T, D), BF16),
            pltpu.VMEM((T, D), F32), pltpu.VMEM((T, D), F32),
            pltpu.VMEM((T + CONV_LEAD + SUBLANES, D), F32), pltpu.VMEM((T, D), F32),
            pltpu.VMEM((SUBLANES, LANES), F32), pltpu.VMEM((CONV_K, SUBLANES, D), F32),
        ],
        compiler_params=pltpu.CompilerParams(
            dimension_semantics=("arbitrary", "arbitrary"), vmem_limit_bytes=VMEM_LIMIT),
        name="prompt_mixer",
    )(x, mixg, w_all, bg, hg, wmo, cw, cb, lng, lnb, wco, wout)


def _ffn_tail(upc_a, upc_g, wdown_ref, x, fing_ref):
    act = (upc_a * _sigmoid(upc_a) * upc_g).astype(BF16)
    x2 = x + _mm(act, wdown_ref[...])
    return _rmsnorm(x2, fing_ref[...])


def _prompt_ffn_kernel(x_ref, fg_ref, wup_ref, fw_ref, fb_ref, wdown_ref, fing_ref,
                       y_ref, ffn_ref, up_s, act_s):
    T = x_ref.shape[1]
    lo = FFN_LEAD - (FFN_K - 1)
    c = pl.program_id(1)

    @pl.when(c == 0)
    def _():
        up_s[0:FFN_LEAD, :] = jnp.zeros((FFN_LEAD, 2 * D_FF), F32)

    x = x_ref[0]
    hf = _rmsnorm(x, fg_ref[...]).astype(BF16)
    nb = 2 * D_FF // DH
    for blk in range(nb):
        cs = slice(blk * DH, (blk + 1) * DH)
        up_s[FFN_LEAD:FFN_LEAD + T, cs] = _mm(hf, wup_ref[:, cs])

    def conv(cs):
        return (fb_ref[:, cs] + fw_ref[0:1, cs] * up_s[lo:lo + T, cs]
                + fw_ref[1:2, cs] * up_s[lo + 1:lo + 1 + T, cs]
                + fw_ref[2:3, cs] * up_s[lo + 2:lo + 2 + T, cs])

    for blk in range(D_FF // DH):
        ca = slice(blk * DH, (blk + 1) * DH)
        a = conv(ca)
        g = conv(slice(D_FF + blk * DH, D_FF + (blk + 1) * DH))
        act_s[:, ca] = (a * _sigmoid(a) * g).astype(BF16)
    tail = up_s[T + lo:T + FFN_LEAD, :]
    ffn_ref[0] = tail
    up_s[lo:FFN_LEAD, :] = tail
    x2 = x + _mm(act_s[...], wdown_ref[...])
    y_ref[0] = _rmsnorm(x2, fing_ref[...])


def _prompt_ffn(x1, fg, wup, fw, fb, wdown, fing):
    B, S, D = x1.shape
    T = FFN_CHUNK
    nc = S // T
    return pl.pallas_call(
        _prompt_ffn_kernel,
        grid=(B, nc),
        in_specs=[
            pl.BlockSpec((1, T, D), lambda b, c: (b, c, 0)),
            _const_spec((1, D)), _const_spec((D, 2 * D_FF)), _const_spec((FFN_K, 2 * D_FF)),
            _const_spec((1, 2 * D_FF)), _const_spec((D_FF, D)), _const_spec((1, D)),
        ],
        out_specs=[
            pl.BlockSpec((1, T, D), lambda b, c: (b, c, 0)),
            pl.BlockSpec((None, 1, FFN_K - 1, 2 * D_FF), lambda b, c: (0, b, 0, 0)),
        ],
        out_shape=[
            jax.ShapeDtypeStruct((B, S, D), F32),
            jax.ShapeDtypeStruct((1, B, FFN_K - 1, 2 * D_FF), F32),
        ],
        scratch_shapes=[pltpu.VMEM((T + FFN_LEAD, 2 * D_FF), F32), pltpu.VMEM((T, D_FF), BF16)],
        compiler_params=pltpu.CompilerParams(
            dimension_semantics=("arbitrary", "arbitrary"), vmem_limit_bytes=VMEM_LIMIT),
        name="prompt_ffn",
    )(x1, fg, wup, fw, fb, wdown, fing)


def _sample_proj_kernel(x_ref, mixg_ref, w_ref, z_ref):
    xn = _rmsnorm(x_ref[...], mixg_ref[...]).astype(BF16)
    for c0 in range(0, N_Z, DH):
        cs = slice(c0, min(c0 + DH, N_Z))
        z_ref[:, cs] = _mm(xn, w_ref[:, cs])


def _sample_proj(x, mixg, w_all):
    R, D = x.shape
    return pl.pallas_call(
        _sample_proj_kernel,
        grid=(R // SAMPLE_ROWS,),
        in_specs=[pl.BlockSpec((SAMPLE_ROWS, D), lambda i: (i, 0)),
                  _const_spec((1, D)), _const_spec((D, N_Z))],
        out_specs=pl.BlockSpec((SAMPLE_ROWS, N_Z), lambda i: (i, 0)),
        out_shape=jax.ShapeDtypeStruct((R, N_Z), F32),
        compiler_params=pltpu.CompilerParams(
            dimension_semantics=("arbitrary",), vmem_limit_bytes=VMEM_LIMIT),
        name="sample_proj",
    )(x, mixg, w_all)


def _sample_mlstm_kernel(zq_ref, zg_ref, bg_ref, c_hbm, n_ref, m_ref,
                         h_ref, co_ref, no_ref, mo_ref, seq_s, c_buf, c_sem, *, t_seq, n_steps):
    n_seq = zq_ref.shape[1]
    pad = LANES - SUBLANES
    wq = zq_ref.shape[2]
    step = pl.program_id(0)
    slot = step % MLSTM_RING

    def c_copy(s):
        k = s % MLSTM_RING
        return pltpu.make_async_copy(c_hbm.at[0, pl.ds(s * n_seq, n_seq)], c_buf.at[k], c_sem.at[k])

    @pl.when(step == 0)
    def _():
        for s in range(min(MLSTM_RING - 1, n_steps)):
            c_copy(s).start()

    @pl.when(step + (MLSTM_RING - 1) < n_steps)
    def _():
        c_copy(step + (MLSTM_RING - 1)).start()

    c_copy(step).wait()
    seq_s[...] = jnp.zeros_like(seq_s)
    for j in range(n_seq):
        for t in range(t_seq):
            seq_s[j, t:t + 1, 0:wq] = zq_ref[t, j:j + 1, :]
            seq_s[j, t:t + 1, wq:] = zg_ref[t, j:j + 1, :]
    items = []
    for j in range(n_seq):
        zq = seq_s[j, :, 0:wq]
        b_all, a_seq = _gate_columns(seq_s[j, :, wq:], bg_ref[...])
        a_all = jnp.concatenate([a_seq, jnp.zeros((pad, LANES), F32)], axis=0)
        a_t = a_all.T
        for h in range(N_HEADS):
            q = zq[:, Q0 + h * DH:Q0 + (h + 1) * DH]
            k = zq[:, K0 + h * DH:K0 + (h + 1) * DH] * (DH ** -0.5)
            v = zq[:, V0 + h * DH:V0 + (h + 1) * DH]
            zeros = jnp.zeros((pad, DH), F32)
            k = jnp.concatenate([k, zeros], axis=0)
            v = jnp.concatenate([v, zeros], axis=0)
            out = {}
            steps = _mlstm_head_steps(
                q, k, v, b_all[:, h:h + 1], a_all[:, h:h + 1], a_t[h:h + 1, :],
                m_ref[j:j + 1, h:h + 1], functools.partial(c_buf.__getitem__, (slot, j, h)),
                n_ref[j, h:h + 1, :], t_seq, out)
            items.append((j, h, out, steps))
    _round_robin(steps for _, _, _, steps in items)
    for j, h, out, _ in items:
        co_ref[j, h] = out["c"]
        no_ref[j, h:h + 1, :] = out["n"]
        mo_ref[j:j + 1, h:h + 1] = out["m"]
        for t in range(t_seq):
            h_ref[t, j:j + 1, h * DH:(h + 1) * DH] = out["h"][t:t + 1, :]


def _sample_mlstm(z, bg, c0, n0, m0, t_seq):
    nb = c0.shape[1]
    sb = MLSTM_SEQS
    kern = functools.partial(_sample_mlstm_kernel, t_seq=t_seq, n_steps=nb // sb)
    c_spec = pl.BlockSpec((None, sb, N_HEADS, DH, DH), lambda i: (0, i, 0, 0, 0))
    n_spec = pl.BlockSpec((None, sb, N_HEADS, DH), lambda i: (0, i, 0, 0))
    m_spec = pl.BlockSpec((None, sb, N_HEADS), lambda i: (0, i, 0))
    return pl.pallas_call(
        kern,
        grid=(nb // sb,),
        in_specs=[
            pl.BlockSpec((t_seq, sb, 3 * D_MODEL), lambda i: (0, i, 0)),
            pl.BlockSpec((t_seq, sb, LANES), lambda i: (0, i, IG0 // LANES)),
            _const_spec((1, LANES)),
            pl.BlockSpec(memory_space=pl.ANY), n_spec, m_spec,
        ],
        out_specs=[pl.BlockSpec((t_seq, sb, D_MODEL), lambda i: (0, i, 0)), c_spec, n_spec, m_spec],
        out_shape=[
            jax.ShapeDtypeStruct((t_seq, nb, D_MODEL), F32),
            jax.ShapeDtypeStruct(c0.shape, F32),
            jax.ShapeDtypeStruct(n0.shape, F32),
            jax.ShapeDtypeStruct(m0.shape, F32),
        ],
        scratch_shapes=[pltpu.VMEM((sb, SUBLANES, 3 * D_MODEL + LANES), F32),
                        pltpu.VMEM((MLSTM_RING, sb, N_HEADS, DH, DH), F32),
                        pltpu.SemaphoreType.DMA((MLSTM_RING,))],
        compiler_params=pltpu.CompilerParams(
            dimension_semantics=("arbitrary",), vmem_limit_bytes=VMEM_LIMIT),
        name="sample_mlstm",
    )(z, z, bg, c0, n0, m0)


def _rows(ref):
    return jnp.concatenate([ref[t] for t in range(ref.shape[0])], axis=0)


def _sample_mixer_tail_kernel(x_ref, z_ref, h_ref, cst_ref, hg_ref, wmo_ref, cw_ref, cb_ref,
                              lng_ref, lnb_ref, wco_ref, wout_ref,
                              x1_ref, cnew_ref, uc_s):
    t_seq, n_seq, _ = x_ref.shape
    hist = CONV_K - 1
    z = _rows(z_ref)
    hcat = _rows(h_ref)
    ha = jnp.concatenate(
        [_head_out(hcat[:, h * DH:(h + 1) * DH], hg_ref[:, h * DH:(h + 1) * DH],
                   z[:, O0 + h * DH:O0 + (h + 1) * DH]) for h in range(N_HEADS)], axis=1)
    y_a = _mm(ha.astype(BF16), wmo_ref[...])

    for t in range(t_seq):
        gv = z_ref[t, :, GV0:GV0 + D_MODEL]
        uc_s[t] = gv * jnp.tanh(z_ref[t, :, GG0:GG0 + D_MODEL]) + gv

    def plane(r):
        return cst_ref[r] if r < hist else uc_s[r - hist]

    outs = []
    for t in range(t_seq):
        acc = cb_ref[...] + cw_ref[0:1, :] * plane(t)
        for j in range(1, CONV_K):
            acc = acc + cw_ref[j:j + 1, :] * plane(t + j)
        outs.append(acc)
    for r in range(hist):
        cnew_ref[r] = plane(r + t_seq)
    uc = jnp.concatenate(outs, axis=0)

    ucn = _layernorm(uc, lng_ref[...], lnb_ref[...])
    y_b = _mm((ucn * _sigmoid(ucn)).astype(BF16), wco_ref[...])
    mix = _sigmoid_of_twice(z[:, GA0:GA0 + D_MODEL]) * y_a + _sigmoid_of_twice(z[:, GB0:GB0 + D_MODEL]) * y_b
    x1 = _rows(x_ref) + _mm(mix.astype(BF16), wout_ref[...])
    for t in range(t_seq):
        x1_ref[t] = x1[t * n_seq:(t + 1) * n_seq, :]


def _sample_mixer_tail(x, z, h, cst, hg, wmo, cw, cb, lng, lnb, wco, wout):
    t_seq, nb, D = x.shape
    sb = SAMPLE_ROWS // t_seq
    hist = CONV_K - 1
    st_spec = pl.BlockSpec((None, hist, sb, D), lambda i: (0, 0, i, 0))
    return pl.pallas_call(
        _sample_mixer_tail_kernel,
        grid=(nb // sb,),
        in_specs=[
            pl.BlockSpec((t_seq, sb, D), lambda i: (0, i, 0)),
            pl.BlockSpec((t_seq, sb, N_Z), lambda i: (0, i, 0)),
            pl.BlockSpec((t_seq, sb, D), lambda i: (0, i, 0)),
            st_spec,
            _const_spec((1, D)), _const_spec((D, D)), _const_spec((CONV_K, D)), _const_spec((1, D)),
            _const_spec((1, D)), _const_spec((1, D)), _const_spec((D, D)), _const_spec((D, D)),
        ],
        out_specs=[pl.BlockSpec((t_seq, sb, D), lambda i: (0, i, 0)), st_spec],
        out_shape=[
            jax.ShapeDtypeStruct((t_seq, nb, D), F32),
            jax.ShapeDtypeStruct(cst.shape, F32),
        ],
        scratch_shapes=[pltpu.VMEM((t_seq, sb, D), F32)],
        compiler_params=pltpu.CompilerParams(
            dimension_semantics=("arbitrary",), vmem_limit_bytes=VMEM_LIMIT),
        name="sample_mixer_tail",
    )(x, z, h, cst, hg, wmo, cw, cb, lng, lnb, wco, wout)


def _sample_ffn_kernel(x_ref, fst_ref, fg_ref, wup_ref, fw_ref, fb_ref, wdown_ref, fing_ref,
                       y_ref, fnew_ref, up_s, upc_s):
    t_seq, n_seq, _ = x_ref.shape
    hist = FFN_K - 1
    x = _rows(x_ref)
    hf = _rmsnorm(x, fg_ref[...]).astype(BF16)
    for blk in range(2 * D_FF // DH):
        cs = slice(blk * DH, (blk + 1) * DH)
        up = _mm(hf, wup_ref[:, cs])
        for t in range(t_seq):
            up_s[hist + t, :, cs] = up[t * n_seq:(t + 1) * n_seq, :]
    for j in range(n_seq):
        for r in range(hist):
            up_s[r, j:j + 1, :] = fst_ref[j, r:r + 1, :]
    for t in range(t_seq):
        acc = fb_ref[...] + fw_ref[0:1, :] * up_s[t]
        for kk in range(1, FFN_K):
            acc = acc + fw_ref[kk:kk + 1, :] * up_s[t + kk]
        upc_s[t] = acc
    for j in range(n_seq):
        for r in range(hist):
            fnew_ref[j, r:r + 1, :] = up_s[t_seq + r, j:j + 1, :]
    upc = _rows(upc_s)
    y = _ffn_tail(upc[:, 0:D_FF], upc[:, D_FF:2 * D_FF], wdown_ref, x, fing_ref)
    for t in range(t_seq):
        y_ref[t] = y[t * n_seq:(t + 1) * n_seq, :]


def _sample_ffn(x1, fst, fg, wup, fw, fb, wdown, fing):
    t_seq, nb, D = x1.shape
    sb = SAMPLE_ROWS // t_seq
    hist = FFN_K - 1
    st_spec = pl.BlockSpec((None, sb, hist, 2 * D_FF), lambda i: (0, i, 0, 0))
    return pl.pallas_call(
        _sample_ffn_kernel,
        grid=(nb // sb,),
        in_specs=[
            pl.BlockSpec((t_seq, sb, D), lambda i: (0, i, 0)),
            st_spec,
            _const_spec((1, D)), _const_spec((D, 2 * D_FF)), _const_spec((FFN_K, 2 * D_FF)),
            _const_spec((1, 2 * D_FF)), _const_spec((D_FF, D)), _const_spec((1, D)),
        ],
        out_specs=[pl.BlockSpec((t_seq, sb, D), lambda i: (0, i, 0)), st_spec],
        out_shape=[
            jax.ShapeDtypeStruct((t_seq, nb, D), F32),
            jax.ShapeDtypeStruct(fst.shape, F32),
        ],
        scratch_shapes=[pltpu.VMEM((t_seq + hist, sb, 2 * D_FF), F32), pltpu.VMEM((t_seq, sb, 2 * D_FF), F32)],
        compiler_params=pltpu.CompilerParams(
            dimension_semantics=("arbitrary",), vmem_limit_bytes=VMEM_LIMIT),
        name="sample_ffn",
    )(x1, fst, fg, wup, fw, fb, wdown, fing)


def _pad_gate_bias(b):
    return jnp.pad(b, (0, LANES - b.shape[0])).reshape(1, LANES)


def kernel(x_prompt, x_sample, state_C, state_n, state_m, state_conv, state_ffn, mix_norm_g, w_in, b_if,
           head_norm_g, w_mlstm_out, conv_w, conv_b, conv_ln_g, conv_ln_b, w_conv_out, w_out, ffn_norm_g,
           w_up, ffn_conv_w, ffn_conv_b, w_down, final_norm_g):
    depth = w_in.shape[0]
    assert depth == 1, "single-layer trunk"
    l = 0
    row = lambda a: a.reshape(1, -1)
    w_all = _pack_w_in(jnp.swapaxes(w_in, 1, 2))
    bg = _pad_gate_bias(b_if[l])
    mixg, hg = row(mix_norm_g[l]), row(head_norm_g[l])
    wmo, wco, wout = (w.astype(BF16) for w in (w_mlstm_out[l], w_conv_out[l], w_out[l]))
    cw, cb = conv_w[l], row(conv_b[l])
    lng, lnb = row(conv_ln_g[l]), row(conv_ln_b[l])
    fg, fing = row(ffn_norm_g[l]), row(final_norm_g)
    wup, wdown = w_up[l].astype(BF16), w_down[l].astype(BF16)
    fw, fb = ffn_conv_w[l], row(ffn_conv_b[l])

    x1p, c_p, n_p, m_p, conv_p = _prompt_mixer(x_prompt, mixg, w_all, bg, hg, wmo, cw, cb, lng, lnb, wco, wout)
    y_p, ffn_p = _prompt_ffn(x1p, fg, wup, fw, fb, wdown, fing)

    nb, t_seq, d = x_sample.shape
    xs = jnp.swapaxes(x_sample, 0, 1)
    z = _sample_proj(xs.reshape(t_seq * nb, d), mixg, w_all).reshape(t_seq, nb, N_Z)
    h_s, c_s, n_s, m_s = _sample_mlstm(z, bg, state_C, state_n, state_m, t_seq)
    x1s, conv_t = _sample_mixer_tail(xs, z, h_s, jnp.swapaxes(state_conv, 1, 2), hg, wmo, cw, cb, lng, lnb,
                                     wco, wout)
    conv_s = jnp.swapaxes(conv_t, 1, 2)
    y_t, ffn_s = _sample_ffn(x1s, state_ffn, fg, wup, fw, fb, wdown, fing)
    y_s = jnp.swapaxes(y_t, 0, 1)

    return (y_p, y_s,
            c_p, n_p, m_p.reshape(1, -1, N_HEADS), conv_p, ffn_p,
            c_s, n_s, m_s, conv_s, ffn_s)
```
